```python
import jax
import jax.numpy as jnp
from jax import lax
import numpy as np

D_MODEL = 1024
BATCH = 4
SEQ = 4096
DEPTH = 2

GRID_W = 64
CTX_LEN = 256
HEAD_DIM = 64
NA_HEADS = 8
NA_WIN_R = 8
NA_WIN_C = 16
NA_QBLK_C = 16
NA_KBAND_C = 32
FN_GROUPS = 8
FN_GROUP_DIM = 64
WA_HEADS = 8
WA_KV_HEADS = 2
WA_WINDOW = 128
WA_BLOCK = 128
D_FF = 2816
N_BRANCH = 3
N_MOD = 9
ROPE_BASE = 10000.0
EPS = 1e-6
NEG_INF = -1e30

NA_W = NA_HEADS * HEAD_DIM
FN_W = FN_GROUPS * FN_GROUP_DIM
WA_QW = WA_HEADS * HEAD_DIM
WA_KVW = WA_KV_HEADS * HEAD_DIM
BRANCH_W = NA_W
O_NA_K = NA_W
O_NA_V = 2 * NA_W
O_FN = 3 * NA_W
O_WA_Q = O_FN + FN_W
O_WA_K = O_WA_Q + WA_QW
O_WA_V = O_WA_K + WA_KVW
O_GATE = O_WA_V + WA_KVW
P_IN = O_GATE + N_BRANCH * D_MODEL

kernel_name = "hybrid_natten_fnet_swa_prefix_dit"


def rmsnorm(x, g):
    xf = x.astype(jnp.float32)
    y = xf * lax.rsqrt(jnp.mean(xf * xf, axis=-1, keepdims=True) + EPS)
    return (y * g.astype(jnp.float32)).astype(x.dtype)


def modulate(x, shift, scale):
    return x * (1 + scale) + shift


def swiglu(u, w13, w2):
    a, b = jnp.split(u @ w13, 2, axis=-1)
    return (jax.nn.silu(a) * b) @ w2


def ffn_half_step(h, g, shift, scale, gate, w13, w2):
    return h + 0.5 * gate * swiglu(modulate(rmsnorm(h, g), shift, scale), w13, w2)


def heads(t, n):
    return t.reshape(*t.shape[:-1], n, HEAD_DIM)


def axial_rope(n_tok):
    t = jnp.arange(n_tok)
    row = (t // GRID_W).astype(jnp.float32)
    col = (t % GRID_W).astype(jnp.float32)
    n_freq = HEAD_DIM // 4
    inv = ROPE_BASE ** (-jnp.arange(n_freq, dtype=jnp.float32) / n_freq)
    ang = jnp.concatenate([row[:, None] * inv, col[:, None] * inv], axis=-1)
    return jnp.cos(ang), jnp.sin(ang)


def apply_rope(t, cos, sin):
    tf = t.astype(jnp.float32)
    t1, t2 = jnp.split(tf, 2, axis=-1)
    c = cos[:, None, :]
    s = sin[:, None, :]
    return jnp.concatenate([t1 * c - t2 * s, t1 * s + t2 * c], axis=-1).astype(t.dtype)


def split_projection(p):
    return jnp.split(p, [O_NA_K, O_NA_V, O_FN, O_WA_Q, O_WA_K, O_WA_V, O_GATE], axis=-1)


def neighbourhood_attention(q, k, v, kc, vc, bias_tab, rows):
    B, S, H, Dh = q.shape
    wr = min(NA_WIN_R, rows)
    n_cb = GRID_W // NA_QBLK_C
    r = jnp.arange(rows)
    row_start = jnp.clip(r - wr // 2, 0, rows - wr)
    key_rows = row_start[:, None] + jnp.arange(wr)
    cb = jnp.arange(n_cb)
    band_start = jnp.clip(cb * NA_QBLK_C - NA_WIN_C // 2, 0, GRID_W - NA_KBAND_C)
    key_cols = band_start[:, None] + jnp.arange(NA_KBAND_C)
    q_cols = cb[:, None] * NA_QBLK_C + jnp.arange(NA_QBLK_C)
    win_start = jnp.clip(q_cols - NA_WIN_C // 2, 0, GRID_W - NA_WIN_C)
    kcb = key_cols[:, None, :]
    in_win = (kcb >= win_start[..., None]) & (kcb < win_start[..., None] + NA_WIN_C)
    dr_idx = key_rows - r[:, None] + NA_WIN_R - 1
    dc_idx = jnp.clip(kcb - q_cols[..., None], -(NA_WIN_C - 1), NA_WIN_C - 1) + NA_WIN_C - 1
    bias = bias_tab[:, dr_idx[:, None, None, :, None], dc_idx[None, :, :, None, :]]
    bias = bias.astype(jnp.float32)

    kg = k.reshape(B, rows, GRID_W, H, Dh)
    vg = v.reshape(B, rows, GRID_W, H, Dh)
    ridx = key_rows[:, None, :, None]
    cidx = key_cols[None, :, None, :]
    kb = kg[:, ridx, cidx]
    vb = vg[:, ridx, cidx]
    qg = q.reshape(B, rows, n_cb, NA_QBLK_C, H, Dh)
    scale = Dh ** -0.5
    s_loc = jnp.einsum('brcqhd,brcwkhd->bhrcqwk', qg, kb).astype(jnp.float32) * scale + bias[None]
    s_loc = jnp.where(in_win[:, :, None, :], s_loc, NEG_INF)
    s_loc = s_loc.reshape(*s_loc.shape[:5], wr * NA_KBAND_C)
    s_ctx = jnp.einsum('brcqhd,blhd->bhrcql', qg, kc).astype(jnp.float32) * scale
    n_loc = wr * NA_KBAND_C
    p = jax.nn.softmax(jnp.concatenate([s_loc, s_ctx], axis=-1), axis=-1).astype(v.dtype)
    p_loc = p[..., :n_loc].reshape(*p.shape[:5], wr, NA_KBAND_C)
    p_ctx = p[..., n_loc:]
    o = jnp.einsum('bhrcqwk,brcwkhd->brcqhd', p_loc, vb) + jnp.einsum('bhrcql,blhd->brcqhd', p_ctx, vc)
    return o.reshape(B, S, H * Dh)


def window_gqa_attention(q, k, v, kc, vc, sink):
    B, S, Hq, Dh = q.shape
    Hkv = k.shape[2]
    G = Hq // Hkv
    nb = S // WA_BLOCK
    qb = q.reshape(B, nb, WA_BLOCK, Hkv, G, Dh)
    pad = ((0, 0), (WA_BLOCK, WA_BLOCK), (0, 0), (0, 0))
    kp = jnp.pad(k, pad)
    vp = jnp.pad(v, pad)
    idx = jnp.arange(nb)[:, None] * WA_BLOCK + jnp.arange(3 * WA_BLOCK)
    kb = kp[:, idx]
    vb = vp[:, idx]
    qpos = jnp.arange(S).reshape(nb, WA_BLOCK)
    kpos = (idx - WA_BLOCK)[:, None, :]
    valid = (jnp.abs(kpos - qpos[..., None]) <= WA_WINDOW) & (kpos >= 0) & (kpos < S)
    scale = Dh ** -0.5
    s_loc = jnp.einsum('bnqkgd,bnjkd->bkgnqj', qb, kb).astype(jnp.float32) * scale
    s_loc = jnp.where(valid, s_loc, NEG_INF)
    s_ctx = jnp.einsum('bnqkgd,blkd->bkgnql', qb, kc).astype(jnp.float32) * scale
    s_sink = jnp.broadcast_to(sink.astype(jnp.float32).reshape(1, Hkv, G, 1, 1, 1), s_loc.shape[:-1] + (1,))
    p = jax.nn.softmax(jnp.concatenate([s_loc, s_ctx, s_sink], axis=-1), axis=-1).astype(v.dtype)
    n_loc = 3 * WA_BLOCK
    n_ctx = kc.shape[1]
    o = (jnp.einsum('bkgnqj,bnjkd->bnqkgd', p[..., :n_loc], vb)
         + jnp.einsum('bkgnql,blkd->bnqkgd', p[..., n_loc:n_loc + n_ctx], vc))
    return o.reshape(B, S, Hq * Dh)


def context_attention(q, k, v, sink=None):
    B, L, Hq, Dh = q.shape
    Hkv = k.shape[2]
    G = Hq // Hkv
    qg = q.reshape(B, L, Hkv, G, Dh)
    s = jnp.einsum('bqkgd,bjkd->bkgqj', qg, k).astype(jnp.float32) * (Dh ** -0.5)
    if sink is None:
        p = jax.nn.softmax(s, axis=-1)
    else:
        s_sink = jnp.broadcast_to(sink.astype(jnp.float32).reshape(1, Hkv, G, 1, 1), s.shape[:-1] + (1,))
        p = jax.nn.softmax(jnp.concatenate([s, s_sink], axis=-1), axis=-1)[..., :L]
    o = jnp.einsum('bkgqj,bjkd->bqkgd', p.astype(v.dtype), v)
    return o.reshape(B, L, Hq * Dh)


def fourier_mix(u):
    B, N, _ = u.shape
    ug = u.reshape(B, N, FN_GROUPS, FN_GROUP_DIM).astype(jnp.float32)
    f = jnp.fft.fft2(ug, axes=(1, 3), norm='ortho').real
    return f.reshape(B, N, FN_W).astype(u.dtype)


def merge_branches(a, f, w, gate_logits, w_br, w_out):
    outs = jnp.stack([a, f, w], axis=-2)
    y = jnp.einsum('bnic,icd->bnid', outs, w_br)
    g = jax.nn.sigmoid(gate_logits.reshape(*gate_logits.shape[:-1], N_BRANCH, D_MODEL))
    return jnp.sum(g * y, axis=-2) @ w_out


def setup_inputs(seed: int = 0) -> dict:
    key = jax.random.key(seed)
    ks = jax.random.split(key, 19)
    D = D_MODEL

    def nrm(k, shape, scale):
        return jax.random.normal(k, shape, jnp.float32) * scale

    return {
        "x": nrm(ks[0], (BATCH, SEQ, D), 1.0),
        "c": nrm(ks[1], (BATCH, D), 1.0),
        "ctx": nrm(ks[2], (BATCH, CTX_LEN, D), 1.0),
        "c_ctx": nrm(ks[3], (D,), 1.0),
        "w_ada": nrm(ks[4], (DEPTH, D, N_MOD * D), D ** -0.5),
        "b_ada": nrm(ks[5], (DEPTH, N_MOD * D), 0.02),
        "g_ffn1": 1.0 + nrm(ks[6], (DEPTH, D), 0.02),
        "ffn1_w13": nrm(ks[7], (DEPTH, D, 2 * D_FF), D ** -0.5),
        "ffn1_w2": nrm(ks[8], (DEPTH, D_FF, D), D_FF ** -0.5),
        "g_mix": 1.0 + nrm(ks[9], (DEPTH, D), 0.02),
        "w_in": nrm(ks[10], (DEPTH, D, P_IN), D ** -0.5),
        "na_bias": nrm(ks[11], (DEPTH, NA_HEADS, 2 * NA_WIN_R - 1, 2 * NA_WIN_C - 1), 0.1),
        "wa_sink": nrm(ks[12], (DEPTH, WA_HEADS), 1.0),
        "w_br": nrm(ks[13], (DEPTH, N_BRANCH, BRANCH_W, D), BRANCH_W ** -0.5),
        "w_out": nrm(ks[14], (DEPTH, D, D), D ** -0.5),
        "g_ffn2": 1.0 + nrm(ks[15], (DEPTH, D), 0.02),
        "ffn2_w13": nrm(ks[16], (DEPTH, D, 2 * D_FF), D ** -0.5),
        "ffn2_w2": nrm(ks[17], (DEPTH, D_FF, D), D_FF ** -0.5),
        "g_final": 1.0 + nrm(ks[18], (D,), 0.02),
    }


def reference(x, c, ctx, c_ctx, w_ada, b_ada, g_ffn1, ffn1_w13, ffn1_w2, g_mix, w_in, na_bias, wa_sink,
              w_br, w_out, g_ffn2, ffn2_w13, ffn2_w2, g_final):
    S = x.shape[1]
    rows = S // GRID_W
    cos, sin = axial_rope(S)
    h, hc = x, ctx
    for l in range(DEPTH):
        last = l == DEPTH - 1
        mod = (jax.nn.silu(c) @ w_ada[l] + b_ada[l])[:, None, :]
        mod_c = jax.nn.silu(c_ctx) @ w_ada[l] + b_ada[l]
        sh1, sc1, gt1, sh2, sc2, gt2, sh3, sc3, gt3 = jnp.split(mod, N_MOD, axis=-1)
        csh1, csc1, cgt1, csh2, csc2, cgt2, csh3, csc3, cgt3 = jnp.split(mod_c, N_MOD, axis=-1)

        h = ffn_half_step(h, g_ffn1[l], sh1, sc1, gt1, ffn1_w13[l], ffn1_w2[l])
        hc = ffn_half_step(hc, g_ffn1[l], csh1, csc1, cgt1, ffn1_w13[l], ffn1_w2[l])

        u = modulate(rmsnorm(h, g_mix[l]), sh2, sc2)
        uc = modulate(rmsnorm(hc, g_mix[l]), csh2, csc2)
        nq, nk, nv, fu, wq, wk, wv, gl = split_projection(u @ w_in[l])
        if last:
            cnk, cnv = jnp.split(uc @ w_in[l][:, O_NA_K:O_FN], 2, axis=-1)
            cwk, cwv = jnp.split(uc @ w_in[l][:, O_WA_K:O_GATE], 2, axis=-1)
        else:
            cnq, cnk, cnv, cfu, cwq, cwk, cwv, cgl = split_projection(uc @ w_in[l])
        nk_c, nv_c = heads(cnk, NA_HEADS), heads(cnv, NA_HEADS)
        wk_c, wv_c = heads(cwk, WA_KV_HEADS), heads(cwv, WA_KV_HEADS)

        a = neighbourhood_attention(heads(nq, NA_HEADS), heads(nk, NA_HEADS), heads(nv, NA_HEADS),
                                    nk_c, nv_c, na_bias[l], rows)
        f = fourier_mix(fu)
        w = window_gqa_attention(apply_rope(heads(wq, WA_HEADS), cos, sin),
                                 apply_rope(heads(wk, WA_KV_HEADS), cos, sin),
                                 heads(wv, WA_KV_HEADS), wk_c, wv_c, wa_sink[l])
        h = h + gt2 * merge_branches(a, f, w, gl, w_br[l], w_out[l])

        if not last:
            ca = context_attention(heads(cnq, NA_HEADS), nk_c, nv_c)
            cf = fourier_mix(cfu)
            cw = context_attention(heads(cwq, WA_HEADS), wk_c, wv_c, wa_sink[l])
            hc = hc + cgt2 * merge_branches(ca, cf, cw, cgl, w_br[l], w_out[l])
            hc = ffn_half_step(hc, g_ffn2[l], csh3, csc3, cgt3, ffn2_w13[l], ffn2_w2[l])

        h = ffn_half_step(h, g_ffn2[l], sh3, sc3, gt3, ffn2_w13[l], ffn2_w2[l])
    return rmsnorm(h, g_final)
```

```python
import functools

import numpy as np
import jax
import jax.numpy as jnp
from jax import lax
from jax.experimental import pallas as pl
from jax.experimental.pallas import tpu as pltpu

D = 1024
GRID_W = 64
HD = 64
NA_HEADS = 8
NA_WIN_R = 8
NA_WIN_C = 16
FN_GROUPS = 8
FN_GROUP_DIM = 64
WA_HEADS = 8
WA_KV_HEADS = 2
WA_WINDOW = 128
D_FF = 2816
N_MOD = 9
ROPE_BASE = 10000.0
EPS = 1e-6
NEG_INF = -1e30
BW = 512

LANES = 128
MXU_DIM = 256
TM = 512
TF = MXU_DIM
NCH = D_FF // TF
VMEM_LIMIT = 56 * 1024 * 1024

NA_QR = 4
NA_KR = 12
NA_Q = NA_QR * GRID_W
NA_K = NA_KR * GRID_W
WA_Q = 128
WA_K = 3 * WA_Q

F32 = jnp.float32
BF16 = jnp.bfloat16


def _cparams(sem):
    return pltpu.CompilerParams(dimension_semantics=sem, vmem_limit_bytes=VMEM_LIMIT)


def _const_spec(shape):
    nd = len(shape)
    return pl.BlockSpec(shape, lambda *_: (0,) * nd, pipeline_mode=pl.Buffered(1))


def _sigmoid(x):
    return 1.0 / (1.0 + jnp.exp(-x))


def _norm_mod(x, g, shift, scale):
    y = x * lax.rsqrt(jnp.mean(x * x, axis=-1, keepdims=True) + EPS) * g
    return y * (1.0 + scale) + shift


def _dot(a, b):
    return jnp.dot(a, b, preferred_element_type=F32)


def _dot_nt(a, b):
    return lax.dot_general(a, b, (((1,), (1,)), ((), ())), preferred_element_type=F32)


def _ada_kernel(c_ref, w_ref, b_ref, o_ref):
    x = c_ref[...]
    sx = (x * _sigmoid(x)).astype(BF16)
    o_ref[...] = _dot(sx, w_ref[...].astype(BF16)) + b_ref[...]


def _ada(cc, w_ada, b_ada):
    depth = w_ada.shape[0]
    n = w_ada.shape[2]
    tn = 1536
    rows = cc.shape[0]
    return pl.pallas_call(
        _ada_kernel,
        grid=(depth, n // tn),
        in_specs=[
            pl.BlockSpec((rows, D), lambda l, j: (0, 0)),
            pl.BlockSpec((None, D, tn), lambda l, j: (l, 0, j)),
            pl.BlockSpec((None, 1, tn), lambda l, j: (l, 0, j)),
        ],
        out_specs=pl.BlockSpec((None, rows, tn), lambda l, j: (l, 0, j)),
        out_shape=jax.ShapeDtypeStruct((depth, rows, n), F32),
        compiler_params=_cparams(("arbitrary", "arbitrary")),
        name="ada",
    )(cc, w_ada, b_ada.reshape(depth, 1, n))


def _mod_spec(layer, k, n_lat_tiles, tiles_per_batch, n_batch, mod_rows):
    def idx(i):
        row = jnp.where(i < n_lat_tiles, i // tiles_per_batch, n_batch)
        return (layer * mod_rows + row, 0, k)

    return pl.BlockSpec((None, 1, D), idx)


def _ffn_kernel(x_ref, g_ref, sh_ref, sc_ref, gt_ref, w13_ref, w2_ref, *rest, final):
    if final:
        gf_ref, o_ref, hm_ref = rest
    else:
        o_ref, hm_ref = rest
    x = x_ref[...]
    u = _norm_mod(x, g_ref[...], sh_ref[...], sc_ref[...]).astype(BF16)
    for c in range(NCH):
        ab = _dot(u, w13_ref[c])
        a = ab[:, :TF]
        b = ab[:, TF:]
        hm_ref[:, c * TF:(c + 1) * TF] = (a * _sigmoid(a) * b).astype(BF16)
    f = _dot(hm_ref[...], w2_ref[...])
    out = x + 0.5 * gt_ref[...] * f
    if final:
        out = out * lax.rsqrt(jnp.mean(out * out, axis=-1, keepdims=True) + EPS) * gf_ref[...]
    o_ref[...] = out


def _ffn(h, n_tiles, g, mod3, layer, mod_k, w13c, w2, tile_info, g_final=None):
    final = g_final is not None
    tile_spec = pl.BlockSpec((TM, D), lambda i: (i, 0))
    vec_spec = pl.BlockSpec((1, D), lambda i: (0, 0))
    in_specs = [
        tile_spec,
        vec_spec,
        _mod_spec(layer, mod_k, *tile_info),
        _mod_spec(layer, mod_k + 1, *tile_info),
        _mod_spec(layer, mod_k + 2, *tile_info),
        _const_spec(w13c.shape),
        _const_spec(w2.shape),
    ]
    args = [h, g.reshape(1, D), mod3, mod3, mod3, w13c, w2]
    if final:
        in_specs.append(vec_spec)
        args.append(g_final.reshape(1, D))
    return pl.pallas_call(
        functools.partial(_ffn_kernel, final=final),
        grid=(n_tiles,),
        in_specs=in_specs,
        out_specs=tile_spec,
        out_shape=jax.ShapeDtypeStruct((n_tiles * TM, D), F32),
        scratch_shapes=[pltpu.VMEM((TM, D_FF), BF16)],
        compiler_params=_cparams(("arbitrary",)),
        name="ffn_final" if final else "ffn",
    )(*args)


PROJ_W = 6 * BW


def _rope(t, cos, sin_signed, first_half):
    partner = jnp.where(first_half, pltpu.roll(t, LANES - HD // 2, axis=1), pltpu.roll(t, HD // 2, axis=1))
    return t * cos + partner * sin_signed


def _proj_kernel(x_ref, g_ref, sh_ref, sc_ref, w_ref, cos_ref, sin_ref,
                 qn_ref, kn_ref, vn_ref, fu_ref, qw_ref, kw_ref, vw_ref):
    u = _norm_mod(x_ref[...], g_ref[...], sh_ref[...], sc_ref[...]).astype(BF16)
    scale = HD ** -0.5
    qn_ref[...] = (_dot(u, w_ref[:, 0:BW]) * scale).astype(BF16)
    kn_ref[...] = _dot(u, w_ref[:, BW:2 * BW]).astype(BF16)
    vn_ref[...] = _dot(u, w_ref[:, 2 * BW:3 * BW]).astype(BF16)
    fu_ref[...] = _dot(u, w_ref[:, 3 * BW:4 * BW]).astype(BF16)
    cos = cos_ref[...]
    sin = sin_ref[...]
    lane = lax.broadcasted_iota(jnp.int32, (TM, LANES), 1)
    first_half = (lane & (HD - 1)) < (HD // 2)
    wq = _dot(u, w_ref[:, 4 * BW:5 * BW])
    for j in range(BW // LANES):
        sl = slice(j * LANES, (j + 1) * LANES)
        qw_ref[:, sl] = (_rope(wq[:, sl], cos, sin, first_half) * scale).astype(BF16)
    wk = _dot(u, w_ref[:, 5 * BW:5 * BW + 2 * LANES])
    for j in range(2):
        sl = slice(j * LANES, (j + 1) * LANES)
        kw_ref[:, sl] = _rope(wk[:, sl], cos, sin, first_half).astype(BF16)
    vw_ref[...] = _dot(u, w_ref[:, 5 * BW + 2 * LANES:PROJ_W]).astype(BF16)


def _proj(h, n_tiles, g, mod3, layer, w_proj, cos_t, sin_t, tile_info):
    n_lat_tiles, tiles_per_batch, _, _ = tile_info
    tile_spec = pl.BlockSpec((TM, D), lambda i: (i, 0))
    vec_spec = pl.BlockSpec((1, D), lambda i: (0, 0))
    rope_spec = pl.BlockSpec((TM, LANES), lambda i: (jnp.where(i < n_lat_tiles, i % tiles_per_batch, tiles_per_batch), 0))
    rows = n_tiles * TM

    def out_spec(w):
        return pl.BlockSpec((TM, w), lambda i: (i, 0))

    widths = [BW, BW, BW, BW, BW, 2 * LANES, 2 * LANES]
    return pl.pallas_call(
        _proj_kernel,
        grid=(n_tiles,),
        in_specs=[tile_spec, vec_spec,
                  _mod_spec(layer, 3, *tile_info), _mod_spec(layer, 4, *tile_info),
                  _const_spec(w_proj.shape), rope_spec, rope_spec],
        out_specs=[out_spec(w) for w in widths],
        out_shape=[jax.ShapeDtypeStruct((rows, w), BF16) for w in widths],
        compiler_params=_cparams(("arbitrary",)),
        name="proj",
    )(h, g.reshape(1, D), mod3, mod3, w_proj, cos_t, sin_t)


def _attend(qs, key_sets, extra_logit=None):
    scores = []
    for k, _, bias in key_sets:
        s = _dot_nt(qs, k)
        if bias is not None:
            s = s + bias
        scores.append(s)
    m = scores[0].max(axis=-1, keepdims=True)
    for s in scores[1:]:
        m = jnp.maximum(m, s.max(axis=-1, keepdims=True))
    if extra_logit is not None:
        m = jnp.maximum(m, extra_logit)
    den = None
    acc = None
    for s, (_, v, _) in zip(scores, key_sets):
        p = jnp.exp(s - m)
        d = p.sum(axis=-1, keepdims=True)
        o = _dot(p.astype(BF16), v)
        den = d if den is None else den + d
        acc = o if acc is None else acc + o
    if extra_logit is not None:
        den = den + jnp.exp(extra_logit - m)
    return acc / den


def _stack_heads(q, rows):
    lane = lax.broadcasted_iota(jnp.int32, (rows, LANES), 1)
    lo = lane < HD
    zero = jnp.zeros_like(q)
    return jnp.concatenate([jnp.where(lo, q, zero), jnp.where(lo, zero, q)], axis=0), lo


def _na_kernel(q_ref, k_ref, v_ref, kc_ref, vc_ref, bias_ref, o_ref, *, rows):
    j = pl.program_id(2)
    krow = jnp.clip(NA_QR * j - NA_WIN_R // 2, 0, rows - NA_KR)
    kstart = pl.multiple_of(krow * GRID_W, GRID_W)
    kb = k_ref[pl.ds(kstart, NA_K), :]
    vb = v_ref[pl.ds(kstart, NA_K), :]
    qs, lo = _stack_heads(q_ref[...], NA_Q)
    o = _attend(qs, [(kb, vb, bias_ref[...]), (kc_ref[...], vc_ref[...], None)])
    o_ref[...] = jnp.where(lo, o[:NA_Q], o[NA_Q:]).astype(BF16)


def _na(qn, kn, vn, bias_tab, n_batch, seq, ctx_len):
    rows = seq // GRID_W
    n_blk = rows // NA_QR
    n_pair = NA_HEADS // 2
    ctx_blk0 = (n_batch * seq) // ctx_len

    def pat(j):
        return jnp.where(j == 0, 0, jnp.where(j == n_blk - 1, 2, 1))

    return pl.pallas_call(
        functools.partial(_na_kernel, rows=rows),
        grid=(n_batch, n_pair, n_blk),
        in_specs=[
            pl.BlockSpec((NA_Q, LANES), lambda b, p, j: (b * n_blk + j, p)),
            pl.BlockSpec((seq, LANES), lambda b, p, j: (b, p)),
            pl.BlockSpec((seq, LANES), lambda b, p, j: (b, p)),
            pl.BlockSpec((ctx_len, LANES), lambda b, p, j: (ctx_blk0 + b, p)),
            pl.BlockSpec((ctx_len, LANES), lambda b, p, j: (ctx_blk0 + b, p)),
            pl.BlockSpec((None, None, 2 * NA_Q, NA_K), lambda b, p, j: (p, pat(j), 0, 0)),
        ],
        out_specs=pl.BlockSpec((NA_Q, LANES), lambda b, p, j: (b * n_blk + j, p)),
        out_shape=jax.ShapeDtypeStruct((n_batch * seq, BW), BF16),
        compiler_params=_cparams(("arbitrary", "arbitrary", "arbitrary")),
        name="na",
    )(qn, kn, vn, kn, vn, bias_tab)


def _na_bias_table(bias, rows):
    n_blk = rows // NA_QR
    i = np.arange(NA_QR)[:, None, None, None]
    qc = np.arange(GRID_W)[None, :, None, None]
    t = np.arange(NA_KR)[None, None, :, None]
    kc = np.arange(GRID_W)[None, None, None, :]
    ws = np.clip(qc - NA_WIN_C // 2, 0, GRID_W - NA_WIN_C)
    col_ok = (kc >= ws) & (kc < ws + NA_WIN_C)
    dc = np.clip(kc - qc, -(NA_WIN_C - 1), NA_WIN_C - 1) + NA_WIN_C - 1
    dr_l, ok_l = [], []
    for blk in (0, 1, n_blk - 1):
        r = NA_QR * blk + i
        krow = min(max(NA_QR * blk - NA_WIN_R // 2, 0), rows - NA_KR) + t
        rs = np.clip(r - NA_WIN_R // 2, 0, rows - NA_WIN_R)
        row_ok = (krow >= rs) & (krow < rs + NA_WIN_R)
        dr_l.append(np.clip(krow - r + NA_WIN_R - 1, 0, 2 * NA_WIN_R - 2))
        ok_l.append(row_ok & col_ok)
    shape = (3, NA_QR, GRID_W, NA_KR, GRID_W)
    dr = np.stack([np.broadcast_to(a, shape[1:]) for a in dr_l]).reshape(3, NA_Q, NA_K)
    ok = np.stack([np.broadcast_to(a, shape[1:]) for a in ok_l]).reshape(3, NA_Q, NA_K)
    dcb = np.broadcast_to(dc, shape).reshape(3, NA_Q, NA_K)
    tab = jnp.where(ok[None], bias.astype(F32)[:, dr, dcb], NEG_INF)
    h = bias.shape[0]
    tab = tab.reshape(h // 2, 2, 3, NA_Q, NA_K).transpose(0, 2, 1, 3, 4)
    return tab.reshape(h // 2, 3, 2 * NA_Q, NA_K)


def _stack_group(q):
    rows = q.shape[0]
    a, lo = _stack_heads(q[:, :LANES], rows)
    b, _ = _stack_heads(q[:, LANES:], rows)
    return jnp.concatenate([a, b], axis=0), lo


def _unstack_group(o, rows, lo):
    oa = jnp.where(lo, o[0:rows], o[rows:2 * rows])
    ob = jnp.where(lo, o[2 * rows:3 * rows], o[3 * rows:4 * rows])
    return jnp.concatenate([oa, ob], axis=1)


def _sink_column(sink_ref, g, rows):
    gq = WA_HEADS // WA_KV_HEADS
    return jnp.concatenate([jnp.full((rows, 1), sink_ref[g * gq + h], F32) for h in range(gq)], axis=0)


def _wa_kernel(sink_ref, q_ref, k_ref, v_ref, kc_ref, vc_ref, o_ref, *, seq):
    g = pl.program_id(1)
    n = pl.program_id(2)
    kstart = pl.multiple_of(jnp.clip(n * WA_Q - WA_Q, 0, seq - WA_K), WA_Q)
    kb = k_ref[pl.ds(kstart, WA_K), :]
    vb = v_ref[pl.ds(kstart, WA_K), :]
    qs, lo = _stack_group(q_ref[...])
    qpos = n * WA_Q + lax.broadcasted_iota(jnp.int32, (WA_Q, WA_K), 0)
    kpos = kstart + lax.broadcasted_iota(jnp.int32, (WA_Q, WA_K), 1)
    band = jnp.where(jnp.abs(kpos - qpos) <= WA_WINDOW, 0.0, NEG_INF).astype(F32)
    band = jnp.concatenate([band] * (WA_HEADS // WA_KV_HEADS), axis=0)
    o = _attend(qs, [(kb, vb, band), (kc_ref[...], vc_ref[...], None)], _sink_column(sink_ref, g, WA_Q))
    o_ref[...] = _unstack_group(o, WA_Q, lo).astype(BF16)


def _wa(sink, qw, kw, vw, n_batch, seq, ctx_len):
    n_blk = seq // WA_Q
    ctx_blk0 = (n_batch * seq) // ctx_len
    gw = 2 * LANES
    return pl.pallas_call(
        functools.partial(_wa_kernel, seq=seq),
        grid=(n_batch, WA_KV_HEADS, n_blk),
        in_specs=[
            pl.BlockSpec(memory_space=pltpu.SMEM),
            pl.BlockSpec((WA_Q, gw), lambda b, g, n: (b * n_blk + n, g)),
            pl.BlockSpec((seq, LANES), lambda b, g, n: (b, g)),
            pl.BlockSpec((seq, LANES), lambda b, g, n: (b, g)),
            pl.BlockSpec((ctx_len, LANES), lambda b, g, n: (ctx_blk0 + b, g)),
            pl.BlockSpec((ctx_len, LANES), lambda b, g, n: (ctx_blk0 + b, g)),
        ],
        out_specs=pl.BlockSpec((WA_Q, gw), lambda b, g, n: (b * n_blk + n, g)),
        out_shape=jax.ShapeDtypeStruct((n_batch * seq, BW), BF16),
        compiler_params=_cparams(("arbitrary", "arbitrary", "arbitrary")),
        name="wa",
    )(sink, qw, kw, vw, kw, vw)


FFT_TN = 4096
FFT_KB = 8


def _fft1_kernel(f_ref, x_ref, ar_ref, ai_ref):
    r = _dot(f_ref[...], x_ref[...])
    ar_ref[...] = r[:GRID_W].astype(BF16)
    ai_ref[...] = r[GRID_W:].astype(BF16)


def _fft2_kernel(m_ref, ar_ref, ai_ref, zr_ref, zi_ref):
    for t in range(FFT_KB):
        a = jnp.concatenate([ar_ref[t], ai_ref[t]], axis=0)
        z = _dot(m_ref[t], a)
        zr_ref[:, t * BW:(t + 1) * BW] = z[:GRID_W].astype(BF16)
        zi_ref[:, t * BW:(t + 1) * BW] = z[GRID_W:].astype(BF16)


def _fft_tables():
    n = GRID_W
    k = np.arange(n)
    ang1 = 2.0 * np.pi * ((k[:, None] * k[None, :]) % n) / n
    f1 = np.concatenate([np.cos(ang1), -np.sin(ang1)], axis=0) / 8.0
    ka = k[:, None, None]
    kb = k[None, :, None]
    n1 = k[None, None, :]
    ang2 = 2.0 * np.pi * ((n1 * (ka + n * kb)) % (n * n)) / (n * n)
    mr, mi = np.cos(ang2), -np.sin(ang2)
    m2 = np.concatenate([np.concatenate([mr, -mi], axis=2), np.concatenate([mi, mr], axis=2)], axis=1) / 8.0
    c = np.arange(FN_GROUP_DIM)
    angc = 2.0 * np.pi * ((c[:, None] * c[None, :]) % FN_GROUP_DIM) / FN_GROUP_DIM
    eye = np.eye(FN_GROUPS)
    cbd = np.kron(eye, np.cos(angc)) / 8.0
    sbd = np.kron(eye, np.sin(angc)) / 8.0
    return f1, m2, cbd, sbd


def _ctx_dft_table(ctx_len):
    k = np.arange(ctx_len)
    ang = 2.0 * np.pi * ((k[:, None] * k[None, :]) % ctx_len) / ctx_len
    return np.concatenate([np.cos(ang), -np.sin(ang)], axis=0) / np.sqrt(ctx_len)


def _fft(fu, f1, m2, n_batch, seq):
    n = GRID_W
    wide = n * BW
    x2 = fu.reshape(-1, wide)
    ar, ai = pl.pallas_call(
        _fft1_kernel,
        grid=(n_batch, wide // FFT_TN),
        in_specs=[pl.BlockSpec((2 * n, n), lambda b, j: (0, 0)),
                  pl.BlockSpec((n, FFT_TN), lambda b, j: (b, j))],
        out_specs=[pl.BlockSpec((n, FFT_TN), lambda b, j: (b, j))] * 2,
        out_shape=[jax.ShapeDtypeStruct((n_batch * n, wide), BF16)] * 2,
        compiler_params=_cparams(("arbitrary", "arbitrary")),
        name="fft1",
    )(f1, x2)
    a_spec = pl.BlockSpec((FFT_KB, n, BW), lambda b, j: (b * (n // FFT_KB) + j, 0, 0))
    z_spec = pl.BlockSpec((n, FFT_KB * BW), lambda b, j: (b, j))
    zr, zi = pl.pallas_call(
        _fft2_kernel,
        grid=(n_batch, n // FFT_KB),
        in_specs=[pl.BlockSpec((FFT_KB, 2 * n, 2 * n), lambda b, j: (j, 0, 0)), a_spec, a_spec],
        out_specs=[z_spec, z_spec],
        out_shape=[jax.ShapeDtypeStruct((n_batch * n, wide), BF16)] * 2,
        compiler_params=_cparams(("arbitrary", "arbitrary")),
        name="fft2",
    )(m2, ar.reshape(n_batch * n, n, BW), ai.reshape(n_batch * n, n, BW))
    return zr.reshape(n_batch * seq, BW), zi.reshape(n_batch * seq, BW)


def _ctx_kernel(sink_ref, qn_ref, kn_ref, vn_ref, fu_ref, qw_ref, kw_ref, vw_ref, dft_ref,
                a_ref, w_ref, zr_ref, zi_ref, *, ctx_len):
    for p in range(NA_HEADS // 2):
        sl = slice(p * LANES, (p + 1) * LANES)
        qs, lo = _stack_heads(qn_ref[:, sl], ctx_len)
        o = _attend(qs, [(kn_ref[:, sl], vn_ref[:, sl], None)])
        a_ref[:, sl] = jnp.where(lo, o[:ctx_len], o[ctx_len:]).astype(BF16)
    for g in range(WA_KV_HEADS):
        sl = slice(g * LANES, (g + 1) * LANES)
        sl2 = slice(g * 2 * LANES, (g + 1) * 2 * LANES)
        qs, lo = _stack_group(qw_ref[:, sl2])
        o = _attend(qs, [(kw_ref[:, sl], vw_ref[:, sl], None)], _sink_column(sink_ref, g, ctx_len))
        w_ref[:, sl2] = _unstack_group(o, ctx_len, lo).astype(BF16)
    z = _dot(dft_ref[...], fu_ref[...])
    zr_ref[...] = z[:ctx_len].astype(BF16)
    zi_ref[...] = z[ctx_len:].astype(BF16)


def _ctx_mix(sink, qn, kn, vn, fu, qw, kw, vw, dft_c, n_batch, seq, ctx_len):
    blk0 = (n_batch * seq) // ctx_len

    def in_spec(w):
        return pl.BlockSpec((ctx_len, w), lambda b: (blk0 + b, 0))

    out_spec = pl.BlockSpec((ctx_len, BW), lambda b: (b, 0))
    return pl.pallas_call(
        functools.partial(_ctx_kernel, ctx_len=ctx_len),
        grid=(n_batch,),
        in_specs=[pl.BlockSpec(memory_space=pltpu.SMEM),
                  in_spec(BW), in_spec(BW), in_spec(BW), in_spec(BW), in_spec(BW),
                  in_spec(2 * LANES), in_spec(2 * LANES),
                  pl.BlockSpec(dft_c.shape, lambda b: (0, 0))],
        out_specs=[out_spec] * 4,
        out_shape=[jax.ShapeDtypeStruct((n_batch * ctx_len, BW), BF16)] * 4,
        compiler_params=_cparams(("arbitrary",)),
        name="ctx_mix",
    )(sink, qn, kn, vn, fu, qw, kw, vw, dft_c)


def _merge_kernel(x_ref, g_ref, sh_ref, sc_ref, gt_ref, a_ref, zr_ref, zi_ref, w_ref, *rest, n_lat_tiles):
    if n_lat_tiles is None:
        wg_ref, wbr_ref, cbd_ref, sbd_ref, wout_ref, o_ref = rest
        a, zr, zi, w = a_ref[...], zr_ref[...], zi_ref[...], w_ref[...]
    else:
        ac_ref, zrc_ref, zic_ref, wc_ref, wg_ref, wbr_ref, cbd_ref, sbd_ref, wout_ref, o_ref = rest
        is_ctx = pl.program_id(0) >= n_lat_tiles
        a = jnp.where(is_ctx, ac_ref[...], a_ref[...])
        zr = jnp.where(is_ctx, zrc_ref[...], zr_ref[...])
        zi = jnp.where(is_ctx, zic_ref[...], zi_ref[...])
        w = jnp.where(is_ctx, wc_ref[...], w_ref[...])
    x = x_ref[...]
    u = _norm_mod(x, g_ref[...], sh_ref[...], sc_ref[...]).astype(BF16)
    f = (_dot(zr, cbd_ref[...]) + _dot(zi, sbd_ref[...])).astype(BF16)
    acc = None
    for i, br in enumerate((a, f, w)):
        gate = _sigmoid(_dot(u, wg_ref[:, i * D:(i + 1) * D]))
        term = gate * _dot(br, wbr_ref[i])
        acc = term if acc is None else acc + term
    o_ref[...] = x + gt_ref[...] * _dot(acc.astype(BF16), wout_ref[...])


def _merge(h, n_tiles, g, mod3, layer, branches, ctx_branches, weights, tile_info):
    n_lat_tiles = tile_info[0]
    tile_spec = pl.BlockSpec((TM, D), lambda i: (i, 0))
    vec_spec = pl.BlockSpec((1, D), lambda i: (0, 0))
    has_ctx = ctx_branches is not None
    lat_spec = pl.BlockSpec((TM, BW), lambda i: (jnp.minimum(i, n_lat_tiles - 1), 0))
    ctx_spec = pl.BlockSpec((TM, BW), lambda i: (jnp.maximum(i - n_lat_tiles, 0), 0))
    in_specs = [tile_spec, vec_spec,
                _mod_spec(layer, 3, *tile_info), _mod_spec(layer, 4, *tile_info), _mod_spec(layer, 5, *tile_info)]
    in_specs += [lat_spec] * 4
    args = [h, g.reshape(1, D), mod3, mod3, mod3, *branches]
    if has_ctx:
        in_specs += [ctx_spec] * 4
        args += list(ctx_branches)
    in_specs += [_const_spec(w.shape) for w in weights]
    args += list(weights)
    return pl.pallas_call(
        functools.partial(_merge_kernel, n_lat_tiles=n_lat_tiles if has_ctx else None),
        grid=(n_tiles,),
        in_specs=in_specs,
        out_specs=tile_spec,
        out_shape=jax.ShapeDtypeStruct((n_tiles * TM, D), F32),
        compiler_params=_cparams(("arbitrary",)),
        name="merge_ctx" if has_ctx else "merge",
    )(*args)


def _ffn_weights(w13, w2):
    a = w13[:, :D_FF].reshape(D, NCH, TF)
    b = w13[:, D_FF:].reshape(D, NCH, TF)
    w13c = jnp.concatenate([a, b], axis=2).transpose(1, 0, 2).astype(BF16)
    return w13c, w2.astype(BF16)


def _proj_weights(w_in):
    o_wa_k = 5 * BW
    o_wa_v = o_wa_k + WA_KV_HEADS * HD
    o_gate = o_wa_v + WA_KV_HEADS * HD
    wk = w_in[:, o_wa_k:o_wa_v].reshape(D, WA_KV_HEADS, 1, HD)
    wv = w_in[:, o_wa_v:o_gate].reshape(D, WA_KV_HEADS, 1, HD)
    kk = jnp.broadcast_to(wk, (D, WA_KV_HEADS, 2, HD)).reshape(D, 2 * LANES)
    vv = jnp.broadcast_to(wv, (D, WA_KV_HEADS, 2, HD)).reshape(D, 2 * LANES)
    w_proj = jnp.concatenate([w_in[:, :o_wa_k], kk, vv], axis=1).astype(BF16)
    return w_proj, w_in[:, o_gate:].astype(BF16)


def _rope_tables(seq):
    t = np.arange(seq)
    row = (t // GRID_W).astype(np.float64)
    col = (t % GRID_W).astype(np.float64)
    n_freq = HD // 4
    inv = ROPE_BASE ** (-np.arange(n_freq, dtype=np.float64) / n_freq)
    ang = np.concatenate([row[:, None] * inv, col[:, None] * inv], axis=-1)
    cos, sin = np.cos(ang), np.sin(ang)
    cos_h = np.concatenate([cos, cos], axis=1)
    sin_h = np.concatenate([-sin, sin], axis=1)
    cos2 = np.concatenate([np.tile(cos_h, (1, 2)), np.ones((TM, LANES), np.float32)], axis=0)
    sin2 = np.concatenate([np.tile(sin_h, (1, 2)), np.zeros((TM, LANES), np.float32)], axis=0)
    return jnp.asarray(cos2, F32), jnp.asarray(sin2, F32)


def kernel(x, c, ctx, c_ctx, w_ada, b_ada, g_ffn1, ffn1_w13, ffn1_w2, g_mix, w_in, na_bias, wa_sink,
           w_br, w_out, g_ffn2, ffn2_w13, ffn2_w2, g_final):
    n_batch, seq, _ = x.shape
    ctx_len = ctx.shape[1]
    depth = w_ada.shape[0]
    rows = seq // GRID_W
    n_lat = n_batch * seq
    n_lat_tiles = n_lat // TM
    n_all_tiles = (n_lat + n_batch * ctx_len) // TM
    mod_rows = 8
    tile_info = (n_lat_tiles, seq // TM, n_batch, mod_rows)

    cc = jnp.concatenate([c, c_ctx[None], jnp.zeros((mod_rows - n_batch - 1, D), F32)], axis=0)
    mod3 = _ada(cc, w_ada, b_ada).reshape(depth * mod_rows, 1, N_MOD * D)

    cos_t, sin_t = _rope_tables(seq)
    f1, m2, cbd, sbd = (jnp.asarray(t, F32).astype(BF16) for t in _fft_tables())
    dft_c = jnp.asarray(_ctx_dft_table(ctx_len), F32).astype(BF16)

    h = jnp.concatenate([x.reshape(n_lat, D), ctx.reshape(n_batch * ctx_len, D)], axis=0)
    for l in range(depth):
        last = l == depth - 1
        w13a, w2a = _ffn_weights(ffn1_w13[l], ffn1_w2[l])
        w13b, w2b = _ffn_weights(ffn2_w13[l], ffn2_w2[l])
        w_proj, w_gate = _proj_weights(w_in[l])
        merge_w = (w_gate, w_br[l].astype(BF16), cbd, sbd, w_out[l].astype(BF16))
        bias_tab = _na_bias_table(na_bias[l], rows)

        h = _ffn(h, n_all_tiles, g_ffn1[l], mod3, l, 0, w13a, w2a, tile_info)
        qn, kn, vn, fu, qw, kw, vw = _proj(h, n_all_tiles, g_mix[l], mod3, l, w_proj, cos_t, sin_t, tile_info)
        a = _na(qn, kn, vn, bias_tab, n_batch, seq, ctx_len)
        w = _wa(wa_sink[l], qw, kw, vw, n_batch, seq, ctx_len)
        zr, zi = _fft(fu, f1, m2, n_batch, seq)
        if last:
            h = _merge(h, n_lat_tiles, g_mix[l], mod3, l, (a, zr, zi, w), None, merge_w, tile_info)
            h = _ffn(h, n_lat_tiles, g_ffn2[l], mod3, l, 6, w13b, w2b, tile_info, g_final=g_final)
        else:
            ctx_br = _ctx_mix(wa_sink[l], qn, kn, vn, fu, qw, kw, vw, dft_c, n_batch, seq, ctx_len)
            ac, wc, zrc, zic = ctx_br
            h = _merge(h, n_all_tiles, g_mix[l], mod3, l, (a, zr, zi, w), (ac, zrc, zic, wc), merge_w, tile_info)
            h = _ffn(h, n_all_tiles, g_ffn2[l], mod3, l, 6, w13b, w2b, tile_info)
    return h.reshape(n_batch, seq, D)
```

```python
import functools

import numpy as np
import jax
import jax.numpy as jnp
from jax import lax
from jax.experimental import pallas as pl
from jax.experimental.pallas import tpu as pltpu

D = 1024
GRID_W = 64
HD = 64
NA_HEADS = 8
NA_WIN_R = 8
NA_WIN_C = 16
FN_GROUPS = 8
FN_GROUP_DIM = 64
WA_HEADS = 8
WA_KV_HEADS = 2
WA_WINDOW = 128
D_FF = 2816
N_MOD = 9
ROPE_BASE = 10000.0
EPS = 1e-6
NEG_INF = -1e30
BW = 512

LANES = 128
MXU_DIM = 256
TM = 512
TF = MXU_DIM
NCH = D_FF // TF
VMEM_LIMIT = 56 * 1024 * 1024

NA_QR = 4
NA_KR = 12
NA_Q = NA_QR * GRID_W
NA_K = NA_KR * GRID_W
WA_Q = 128
WA_K = 3 * WA_Q
WA_QB = 2

F32 = jnp.float32
BF16 = jnp.bfloat16


def _cparams(sem):
    return pltpu.CompilerParams(dimension_semantics=sem, vmem_limit_bytes=VMEM_LIMIT)


def _const_spec(shape):
    nd = len(shape)
    return pl.BlockSpec(shape, lambda *_: (0,) * nd, pipeline_mode=pl.Buffered(1))


def _sigmoid(x):
    return 1.0 / (1.0 + jnp.exp(-x))


def _norm_mod(x, g, shift, scale):
    y = x * lax.rsqrt(jnp.mean(x * x, axis=-1, keepdims=True) + EPS) * g
    return y * (1.0 + scale) + shift


def _dot(a, b):
    return jnp.dot(a, b, preferred_element_type=F32)


def _dot_nt(a, b):
    return lax.dot_general(a, b, (((1,), (1,)), ((), ())), preferred_element_type=F32)


def _ada_kernel(c_ref, w_ref, b_ref, o_ref):
    x = c_ref[...]
    sx = (x * _sigmoid(x)).astype(BF16)
    o_ref[...] = _dot(sx, w_ref[...].astype(BF16)) + b_ref[...]


def _ada(cc, w_ada, b_ada):
    depth = w_ada.shape[0]
    n = w_ada.shape[2]
    tn = 1536
    rows = cc.shape[0]
    return pl.pallas_call(
        _ada_kernel,
        grid=(depth, n // tn),
        in_specs=[
            pl.BlockSpec((rows, D), lambda l, j: (0, 0)),
            pl.BlockSpec((None, D, tn), lambda l, j: (l, 0, j)),
            pl.BlockSpec((None, 1, tn), lambda l, j: (l, 0, j)),
        ],
        out_specs=pl.BlockSpec((None, rows, tn), lambda l, j: (l, 0, j)),
        out_shape=jax.ShapeDtypeStruct((depth, rows, n), F32),
        compiler_params=_cparams(("arbitrary", "arbitrary")),
        name="ada",
    )(cc, w_ada, b_ada.reshape(depth, 1, n))


def _mod_spec(layer, k, n_lat_tiles, tiles_per_batch, n_batch, mod_rows):
    def idx(i):
        row = jnp.where(i < n_lat_tiles, i // tiles_per_batch, n_batch)
        return (layer * mod_rows + row, 0, k)

    return pl.BlockSpec((None, 1, D), idx)


def _ffn_kernel(*refs, final, split_at):
    refs = list(refs)
    x_ref = refs.pop(0)
    if split_at is not None:
        xc_ref = refs.pop(0)
    g_ref, sh_ref, sc_ref, gt_ref, w13_ref, w2_ref = refs[:6]
    if final:
        gf_ref, o_ref, hm_ref = refs[6:]
    else:
        o_ref, hm_ref = refs[6:]
    x = x_ref[...]
    if split_at is not None:
        x = jnp.where(pl.program_id(0) >= split_at, xc_ref[...], x)
    u = _norm_mod(x, g_ref[...], sh_ref[...], sc_ref[...]).astype(BF16)
    for c in range(NCH):
        a = _dot(u, w13_ref[:, c * TF:(c + 1) * TF])
        b = _dot(u, w13_ref[:, D_FF + c * TF:D_FF + (c + 1) * TF])
        hm_ref[:, c * TF:(c + 1) * TF] = (a * _sigmoid(a) * b).astype(BF16)
    f = _dot(hm_ref[...], w2_ref[...])
    out = x + 0.5 * gt_ref[...] * f
    if final:
        out = out * lax.rsqrt(jnp.mean(out * out, axis=-1, keepdims=True) + EPS) * gf_ref[...]
    o_ref[...] = out


def _ffn(h, n_tiles, g, mod3, layer, mod_k, w13, w2, tile_info, g_final=None, h_ctx=None):
    final = g_final is not None
    n_lat_tiles = tile_info[0]
    tile_spec = pl.BlockSpec((TM, D), lambda i: (i, 0))
    vec_spec = pl.BlockSpec((1, D), lambda i: (0, 0))
    if h_ctx is None:
        in_specs, args = [tile_spec], [h]
    else:
        in_specs = [pl.BlockSpec((TM, D), lambda i: (jnp.minimum(i, n_lat_tiles - 1), 0)),
                    pl.BlockSpec((TM, D), lambda i: (jnp.maximum(i - n_lat_tiles, 0), 0))]
        args = [h, h_ctx]
    in_specs += [
        vec_spec,
        _mod_spec(layer, mod_k, *tile_info),
        _mod_spec(layer, mod_k + 1, *tile_info),
        _mod_spec(layer, mod_k + 2, *tile_info),
        _const_spec(w13.shape),
        _const_spec(w2.shape),
    ]
    args += [g.reshape(1, D), mod3, mod3, mod3, w13, w2]
    if final:
        in_specs.append(vec_spec)
        args.append(g_final.reshape(1, D))
    return pl.pallas_call(
        functools.partial(_ffn_kernel, final=final, split_at=None if h_ctx is None else n_lat_tiles),
        grid=(n_tiles,),
        in_specs=in_specs,
        out_specs=tile_spec,
        out_shape=jax.ShapeDtypeStruct((n_tiles * TM, D), F32),
        scratch_shapes=[pltpu.VMEM((TM, D_FF), BF16)],
        compiler_params=_cparams(("arbitrary",)),
        name="ffn_final" if final else ("ffn" if h_ctx is None else "ffn_split"),
    )(*args)


PROJ_W = 5 * BW + 2 * LANES


def _rope(t, cos, sin_signed, first_half):
    partner = jnp.where(first_half, pltpu.roll(t, LANES - HD // 2, axis=1), pltpu.roll(t, HD // 2, axis=1))
    return t * cos + partner * sin_signed


def _dup_heads(t, lo):
    sw = pltpu.roll(t, HD, axis=1)
    return jnp.where(lo, t, sw), jnp.where(lo, sw, t)


def _proj_kernel(x_ref, g_ref, sh_ref, sc_ref, w_ref, cos_ref, sin_ref,
                 qn_ref, kn_ref, vn_ref, fu_ref, qw_ref, kw_ref, vw_ref):
    u = _norm_mod(x_ref[...], g_ref[...], sh_ref[...], sc_ref[...]).astype(BF16)
    scale = HD ** -0.5
    qn_ref[...] = (_dot(u, w_ref[:, 0:BW]) * scale).astype(BF16)
    kn_ref[...] = _dot(u, w_ref[:, BW:2 * BW]).astype(BF16)
    vn_ref[...] = _dot(u, w_ref[:, 2 * BW:3 * BW]).astype(BF16)
    fu_ref[...] = _dot(u, w_ref[:, 3 * BW:4 * BW]).astype(BF16)
    cos = cos_ref[...]
    sin = sin_ref[...]
    lane = lax.broadcasted_iota(jnp.int32, (TM, LANES), 1)
    first_half = (lane & (HD - 1)) < (HD // 2)
    lo = lane < HD
    wq = _dot(u, w_ref[:, 4 * BW:5 * BW])
    for j in range(BW // LANES):
        sl = slice(j * LANES, (j + 1) * LANES)
        qw_ref[:, sl] = (_rope(wq[:, sl], cos, sin, first_half) * scale).astype(BF16)
    kv = _dot(u, w_ref[:, 5 * BW:PROJ_W])
    k0, k1 = _dup_heads(_rope(kv[:, :LANES], cos, sin, first_half), lo)
    kw_ref[:, :LANES] = k0.astype(BF16)
    kw_ref[:, LANES:] = k1.astype(BF16)
    v0, v1 = _dup_heads(kv[:, LANES:], lo)
    vw_ref[:, :LANES] = v0.astype(BF16)
    vw_ref[:, LANES:] = v1.astype(BF16)


def _proj(h, n_tiles, g, mod3, layer, w_proj, cos_t, sin_t, tile_info):
    n_lat_tiles, tiles_per_batch, _, _ = tile_info
    tile_spec = pl.BlockSpec((TM, D), lambda i: (i, 0))
    vec_spec = pl.BlockSpec((1, D), lambda i: (0, 0))
    rope_spec = pl.BlockSpec((TM, LANES), lambda i: (jnp.where(i < n_lat_tiles, i % tiles_per_batch, tiles_per_batch), 0))
    rows = n_tiles * TM

    def out_spec(w):
        return pl.BlockSpec((TM, w), lambda i: (i, 0))

    widths = [BW, BW, BW, BW, BW, 2 * LANES, 2 * LANES]
    return pl.pallas_call(
        _proj_kernel,
        grid=(n_tiles,),
        in_specs=[tile_spec, vec_spec,
                  _mod_spec(layer, 3, *tile_info), _mod_spec(layer, 4, *tile_info),
                  _const_spec(w_proj.shape), rope_spec, rope_spec],
        out_specs=[out_spec(w) for w in widths],
        out_shape=[jax.ShapeDtypeStruct((rows, w), BF16) for w in widths],
        compiler_params=_cparams(("arbitrary",)),
        name="proj",
    )(h, g.reshape(1, D), mod3, mod3, w_proj, cos_t, sin_t)


def _lane_tiles(s):
    return [s[:, j * LANES:(j + 1) * LANES] for j in range(s.shape[1] // LANES)]


def _attend(qs, key_sets, sink_tile=None):
    scores = []
    for k, _, bias in key_sets:
        s = _dot_nt(qs, k)
        if bias is not None:
            s = s + bias
        scores.append(s)
    tiles = [t for s in scores for t in _lane_tiles(s)]
    if sink_tile is not None:
        tiles.append(sink_tile)
    m = functools.reduce(jnp.maximum, tiles).max(axis=-1, keepdims=True)
    psum = None
    acc = None
    for s, (_, v, _) in zip(scores, key_sets):
        p = jnp.exp(s - m)
        for t in _lane_tiles(p):
            psum = t if psum is None else psum + t
        o = _dot(p.astype(BF16), v)
        acc = o if acc is None else acc + o
    if sink_tile is not None:
        lane = lax.broadcasted_iota(jnp.int32, sink_tile.shape, 1)
        psum = psum + jnp.where(lane == 0, jnp.exp(sink_tile - m), 0.0)
    return acc / psum.sum(axis=-1, keepdims=True)


def _stack_heads(q, rows):
    lane = lax.broadcasted_iota(jnp.int32, (rows, LANES), 1)
    lo = lane < HD
    zero = jnp.zeros_like(q)
    return jnp.concatenate([jnp.where(lo, q, zero), jnp.where(lo, zero, q)], axis=0), lo


def _na_kernel(q_ref, k_ref, v_ref, kc_ref, vc_ref, bias_ref, o_ref, *, rows):
    j = pl.program_id(1)
    krow = jnp.clip(NA_QR * j - NA_WIN_R // 2, 0, rows - NA_KR)
    kstart = pl.multiple_of(krow * GRID_W, GRID_W)
    for p in range(NA_HEADS // 2):
        sl = slice(p * LANES, (p + 1) * LANES)
        kb = k_ref[pl.ds(kstart, NA_K), sl]
        vb = v_ref[pl.ds(kstart, NA_K), sl]
        qs, lo = _stack_heads(q_ref[:, sl], NA_Q)
        o = _attend(qs, [(kb, vb, bias_ref[p]), (kc_ref[:, sl], vc_ref[:, sl], None)])
        o_ref[:, sl] = jnp.where(lo, o[:NA_Q], o[NA_Q:]).astype(BF16)


def _na(qn, kn, vn, bias_tab, n_batch, seq, ctx_len):
    rows = seq // GRID_W
    n_blk = rows // NA_QR
    n_pair = NA_HEADS // 2
    ctx_blk0 = (n_batch * seq) // ctx_len

    def pat(j):
        return jnp.where(j == 0, 0, jnp.where(j == n_blk - 1, 2, 1))

    return pl.pallas_call(
        functools.partial(_na_kernel, rows=rows),
        grid=(n_batch, n_blk),
        in_specs=[
            pl.BlockSpec((NA_Q, BW), lambda b, j: (b * n_blk + j, 0)),
            pl.BlockSpec((seq, BW), lambda b, j: (b, 0)),
            pl.BlockSpec((seq, BW), lambda b, j: (b, 0)),
            pl.BlockSpec((ctx_len, BW), lambda b, j: (ctx_blk0 + b, 0)),
            pl.BlockSpec((ctx_len, BW), lambda b, j: (ctx_blk0 + b, 0)),
            pl.BlockSpec((None, n_pair, 2 * NA_Q, NA_K), lambda b, j: (pat(j), 0, 0, 0)),
        ],
        out_specs=pl.BlockSpec((NA_Q, BW), lambda b, j: (b * n_blk + j, 0)),
        out_shape=jax.ShapeDtypeStruct((n_batch * seq, BW), BF16),
        compiler_params=_cparams(("arbitrary", "arbitrary")),
        name="na",
    )(qn, kn, vn, kn, vn, bias_tab)


def _na_bias_table(bias, rows):
    n_blk = rows // NA_QR
    h = bias.shape[0]
    n_dc = 2 * NA_WIN_C - 1
    qc = np.arange(GRID_W)[:, None]
    kc = np.arange(GRID_W)[None, :]
    ws = np.clip(qc - NA_WIN_C // 2, 0, GRID_W - NA_WIN_C)
    col_ok = (kc >= ws) & (kc < ws + NA_WIN_C)
    dc = np.clip(kc - qc, -(NA_WIN_C - 1), NA_WIN_C - 1) + NA_WIN_C - 1
    onehot = (dc[None] == np.arange(n_dc)[:, None, None]).astype(np.float32)
    toep = jnp.einsum('hrd,dqk->hrqk', bias.astype(F32), jnp.asarray(onehot), precision=lax.Precision.HIGHEST)
    toep = jnp.where(col_ok[None, None], toep, NEG_INF)
    masked = jnp.full((h, GRID_W, GRID_W), NEG_INF, F32)
    pats = []
    for blk in (0, 1, n_blk - 1):
        k0 = min(max(NA_QR * blk - NA_WIN_R // 2, 0), rows - NA_KR)
        q_rows = []
        for i in range(NA_QR):
            r = NA_QR * blk + i
            rs = min(max(r - NA_WIN_R // 2, 0), rows - NA_WIN_R)
            blocks = []
            for t in range(NA_KR):
                kr = k0 + t
                blocks.append(toep[:, kr - r + NA_WIN_R - 1] if rs <= kr < rs + NA_WIN_R else masked)
            q_rows.append(jnp.concatenate(blocks, axis=-1))
        pats.append(jnp.concatenate(q_rows, axis=1).reshape(h // 2, 2 * NA_Q, NA_K))
    return jnp.stack(pats, axis=0)


def _stack_group(q):
    rows = q.shape[0]
    a, lo = _stack_heads(q[:, :LANES], rows)
    b, _ = _stack_heads(q[:, LANES:], rows)
    return jnp.concatenate([a, b], axis=0), lo


def _unstack_group(o, rows, lo):
    oa = jnp.where(lo, o[0:rows], o[rows:2 * rows])
    ob = jnp.where(lo, o[2 * rows:3 * rows], o[3 * rows:4 * rows])
    return jnp.concatenate([oa, ob], axis=1)


def _sink_tile(sink_ref, g, rows):
    gq = WA_HEADS // WA_KV_HEADS
    return jnp.concatenate([jnp.full((rows, LANES), sink_ref[g * gq + h], F32) for h in range(gq)], axis=0)


def _wa_kernel(sink_ref, q_ref, k_ref, v_ref, kc_ref, vc_ref, o_ref, *, seq):
    for t in range(WA_QB):
        n = pl.program_id(1) * WA_QB + t
        rows = slice(t * WA_Q, (t + 1) * WA_Q)
        kstart = pl.multiple_of(jnp.clip(n * WA_Q - WA_Q, 0, seq - WA_K), WA_Q)
        qpos = n * WA_Q + lax.broadcasted_iota(jnp.int32, (WA_Q, WA_K), 0)
        kpos = kstart + lax.broadcasted_iota(jnp.int32, (WA_Q, WA_K), 1)
        band = jnp.where(jnp.abs(kpos - qpos) <= WA_WINDOW, 0.0, NEG_INF).astype(F32)
        band = jnp.concatenate([band] * (WA_HEADS // WA_KV_HEADS), axis=0)
        for g in range(WA_KV_HEADS):
            sl = slice(g * LANES, (g + 1) * LANES)
            sl2 = slice(g * 2 * LANES, (g + 1) * 2 * LANES)
            kb = k_ref[pl.ds(kstart, WA_K), sl]
            vb = v_ref[pl.ds(kstart, WA_K), sl]
            qs, lo = _stack_group(q_ref[rows, sl2])
            o = _attend(qs, [(kb, vb, band), (kc_ref[:, sl], vc_ref[:, sl], None)], _sink_tile(sink_ref, g, WA_Q))
            o_ref[rows, sl2] = _unstack_group(o, WA_Q, lo).astype(BF16)


def _wa(sink, qw, kw, vw, n_batch, seq, ctx_len):
    n_blk = seq // (WA_Q * WA_QB)
    ctx_blk0 = (n_batch * seq) // ctx_len
    gw = 2 * LANES
    return pl.pallas_call(
        functools.partial(_wa_kernel, seq=seq),
        grid=(n_batch, n_blk),
        in_specs=[
            pl.BlockSpec(memory_space=pltpu.SMEM),
            pl.BlockSpec((WA_Q * WA_QB, BW), lambda b, n: (b * n_blk + n, 0)),
            pl.BlockSpec((seq, gw), lambda b, n: (b, 0)),
            pl.BlockSpec((seq, gw), lambda b, n: (b, 0)),
            pl.BlockSpec((ctx_len, gw), lambda b, n: (ctx_blk0 + b, 0)),
            pl.BlockSpec((ctx_len, gw), lambda b, n: (ctx_blk0 + b, 0)),
        ],
        out_specs=pl.BlockSpec((WA_Q * WA_QB, BW), lambda b, n: (b * n_blk + n, 0)),
        out_shape=jax.ShapeDtypeStruct((n_batch * seq, BW), BF16),
        compiler_params=_cparams(("arbitrary", "arbitrary")),
        name="wa",
    )(sink, qw, kw, vw, kw, vw)


FFT_TN = 4096
FFT_KB = 8


def _fft1_kernel(f_ref, x_ref, ar_ref, ai_ref):
    r = _dot(f_ref[...], x_ref[...])
    ar_ref[...] = r[:GRID_W].astype(BF16)
    ai_ref[...] = r[GRID_W:].astype(BF16)


def _fft2_kernel(m_ref, ar_ref, ai_ref, zr_ref, zi_ref):
    for t in range(FFT_KB):
        a = jnp.concatenate([ar_ref[t], ai_ref[t]], axis=0)
        z = _dot(m_ref[t], a)
        zr_ref[:, t * BW:(t + 1) * BW] = z[:GRID_W].astype(BF16)
        zi_ref[:, t * BW:(t + 1) * BW] = z[GRID_W:].astype(BF16)


def _fft_tables():
    n = GRID_W
    k = np.arange(n)
    ang1 = 2.0 * np.pi * ((k[:, None] * k[None, :]) % n) / n
    f1 = np.concatenate([np.cos(ang1), -np.sin(ang1)], axis=0) / 8.0
    ka = k[:, None, None]
    kb = k[None, :, None]
    n1 = k[None, None, :]
    ang2 = 2.0 * np.pi * ((n1 * (ka + n * kb)) % (n * n)) / (n * n)
    mr, mi = np.cos(ang2), -np.sin(ang2)
    m2 = np.concatenate([np.concatenate([mr, -mi], axis=2), np.concatenate([mi, mr], axis=2)], axis=1) / 8.0
    c = np.arange(FN_GROUP_DIM)
    angc = 2.0 * np.pi * ((c[:, None] * c[None, :]) % FN_GROUP_DIM) / FN_GROUP_DIM
    eye = np.eye(FN_GROUPS)
    cbd = np.kron(eye, np.cos(angc)) / 8.0
    sbd = np.kron(eye, np.sin(angc)) / 8.0
    return f1, m2, cbd, sbd


def _ctx_dft_table(ctx_len):
    k = np.arange(ctx_len)
    ang = 2.0 * np.pi * ((k[:, None] * k[None, :]) % ctx_len) / ctx_len
    return np.concatenate([np.cos(ang), -np.sin(ang)], axis=0) / np.sqrt(ctx_len)


def _fft(fu, f1, m2, n_batch, seq):
    n = GRID_W
    wide = n * BW
    x2 = fu.reshape(-1, wide)
    ar, ai = pl.pallas_call(
        _fft1_kernel,
        grid=(n_batch, wide // FFT_TN),
        in_specs=[pl.BlockSpec((2 * n, n), lambda b, j: (0, 0)),
                  pl.BlockSpec((n, FFT_TN), lambda b, j: (b, j))],
        out_specs=[pl.BlockSpec((n, FFT_TN), lambda b, j: (b, j))] * 2,
        out_shape=[jax.ShapeDtypeStruct((n_batch * n, wide), BF16)] * 2,
        compiler_params=_cparams(("arbitrary", "arbitrary")),
        name="fft1",
    )(f1, x2)
    a_spec = pl.BlockSpec((FFT_KB, n, BW), lambda b, j: (b * (n // FFT_KB) + j, 0, 0))
    z_spec = pl.BlockSpec((n, FFT_KB * BW), lambda b, j: (b, j))
    zr, zi = pl.pallas_call(
        _fft2_kernel,
        grid=(n_batch, n // FFT_KB),
        in_specs=[pl.BlockSpec((FFT_KB, 2 * n, 2 * n), lambda b, j: (j, 0, 0)), a_spec, a_spec],
        out_specs=[z_spec, z_spec],
        out_shape=[jax.ShapeDtypeStruct((n_batch * n, wide), BF16)] * 2,
        compiler_params=_cparams(("arbitrary", "arbitrary")),
        name="fft2",
    )(m2, ar.reshape(n_batch * n, n, BW), ai.reshape(n_batch * n, n, BW))
    return zr.reshape(n_batch * seq, BW), zi.reshape(n_batch * seq, BW)


def _ctx_kernel(sink_ref, qn_ref, kn_ref, vn_ref, fu_ref, qw_ref, kw_ref, vw_ref, dft_ref,
                a_ref, w_ref, zr_ref, zi_ref, *, ctx_len):
    for p in range(NA_HEADS // 2):
        sl = slice(p * LANES, (p + 1) * LANES)
        qs, lo = _stack_heads(qn_ref[:, sl], ctx_len)
        o = _attend(qs, [(kn_ref[:, sl], vn_ref[:, sl], None)])
        a_ref[:, sl] = jnp.where(lo, o[:ctx_len], o[ctx_len:]).astype(BF16)
    for g in range(WA_KV_HEADS):
        sl = slice(g * LANES, (g + 1) * LANES)
        sl2 = slice(g * 2 * LANES, (g + 1) * 2 * LANES)
        qs, lo = _stack_group(qw_ref[:, sl2])
        o = _attend(qs, [(kw_ref[:, sl], vw_ref[:, sl], None)], _sink_tile(sink_ref, g, ctx_len))
        w_ref[:, sl2] = _unstack_group(o, ctx_len, lo).astype(BF16)
    z = _dot(dft_ref[...], fu_ref[...])
    zr_ref[...] = z[:ctx_len].astype(BF16)
    zi_ref[...] = z[ctx_len:].astype(BF16)


def _ctx_mix(sink, qn, kn, vn, fu, qw, kw, vw, dft_c, n_batch, seq, ctx_len):
    blk0 = (n_batch * seq) // ctx_len

    def in_spec(w):
        return pl.BlockSpec((ctx_len, w), lambda b: (blk0 + b, 0))

    out_spec = pl.BlockSpec((ctx_len, BW), lambda b: (b, 0))
    return pl.pallas_call(
        functools.partial(_ctx_kernel, ctx_len=ctx_len),
        grid=(n_batch,),
        in_specs=[pl.BlockSpec(memory_space=pltpu.SMEM),
                  in_spec(BW), in_spec(BW), in_spec(BW), in_spec(BW), in_spec(BW),
                  in_spec(2 * LANES), in_spec(2 * LANES),
                  pl.BlockSpec(dft_c.shape, lambda b: (0, 0))],
        out_specs=[out_spec] * 4,
        out_shape=[jax.ShapeDtypeStruct((n_batch * ctx_len, BW), BF16)] * 4,
        compiler_params=_cparams(("arbitrary",)),
        name="ctx_mix",
    )(sink, qn, kn, vn, fu, qw, kw, vw, dft_c)


def _merge_kernel(x_ref, g_ref, sh_ref, sc_ref, gt_ref, a_ref, zr_ref, zi_ref, w_ref, *rest, n_lat_tiles):
    if n_lat_tiles is None:
        wg_ref, wbr_ref, cbd_ref, sbd_ref, wout_ref, o_ref = rest
        a, zr, zi, w = a_ref[...], zr_ref[...], zi_ref[...], w_ref[...]
    else:
        ac_ref, zrc_ref, zic_ref, wc_ref, wg_ref, wbr_ref, cbd_ref, sbd_ref, wout_ref, o_ref = rest
        is_ctx = pl.program_id(0) >= n_lat_tiles
        a = jnp.where(is_ctx, ac_ref[...], a_ref[...])
        zr = jnp.where(is_ctx, zrc_ref[...], zr_ref[...])
        zi = jnp.where(is_ctx, zic_ref[...], zi_ref[...])
        w = jnp.where(is_ctx, wc_ref[...], w_ref[...])
    x = x_ref[...]
    u = _norm_mod(x, g_ref[...], sh_ref[...], sc_ref[...]).astype(BF16)
    f = (_dot(zr, cbd_ref[...]) + _dot(zi, sbd_ref[...])).astype(BF16)
    acc = None
    for i, br in enumerate((a, f, w)):
        gate = _sigmoid(_dot(u, wg_ref[:, i * D:(i + 1) * D]))
        term = gate * _dot(br, wbr_ref[i])
        acc = term if acc is None else acc + term
    o_ref[...] = x + gt_ref[...] * _dot(acc.astype(BF16), wout_ref[...])


def _merge(h, n_tiles, g, mod3, layer, branches, ctx_branches, weights, tile_info):
    n_lat_tiles = tile_info[0]
    tile_spec = pl.BlockSpec((TM, D), lambda i: (i, 0))
    vec_spec = pl.BlockSpec((1, D), lambda i: (0, 0))
    has_ctx = ctx_branches is not None
    lat_spec = pl.BlockSpec((TM, BW), lambda i: (jnp.minimum(i, n_lat_tiles - 1), 0))
    ctx_spec = pl.BlockSpec((TM, BW), lambda i: (jnp.maximum(i - n_lat_tiles, 0), 0))
    in_specs = [tile_spec, vec_spec,
                _mod_spec(layer, 3, *tile_info), _mod_spec(layer, 4, *tile_info), _mod_spec(layer, 5, *tile_info)]
    in_specs += [lat_spec] * 4
    args = [h, g.reshape(1, D), mod3, mod3, mod3, *branches]
    if has_ctx:
        in_specs += [ctx_spec] * 4
        args += list(ctx_branches)
    in_specs += [_const_spec(w.shape) for w in weights]
    args += list(weights)
    return pl.pallas_call(
        functools.partial(_merge_kernel, n_lat_tiles=n_lat_tiles if has_ctx else None),
        grid=(n_tiles,),
        in_specs=in_specs,
        out_specs=tile_spec,
        out_shape=jax.ShapeDtypeStruct((n_tiles * TM, D), F32),
        compiler_params=_cparams(("arbitrary",)),
        name="merge_ctx" if has_ctx else "merge",
    )(*args)


def _proj_weights(w_in):
    return w_in[:, :PROJ_W].astype(BF16), w_in[:, PROJ_W:].astype(BF16)


def _rope_tables(seq):
    t = np.arange(seq)
    row = (t // GRID_W).astype(np.float64)
    col = (t % GRID_W).astype(np.float64)
    n_freq = HD // 4
    inv = ROPE_BASE ** (-np.arange(n_freq, dtype=np.float64) / n_freq)
    ang = np.concatenate([row[:, None] * inv, col[:, None] * inv], axis=-1)
    cos, sin = np.cos(ang), np.sin(ang)
    cos_h = np.concatenate([cos, cos], axis=1)
    sin_h = np.concatenate([-sin, sin], axis=1)
    cos2 = np.concatenate([np.tile(cos_h, (1, 2)), np.ones((TM, LANES), np.float32)], axis=0)
    sin2 = np.concatenate([np.tile(sin_h, (1, 2)), np.zeros((TM, LANES), np.float32)], axis=0)
    return jnp.asarray(cos2, F32), jnp.asarray(sin2, F32)


def kernel(x, c, ctx, c_ctx, w_ada, b_ada, g_ffn1, ffn1_w13, ffn1_w2, g_mix, w_in, na_bias, wa_sink,
           w_br, w_out, g_ffn2, ffn2_w13, ffn2_w2, g_final):
    n_batch, seq, _ = x.shape
    ctx_len = ctx.shape[1]
    depth = w_ada.shape[0]
    rows = seq // GRID_W
    n_lat = n_batch * seq
    n_lat_tiles = n_lat // TM
    n_all_tiles = (n_lat + n_batch * ctx_len) // TM
    mod_rows = 8
    tile_info = (n_lat_tiles, seq // TM, n_batch, mod_rows)

    cc = jnp.concatenate([c, c_ctx[None], jnp.zeros((mod_rows - n_batch - 1, D), F32)], axis=0)
    mod3 = _ada(cc, w_ada, b_ada).reshape(depth * mod_rows, 1, N_MOD * D)

    cos_t, sin_t = _rope_tables(seq)
    f1, m2, cbd, sbd = (jnp.asarray(t, F32).astype(BF16) for t in _fft_tables())
    dft_c = jnp.asarray(_ctx_dft_table(ctx_len), F32).astype(BF16)

    h = x.reshape(n_lat, D)
    h_ctx = ctx.reshape(n_batch * ctx_len, D)
    for l in range(depth):
        last = l == depth - 1
        w13a, w2a = ffn1_w13[l].astype(BF16), ffn1_w2[l].astype(BF16)
        w13b, w2b = ffn2_w13[l].astype(BF16), ffn2_w2[l].astype(BF16)
        w_proj, w_gate = _proj_weights(w_in[l])
        merge_w = (w_gate, w_br[l].astype(BF16), cbd, sbd, w_out[l].astype(BF16))
        bias_tab = _na_bias_table(na_bias[l], rows)

        h = _ffn(h, n_all_tiles, g_ffn1[l], mod3, l, 0, w13a, w2a, tile_info, h_ctx=h_ctx if l == 0 else None)
        qn, kn, vn, fu, qw, kw, vw = _proj(h, n_all_tiles, g_mix[l], mod3, l, w_proj, cos_t, sin_t, tile_info)
        a = _na(qn, kn, vn, bias_tab, n_batch, seq, ctx_len)
        w = _wa(wa_sink[l], qw, kw, vw, n_batch, seq, ctx_len)
        zr, zi = _fft(fu, f1, m2, n_batch, seq)
        if last:
            h = _merge(h, n_lat_tiles, g_mix[l], mod3, l, (a, zr, zi, w), None, merge_w, tile_info)
            h = _ffn(h, n_lat_tiles, g_ffn2[l], mod3, l, 6, w13b, w2b, tile_info, g_final=g_final)
        else:
            ctx_br = _ctx_mix(wa_sink[l], qn, kn, vn, fu, qw, kw, vw, dft_c, n_batch, seq, ctx_len)
            ac, wc, zrc, zic = ctx_br
            h = _merge(h, n_all_tiles, g_mix[l], mod3, l, (a, zr, zi, w), (ac, zrc, zic, wc), merge_w, tile_info)
            h = _ffn(h, n_all_tiles, g_ffn2[l], mod3, l, 6, w13b, w2b, tile_info)
    return h.reshape(n_batch, seq, D)
```

```python
import functools

import numpy as np
import jax
import jax.numpy as jnp
from jax import lax
from jax.experimental import pallas as pl
from jax.experimental.pallas import tpu as pltpu

D = 1024
GRID_W = 64
HD = 64
NA_HEADS = 8
NA_WIN_R = 8
NA_WIN_C = 16
FN_GROUPS = 8
FN_GROUP_DIM = 64
WA_HEADS = 8
WA_KV_HEADS = 2
WA_WINDOW = 128
D_FF = 2816
N_MOD = 9
ROPE_BASE = 10000.0
EPS = 1e-6
NEG_INF = -1e30
BW = 512

LANES = 128
MXU_DIM = 256
TM = 512
TF = MXU_DIM
NCH = D_FF // TF
VMEM_LIMIT = 56 * 1024 * 1024

NA_QR = 4
NA_KR = 12
NA_Q = NA_QR * GRID_W
NA_K = NA_KR * GRID_W
WA_Q = 128
WA_K = 3 * WA_Q
WA_QB = 2

F32 = jnp.float32
BF16 = jnp.bfloat16


def _cparams(sem):
    return pltpu.CompilerParams(dimension_semantics=sem, vmem_limit_bytes=VMEM_LIMIT)


def _const_spec(shape):
    nd = len(shape)
    return pl.BlockSpec(shape, lambda *_: (0,) * nd, pipeline_mode=pl.Buffered(1))


def _layer_spec(stacked, layer):
    nd = stacked.ndim
    return pl.BlockSpec((None,) + stacked.shape[1:], lambda *_: (layer,) + (0,) * (nd - 1),
                        pipeline_mode=pl.Buffered(1))


def _sigmoid(x):
    return 1.0 / (1.0 + jnp.exp(-x))


def _norm_mod(x, g, shift, scale):
    y = x * lax.rsqrt(jnp.mean(x * x, axis=-1, keepdims=True) + EPS) * g
    return y * (1.0 + scale) + shift


def _dot(a, b):
    return jnp.dot(a, b, preferred_element_type=F32)


def _dot_nt(a, b):
    return lax.dot_general(a, b, (((1,), (1,)), ((), ())), preferred_element_type=F32)


def _ada_kernel(c_ref, w_ref, b_ref, o_ref):
    x = c_ref[...]
    sx = (x * _sigmoid(x)).astype(BF16)
    o_ref[...] = _dot(sx, w_ref[...].astype(BF16)) + b_ref[...]


def _ada(cc, w_ada, b_ada):
    depth = w_ada.shape[0]
    n = w_ada.shape[2]
    tn = 1536
    rows = cc.shape[0]
    return pl.pallas_call(
        _ada_kernel,
        grid=(depth, n // tn),
        in_specs=[
            pl.BlockSpec((rows, D), lambda l, j: (0, 0)),
            pl.BlockSpec((None, D, tn), lambda l, j: (l, 0, j)),
            pl.BlockSpec((None, 1, tn), lambda l, j: (l, 0, j)),
        ],
        out_specs=pl.BlockSpec((None, rows, tn), lambda l, j: (l, 0, j)),
        out_shape=jax.ShapeDtypeStruct((depth, rows, n), F32),
        compiler_params=_cparams(("arbitrary", "arbitrary")),
        name="ada",
    )(cc, w_ada, b_ada.reshape(depth, 1, n))


def _mod_spec(layer, k, n_lat_tiles, tiles_per_batch, n_batch, mod_rows):
    def idx(i):
        row = jnp.where(i < n_lat_tiles, i // tiles_per_batch, n_batch)
        return (layer * mod_rows + row, 0, k)

    return pl.BlockSpec((None, 1, D), idx)


def _ffn_kernel(*refs, final, split_at):
    refs = list(refs)
    x_ref = refs.pop(0)
    if split_at is not None:
        xc_ref = refs.pop(0)
    g_ref, sh_ref, sc_ref, gt_ref, w13_ref, w2_ref = refs[:6]
    if final:
        gf_ref, o_ref, hm_ref = refs[6:]
    else:
        o_ref, hm_ref = refs[6:]
    x = x_ref[...]
    if split_at is not None:
        x = jnp.where(pl.program_id(0) >= split_at, xc_ref[...], x)
    u = _norm_mod(x, g_ref[...], sh_ref[...], sc_ref[...]).astype(BF16)
    for c in range(NCH):
        a = _dot(u, w13_ref[:, c * TF:(c + 1) * TF])
        b = _dot(u, w13_ref[:, D_FF + c * TF:D_FF + (c + 1) * TF])
        hm_ref[:, c * TF:(c + 1) * TF] = (a * _sigmoid(a) * b).astype(BF16)
    f = _dot(hm_ref[...], w2_ref[...])
    out = x + 0.5 * gt_ref[...] * f
    if final:
        out = out * lax.rsqrt(jnp.mean(out * out, axis=-1, keepdims=True) + EPS) * gf_ref[...]
    o_ref[...] = out


def _ffn(h, n_tiles, g, mod3, layer, mod_k, w13, w2, tile_info, g_final=None, h_ctx=None):
    final = g_final is not None
    n_lat_tiles = tile_info[0]
    tile_spec = pl.BlockSpec((TM, D), lambda i: (i, 0))
    vec_spec = pl.BlockSpec((1, D), lambda i: (0, 0))
    if h_ctx is None:
        in_specs, args = [tile_spec], [h]
    else:
        in_specs = [pl.BlockSpec((TM, D), lambda i: (jnp.minimum(i, n_lat_tiles - 1), 0)),
                    pl.BlockSpec((TM, D), lambda i: (jnp.maximum(i - n_lat_tiles, 0), 0))]
        args = [h, h_ctx]
    in_specs += [
        vec_spec,
        _mod_spec(layer, mod_k, *tile_info),
        _mod_spec(layer, mod_k + 1, *tile_info),
        _mod_spec(layer, mod_k + 2, *tile_info),
        _layer_spec(w13, layer),
        _layer_spec(w2, layer),
    ]
    args += [g.reshape(1, D), mod3, mod3, mod3, w13, w2]
    if final:
        in_specs.append(vec_spec)
        args.append(g_final.reshape(1, D))
    return pl.pallas_call(
        functools.partial(_ffn_kernel, final=final, split_at=None if h_ctx is None else n_lat_tiles),
        grid=(n_tiles,),
        in_specs=in_specs,
        out_specs=tile_spec,
        out_shape=jax.ShapeDtypeStruct((n_tiles * TM, D), F32),
        scratch_shapes=[pltpu.VMEM((TM, D_FF), BF16)],
        compiler_params=_cparams(("arbitrary",)),
        name="ffn_final" if final else ("ffn" if h_ctx is None else "ffn_split"),
    )(*args)


PROJ_W = 5 * BW + 2 * LANES


def _rope(t, cos, sin_signed, first_half):
    partner = jnp.where(first_half, pltpu.roll(t, LANES - HD // 2, axis=1), pltpu.roll(t, HD // 2, axis=1))
    return t * cos + partner * sin_signed


def _dup_heads(t, lo):
    sw = pltpu.roll(t, HD, axis=1)
    return jnp.where(lo, t, sw), jnp.where(lo, sw, t)


def _proj_kernel(x_ref, g_ref, sh_ref, sc_ref, w_ref, cos_ref, sin_ref,
                 qn_ref, kn_ref, vn_ref, fu_ref, qw_ref, kw_ref, vw_ref):
    u = _norm_mod(x_ref[...], g_ref[...], sh_ref[...], sc_ref[...]).astype(BF16)
    scale = HD ** -0.5
    cos = cos_ref[...]
    sin = sin_ref[...]
    lane = lax.broadcasted_iota(jnp.int32, (TM, LANES), 1)
    first_half = (lane & (HD - 1)) < (HD // 2)
    lo = lane < HD
    kv = _dot(u, w_ref[:, 5 * BW:PROJ_W])
    k0, k1 = _dup_heads(_rope(kv[:, :LANES], cos, sin, first_half), lo)
    kw_ref[:, :LANES] = k0.astype(BF16)
    kw_ref[:, LANES:] = k1.astype(BF16)
    v0, v1 = _dup_heads(kv[:, LANES:], lo)
    vw_ref[:, :LANES] = v0.astype(BF16)
    vw_ref[:, LANES:] = v1.astype(BF16)
    wq = _dot(u, w_ref[:, 4 * BW:5 * BW])
    for j in range(BW // LANES):
        sl = slice(j * LANES, (j + 1) * LANES)
        qw_ref[:, sl] = (_rope(wq[:, sl], cos, sin, first_half) * scale).astype(BF16)
    qn_ref[...] = (_dot(u, w_ref[:, 0:BW]) * scale).astype(BF16)
    kn_ref[...] = _dot(u, w_ref[:, BW:2 * BW]).astype(BF16)
    vn_ref[...] = _dot(u, w_ref[:, 2 * BW:3 * BW]).astype(BF16)
    fu_ref[...] = _dot(u, w_ref[:, 3 * BW:4 * BW]).astype(BF16)


def _proj(h, n_tiles, g, mod3, layer, w_in, cos_t, sin_t, tile_info):
    n_lat_tiles, tiles_per_batch, _, _ = tile_info
    tile_spec = pl.BlockSpec((TM, D), lambda i: (i, 0))
    vec_spec = pl.BlockSpec((1, D), lambda i: (0, 0))
    rope_spec = pl.BlockSpec((TM, LANES), lambda i: (jnp.where(i < n_lat_tiles, i % tiles_per_batch, tiles_per_batch), 0))
    rows = n_tiles * TM

    def out_spec(w):
        return pl.BlockSpec((TM, w), lambda i: (i, 0))

    widths = [BW, BW, BW, BW, BW, 2 * LANES, 2 * LANES]
    return pl.pallas_call(
        _proj_kernel,
        grid=(n_tiles,),
        in_specs=[tile_spec, vec_spec,
                  _mod_spec(layer, 3, *tile_info), _mod_spec(layer, 4, *tile_info),
                  _layer_spec(w_in, layer), rope_spec, rope_spec],
        out_specs=[out_spec(w) for w in widths],
        out_shape=[jax.ShapeDtypeStruct((rows, w), BF16) for w in widths],
        compiler_params=_cparams(("arbitrary",)),
        name="proj",
    )(h, g.reshape(1, D), mod3, mod3, w_in, cos_t, sin_t)


def _lane_tiles(s):
    return [s[:, j * LANES:(j + 1) * LANES] for j in range(s.shape[1] // LANES)]


def _attend(qs, key_sets, sink_tile=None):
    scores = []
    for k, _, bias in key_sets:
        s = _dot_nt(qs, k)
        if bias is not None:
            s = s + bias
        scores.append(s)
    tiles = [t for s in scores for t in _lane_tiles(s)]
    if sink_tile is not None:
        tiles.append(sink_tile)
    m = functools.reduce(jnp.maximum, tiles).max(axis=-1, keepdims=True)
    psum = None
    acc = None
    for s, (_, v, _) in zip(scores, key_sets):
        p = jnp.exp(s - m)
        for t in _lane_tiles(p):
            psum = t if psum is None else psum + t
        o = _dot(p.astype(BF16), v)
        acc = o if acc is None else acc + o
    if sink_tile is not None:
        lane = lax.broadcasted_iota(jnp.int32, sink_tile.shape, 1)
        psum = psum + jnp.where(lane == 0, jnp.exp(sink_tile - m), 0.0)
    return acc / psum.sum(axis=-1, keepdims=True)


def _stack_heads(q, rows):
    lane = lax.broadcasted_iota(jnp.int32, (rows, LANES), 1)
    lo = lane < HD
    zero = jnp.zeros_like(q)
    return jnp.concatenate([jnp.where(lo, q, zero), jnp.where(lo, zero, q)], axis=0), lo


def _na_kernel(q_ref, k_ref, v_ref, kc_ref, vc_ref, bias_ref, o_ref, *, rows):
    j = pl.program_id(1)
    krow = jnp.clip(NA_QR * j - NA_WIN_R // 2, 0, rows - NA_KR)
    kstart = pl.multiple_of(krow * GRID_W, GRID_W)
    for p in range(NA_HEADS // 2):
        sl = slice(p * LANES, (p + 1) * LANES)
        kb = k_ref[pl.ds(kstart, NA_K), sl]
        vb = v_ref[pl.ds(kstart, NA_K), sl]
        qs, lo = _stack_heads(q_ref[:, sl], NA_Q)
        o = _attend(qs, [(kb, vb, bias_ref[p]), (kc_ref[:, sl], vc_ref[:, sl], None)])
        o_ref[:, sl] = jnp.where(lo, o[:NA_Q], o[NA_Q:]).astype(BF16)


def _na(qn, kn, vn, bias_tab, n_batch, seq, ctx_len):
    rows = seq // GRID_W
    n_blk = rows // NA_QR
    n_pair = NA_HEADS // 2
    ctx_blk0 = (n_batch * seq) // ctx_len

    def pat(j):
        return jnp.where(j == 0, 0, jnp.where(j == n_blk - 1, 2, 1))

    return pl.pallas_call(
        functools.partial(_na_kernel, rows=rows),
        grid=(n_batch, n_blk),
        in_specs=[
            pl.BlockSpec((NA_Q, BW), lambda b, j: (b * n_blk + j, 0)),
            pl.BlockSpec((seq, BW), lambda b, j: (b, 0)),
            pl.BlockSpec((seq, BW), lambda b, j: (b, 0)),
            pl.BlockSpec((ctx_len, BW), lambda b, j: (ctx_blk0 + b, 0)),
            pl.BlockSpec((ctx_len, BW), lambda b, j: (ctx_blk0 + b, 0)),
            pl.BlockSpec((None, n_pair, 2 * NA_Q, NA_K), lambda b, j: (pat(j), 0, 0, 0)),
        ],
        out_specs=pl.BlockSpec((NA_Q, BW), lambda b, j: (b * n_blk + j, 0)),
        out_shape=jax.ShapeDtypeStruct((n_batch * seq, BW), BF16),
        compiler_params=_cparams(("arbitrary", "arbitrary")),
        name="na",
    )(qn, kn, vn, kn, vn, bias_tab)


def _na_bias_table(bias, rows):
    n_blk = rows // NA_QR
    h = bias.shape[0]
    n_dc = 2 * NA_WIN_C - 1
    qc = np.arange(GRID_W)[:, None]
    kc = np.arange(GRID_W)[None, :]
    ws = np.clip(qc - NA_WIN_C // 2, 0, GRID_W - NA_WIN_C)
    col_ok = (kc >= ws) & (kc < ws + NA_WIN_C)
    dc = np.clip(kc - qc, -(NA_WIN_C - 1), NA_WIN_C - 1) + NA_WIN_C - 1
    onehot = (dc[None] == np.arange(n_dc)[:, None, None]).astype(np.float32)
    toep = jnp.einsum('hrd,dqk->hrqk', bias.astype(F32), jnp.asarray(onehot), precision=lax.Precision.HIGHEST)
    toep = jnp.where(col_ok[None, None], toep, NEG_INF)
    masked = jnp.full((h, GRID_W, GRID_W), NEG_INF, F32)
    pats = []
    for blk in (0, 1, n_blk - 1):
        k0 = min(max(NA_QR * blk - NA_WIN_R // 2, 0), rows - NA_KR)
        q_rows = []
        for i in range(NA_QR):
            r = NA_QR * blk + i
            rs = min(max(r - NA_WIN_R // 2, 0), rows - NA_WIN_R)
            blocks = []
            for t in range(NA_KR):
                kr = k0 + t
                blocks.append(toep[:, kr - r + NA_WIN_R - 1] if rs <= kr < rs + NA_WIN_R else masked)
            q_rows.append(jnp.concatenate(blocks, axis=-1))
        pats.append(jnp.concatenate(q_rows, axis=1).reshape(h // 2, 2 * NA_Q, NA_K))
    return jnp.stack(pats, axis=0)


def _stack_group(q):
    rows = q.shape[0]
    a, lo = _stack_heads(q[:, :LANES], rows)
    b, _ = _stack_heads(q[:, LANES:], rows)
    return jnp.concatenate([a, b], axis=0), lo


def _unstack_group(o, rows, lo):
    oa = jnp.where(lo, o[0:rows], o[rows:2 * rows])
    ob = jnp.where(lo, o[2 * rows:3 * rows], o[3 * rows:4 * rows])
    return jnp.concatenate([oa, ob], axis=1)


def _sink_tile(sink_ref, g, rows):
    gq = WA_HEADS // WA_KV_HEADS
    return jnp.concatenate([jnp.full((rows, LANES), sink_ref[g * gq + h], F32) for h in range(gq)], axis=0)


def _wa_kernel(sink_ref, q_ref, k_ref, v_ref, kc_ref, vc_ref, o_ref, *, seq):
    for t in range(WA_QB):
        n = pl.program_id(1) * WA_QB + t
        rows = slice(t * WA_Q, (t + 1) * WA_Q)
        kstart = pl.multiple_of(jnp.clip(n * WA_Q - WA_Q, 0, seq - WA_K), WA_Q)
        qpos = n * WA_Q + lax.broadcasted_iota(jnp.int32, (WA_Q, WA_K), 0)
        kpos = kstart + lax.broadcasted_iota(jnp.int32, (WA_Q, WA_K), 1)
        band = jnp.where(jnp.abs(kpos - qpos) <= WA_WINDOW, 0.0, NEG_INF).astype(F32)
        band = jnp.concatenate([band] * (WA_HEADS // WA_KV_HEADS), axis=0)
        for g in range(WA_KV_HEADS):
            sl = slice(g * LANES, (g + 1) * LANES)
            sl2 = slice(g * 2 * LANES, (g + 1) * 2 * LANES)
            kb = k_ref[pl.ds(kstart, WA_K), sl]
            vb = v_ref[pl.ds(kstart, WA_K), sl]
            qs, lo = _stack_group(q_ref[rows, sl2])
            o = _attend(qs, [(kb, vb, band), (kc_ref[:, sl], vc_ref[:, sl], None)], _sink_tile(sink_ref, g, WA_Q))
            o_ref[rows, sl2] = _unstack_group(o, WA_Q, lo).astype(BF16)


def _wa(sink, qw, kw, vw, n_batch, seq, ctx_len):
    n_blk = seq // (WA_Q * WA_QB)
    ctx_blk0 = (n_batch * seq) // ctx_len
    gw = 2 * LANES
    return pl.pallas_call(
        functools.partial(_wa_kernel, seq=seq),
        grid=(n_batch, n_blk),
        in_specs=[
            pl.BlockSpec(memory_space=pltpu.SMEM),
            pl.BlockSpec((WA_Q * WA_QB, BW), lambda b, n: (b * n_blk + n, 0)),
            pl.BlockSpec((seq, gw), lambda b, n: (b, 0)),
            pl.BlockSpec((seq, gw), lambda b, n: (b, 0)),
            pl.BlockSpec((ctx_len, gw), lambda b, n: (ctx_blk0 + b, 0)),
            pl.BlockSpec((ctx_len, gw), lambda b, n: (ctx_blk0 + b, 0)),
        ],
        out_specs=pl.BlockSpec((WA_Q * WA_QB, BW), lambda b, n: (b * n_blk + n, 0)),
        out_shape=jax.ShapeDtypeStruct((n_batch * seq, BW), BF16),
        compiler_params=_cparams(("arbitrary", "arbitrary")),
        name="wa",
    )(sink, qw, kw, vw, kw, vw)


FFT_BLK = 16


def _swap_major(x):
    return pltpu.einshape("abc->bac", x)


def _fft1_kernel(f_ref, x_ref, ar_ref, ai_ref, a_scr):
    xt = _swap_major(x_ref[...].astype(F32)).astype(BF16)
    f = f_ref[...]
    for i in range(FFT_BLK):
        a_scr[i] = _dot(f, xt[i])
    at = _swap_major(a_scr[...])
    ar_ref[...] = at[:GRID_W].astype(BF16)
    ai_ref[...] = at[GRID_W:].astype(BF16)


def _fft2_kernel(m_ref, ar_ref, ai_ref, zr_ref, zi_ref, z_scr):
    for t in range(FFT_BLK):
        a = jnp.concatenate([ar_ref[t], ai_ref[t]], axis=0)
        z_scr[t] = _dot(m_ref[t], a)
    zt = _swap_major(z_scr[...])
    zr_ref[...] = zt[:GRID_W].astype(BF16)
    zi_ref[...] = zt[GRID_W:].astype(BF16)


def _fft_tables():
    n = GRID_W
    k = np.arange(n)
    ang1 = 2.0 * np.pi * ((k[:, None] * k[None, :]) % n) / n
    f1 = np.concatenate([np.cos(ang1), -np.sin(ang1)], axis=0) / 8.0
    ka = k[:, None, None]
    kb = k[None, :, None]
    n1 = k[None, None, :]
    ang2 = 2.0 * np.pi * ((n1 * (ka + n * kb)) % (n * n)) / (n * n)
    mr, mi = np.cos(ang2), -np.sin(ang2)
    m2 = np.concatenate([np.concatenate([mr, -mi], axis=2), np.concatenate([mi, mr], axis=2)], axis=1) / 8.0
    c = np.arange(FN_GROUP_DIM)
    angc = 2.0 * np.pi * ((c[:, None] * c[None, :]) % FN_GROUP_DIM) / FN_GROUP_DIM
    eye = np.eye(FN_GROUPS)
    cbd = np.kron(eye, np.cos(angc)) / 8.0
    sbd = np.kron(eye, np.sin(angc)) / 8.0
    return f1, m2, cbd, sbd


def _ctx_dft_table(ctx_len):
    k = np.arange(ctx_len)
    ang = 2.0 * np.pi * ((k[:, None] * k[None, :]) % ctx_len) / ctx_len
    return np.concatenate([np.cos(ang), -np.sin(ang)], axis=0) / np.sqrt(ctx_len)


def _fft(fu, f1, m2, n_batch, seq):
    n = GRID_W
    n_blk = n // FFT_BLK
    x3 = fu.reshape(-1, n, BW)
    col_spec = pl.BlockSpec((n, FFT_BLK, BW), lambda b, j: (b, j, 0))
    row_spec = pl.BlockSpec((FFT_BLK, n, BW), lambda b, j: (b * n_blk + j, 0, 0))
    shape3 = jax.ShapeDtypeStruct((n_batch * n, n, BW), BF16)
    scratch = [pltpu.VMEM((FFT_BLK, 2 * n, BW), F32)]
    ar, ai = pl.pallas_call(
        _fft1_kernel,
        grid=(n_batch, n_blk),
        in_specs=[pl.BlockSpec((2 * n, n), lambda b, j: (0, 0)), col_spec],
        out_specs=[col_spec, col_spec],
        out_shape=[shape3, shape3],
        scratch_shapes=scratch,
        compiler_params=_cparams(("arbitrary", "arbitrary")),
        name="fft1",
    )(f1, x3)
    zr, zi = pl.pallas_call(
        _fft2_kernel,
        grid=(n_batch, n_blk),
        in_specs=[pl.BlockSpec((FFT_BLK, 2 * n, 2 * n), lambda b, j: (j, 0, 0)), row_spec, row_spec],
        out_specs=[col_spec, col_spec],
        out_shape=[shape3, shape3],
        scratch_shapes=scratch,
        compiler_params=_cparams(("arbitrary", "arbitrary")),
        name="fft2",
    )(m2, ar, ai)
    return zr.reshape(n_batch * seq, BW), zi.reshape(n_batch * seq, BW)


def _ctx_kernel(sink_ref, qn_ref, kn_ref, vn_ref, fu_ref, qw_ref, kw_ref, vw_ref, dft_ref,
                a_ref, w_ref, zr_ref, zi_ref, *, ctx_len):
    for p in range(NA_HEADS // 2):
        sl = slice(p * LANES, (p + 1) * LANES)
        qs, lo = _stack_heads(qn_ref[:, sl], ctx_len)
        o = _attend(qs, [(kn_ref[:, sl], vn_ref[:, sl], None)])
        a_ref[:, sl] = jnp.where(lo, o[:ctx_len], o[ctx_len:]).astype(BF16)
    for g in range(WA_KV_HEADS):
        sl = slice(g * LANES, (g + 1) * LANES)
        sl2 = slice(g * 2 * LANES, (g + 1) * 2 * LANES)
        qs, lo = _stack_group(qw_ref[:, sl2])
        o = _attend(qs, [(kw_ref[:, sl], vw_ref[:, sl], None)], _sink_tile(sink_ref, g, ctx_len))
        w_ref[:, sl2] = _unstack_group(o, ctx_len, lo).astype(BF16)
    z = _dot(dft_ref[...], fu_ref[...])
    zr_ref[...] = z[:ctx_len].astype(BF16)
    zi_ref[...] = z[ctx_len:].astype(BF16)


def _ctx_mix(sink, qn, kn, vn, fu, qw, kw, vw, dft_c, n_batch, seq, ctx_len):
    blk0 = (n_batch * seq) // ctx_len

    def in_spec(w):
        return pl.BlockSpec((ctx_len, w), lambda b: (blk0 + b, 0))

    out_spec = pl.BlockSpec((ctx_len, BW), lambda b: (b, 0))
    return pl.pallas_call(
        functools.partial(_ctx_kernel, ctx_len=ctx_len),
        grid=(n_batch,),
        in_specs=[pl.BlockSpec(memory_space=pltpu.SMEM),
                  in_spec(BW), in_spec(BW), in_spec(BW), in_spec(BW), in_spec(BW),
                  in_spec(2 * LANES), in_spec(2 * LANES),
                  pl.BlockSpec(dft_c.shape, lambda b: (0, 0))],
        out_specs=[out_spec] * 4,
        out_shape=[jax.ShapeDtypeStruct((n_batch * ctx_len, BW), BF16)] * 4,
        compiler_params=_cparams(("arbitrary",)),
        name="ctx_mix",
    )(sink, qn, kn, vn, fu, qw, kw, vw, dft_c)


def _merge_kernel(x_ref, g_ref, sh_ref, sc_ref, gt_ref, a_ref, zr_ref, zi_ref, w_ref, *rest, n_lat_tiles):
    if n_lat_tiles is None:
        wg_ref, wbr_ref, cbd_ref, sbd_ref, wout_ref, o_ref = rest
        a, zr, zi, w = a_ref[...], zr_ref[...], zi_ref[...], w_ref[...]
    else:
        ac_ref, zrc_ref, zic_ref, wc_ref, wg_ref, wbr_ref, cbd_ref, sbd_ref, wout_ref, o_ref = rest
        is_ctx = pl.program_id(0) >= n_lat_tiles
        a = jnp.where(is_ctx, ac_ref[...], a_ref[...])
        zr = jnp.where(is_ctx, zrc_ref[...], zr_ref[...])
        zi = jnp.where(is_ctx, zic_ref[...], zi_ref[...])
        w = jnp.where(is_ctx, wc_ref[...], w_ref[...])
    x = x_ref[...]
    u = _norm_mod(x, g_ref[...], sh_ref[...], sc_ref[...]).astype(BF16)
    f = (_dot(zr, cbd_ref[...]) + _dot(zi, sbd_ref[...])).astype(BF16)
    acc = None
    for i, br in enumerate((a, f, w)):
        gate = _sigmoid(_dot(u, wg_ref[:, PROJ_W + i * D:PROJ_W + (i + 1) * D]))
        term = gate * _dot(br, wbr_ref[i])
        acc = term if acc is None else acc + term
    o_ref[...] = x + gt_ref[...] * _dot(acc.astype(BF16), wout_ref[...])


def _merge(h, n_tiles, g, mod3, layer, branches, ctx_branches, weights, tile_info):
    n_lat_tiles = tile_info[0]
    tile_spec = pl.BlockSpec((TM, D), lambda i: (i, 0))
    vec_spec = pl.BlockSpec((1, D), lambda i: (0, 0))
    has_ctx = ctx_branches is not None
    lat_spec = pl.BlockSpec((TM, BW), lambda i: (jnp.minimum(i, n_lat_tiles - 1), 0))
    ctx_spec = pl.BlockSpec((TM, BW), lambda i: (jnp.maximum(i - n_lat_tiles, 0), 0))
    in_specs = [tile_spec, vec_spec,
                _mod_spec(layer, 3, *tile_info), _mod_spec(layer, 4, *tile_info), _mod_spec(layer, 5, *tile_info)]
    in_specs += [lat_spec] * 4
    args = [h, g.reshape(1, D), mod3, mod3, mod3, *branches]
    if has_ctx:
        in_specs += [ctx_spec] * 4
        args += list(ctx_branches)
    w_in, w_br, cbd, sbd, w_out = weights
    in_specs += [_layer_spec(w_in, layer), _layer_spec(w_br, layer), _const_spec(cbd.shape), _const_spec(sbd.shape),
                 _layer_spec(w_out, layer)]
    args += list(weights)
    return pl.pallas_call(
        functools.partial(_merge_kernel, n_lat_tiles=n_lat_tiles if has_ctx else None),
        grid=(n_tiles,),
        in_specs=in_specs,
        out_specs=tile_spec,
        out_shape=jax.ShapeDtypeStruct((n_tiles * TM, D), F32),
        compiler_params=_cparams(("arbitrary",)),
        name="merge_ctx" if has_ctx else "merge",
    )(*args)


def _rope_tables(seq):
    t = np.arange(seq)
    row = (t // GRID_W).astype(np.float64)
    col = (t % GRID_W).astype(np.float64)
    n_freq = HD // 4
    inv = ROPE_BASE ** (-np.arange(n_freq, dtype=np.float64) / n_freq)
    ang = np.concatenate([row[:, None] * inv, col[:, None] * inv], axis=-1)
    cos, sin = np.cos(ang), np.sin(ang)
    cos_h = np.concatenate([cos, cos], axis=1)
    sin_h = np.concatenate([-sin, sin], axis=1)
    cos2 = np.concatenate([np.tile(cos_h, (1, 2)), np.ones((TM, LANES), np.float32)], axis=0)
    sin2 = np.concatenate([np.tile(sin_h, (1, 2)), np.zeros((TM, LANES), np.float32)], axis=0)
    return jnp.asarray(cos2, F32), jnp.asarray(sin2, F32)


def kernel(x, c, ctx, c_ctx, w_ada, b_ada, g_ffn1, ffn1_w13, ffn1_w2, g_mix, w_in, na_bias, wa_sink,
           w_br, w_out, g_ffn2, ffn2_w13, ffn2_w2, g_final):
    n_batch, seq, _ = x.shape
    ctx_len = ctx.shape[1]
    depth = w_ada.shape[0]
    rows = seq // GRID_W
    n_lat = n_batch * seq
    n_lat_tiles = n_lat // TM
    n_all_tiles = (n_lat + n_batch * ctx_len) // TM
    mod_rows = 8
    tile_info = (n_lat_tiles, seq // TM, n_batch, mod_rows)

    cc = jnp.concatenate([c, c_ctx[None], jnp.zeros((mod_rows - n_batch - 1, D), F32)], axis=0)
    mod3 = _ada(cc, w_ada, b_ada).reshape(depth * mod_rows, 1, N_MOD * D)

    cos_t, sin_t = _rope_tables(seq)
    f1, m2, cbd, sbd = (jnp.asarray(t, F32).astype(BF16) for t in _fft_tables())
    dft_c = jnp.asarray(_ctx_dft_table(ctx_len), F32).astype(BF16)

    w13a, w2a = ffn1_w13.astype(BF16), ffn1_w2.astype(BF16)
    w13b, w2b = ffn2_w13.astype(BF16), ffn2_w2.astype(BF16)
    w_in_b = w_in.astype(BF16)
    merge_w = (w_in_b, w_br.astype(BF16), cbd, sbd, w_out.astype(BF16))

    h = x.reshape(n_lat, D)
    h_ctx = ctx.reshape(n_batch * ctx_len, D)
    for l in range(depth):
        last = l == depth - 1
        bias_tab = _na_bias_table(na_bias[l], rows)

        h = _ffn(h, n_all_tiles, g_ffn1[l], mod3, l, 0, w13a, w2a, tile_info, h_ctx=h_ctx if l == 0 else None)
        qn, kn, vn, fu, qw, kw, vw = _proj(h, n_all_tiles, g_mix[l], mod3, l, w_in_b, cos_t, sin_t, tile_info)
        a = _na(qn, kn, vn, bias_tab, n_batch, seq, ctx_len)
        w = _wa(wa_sink[l], qw, kw, vw, n_batch, seq, ctx_len)
        zr, zi = _fft(fu, f1, m2, n_batch, seq)
        if last:
            h = _merge(h, n_lat_tiles, g_mix[l], mod3, l, (a, zr, zi, w), None, merge_w, tile_info)
            h = _ffn(h, n_lat_tiles, g_ffn2[l], mod3, l, 6, w13b, w2b, tile_info, g_final=g_final)
        else:
            ctx_br = _ctx_mix(wa_sink[l], qn, kn, vn, fu, qw, kw, vw, dft_c, n_batch, seq, ctx_len)
            ac, wc, zrc, zic = ctx_br
            h = _merge(h, n_all_tiles, g_mix[l], mod3, l, (a, zr, zi, w), (ac, zrc, zic, wc), merge_w, tile_info)
            h = _ffn(h, n_all_tiles, g_ffn2[l], mod3, l, 6, w13b, w2b, tile_info)
    return h.reshape(n_batch, seq, D)
```

```python
import functools

import numpy as np
import jax
import jax.numpy as jnp
from jax import lax
from jax.experimental import pallas as pl
from jax.experimental.pallas import tpu as pltpu

D = 1024
GRID_W = 64
HD = 64
NA_HEADS = 8
NA_WIN_R = 8
NA_WIN_C = 16
FN_GROUPS = 8
FN_GROUP_DIM = 64
WA_HEADS = 8
WA_KV_HEADS = 2
WA_WINDOW = 128
D_FF = 2816
N_MOD = 9
ROPE_BASE = 10000.0
EPS = 1e-6
NEG_INF = -1e30
LOG2E = 1.4426950408889634
BW = 512

LANES = 128
MXU_DIM = 256
TM = 512
TF = MXU_DIM
VMEM_LIMIT = 56 * 1024 * 1024

NA_QR = 4
NA_KR = 12
NA_Q = NA_QR * GRID_W
NA_K = NA_KR * GRID_W
WA_Q = 128
WA_K = 3 * WA_Q
WA_QB = 4

F32 = jnp.float32
BF16 = jnp.bfloat16


def _cparams(sem):
    return pltpu.CompilerParams(dimension_semantics=sem, vmem_limit_bytes=VMEM_LIMIT)


def _const_spec(shape):
    nd = len(shape)
    return pl.BlockSpec(shape, lambda *_: (0,) * nd, pipeline_mode=pl.Buffered(1))


def _layer_spec(stacked, layer):
    nd = stacked.ndim
    return pl.BlockSpec((None,) + stacked.shape[1:], lambda *_: (layer,) + (0,) * (nd - 1),
                        pipeline_mode=pl.Buffered(1))


def _sigmoid(x):
    return 1.0 / (1.0 + jnp.exp(-x))


def _norm_mod(x, g, shift, scale):
    y = x * lax.rsqrt(jnp.mean(x * x, axis=-1, keepdims=True) + EPS) * g
    return y * (1.0 + scale) + shift


def _dot(a, b):
    return jnp.dot(a, b, preferred_element_type=F32)


def _dot_nt(a, b):
    return lax.dot_general(a, b, (((1,), (1,)), ((), ())), preferred_element_type=F32)


def _ada_kernel(c_ref, w_ref, b_ref, o_ref):
    x = c_ref[...]
    sx = (x * _sigmoid(x)).astype(BF16)
    o_ref[...] = _dot(sx, w_ref[...].astype(BF16)) + b_ref[...]


def _ada(cc, w_ada, b_ada):
    depth = w_ada.shape[0]
    n = w_ada.shape[2]
    tn = 1536
    rows = cc.shape[0]
    return pl.pallas_call(
        _ada_kernel,
        grid=(depth, n // tn),
        in_specs=[
            pl.BlockSpec((rows, D), lambda l, j: (0, 0)),
            pl.BlockSpec((None, D, tn), lambda l, j: (l, 0, j)),
            pl.BlockSpec((None, 1, tn), lambda l, j: (l, 0, j)),
        ],
        out_specs=pl.BlockSpec((None, rows, tn), lambda l, j: (l, 0, j)),
        out_shape=jax.ShapeDtypeStruct((depth, rows, n), F32),
        compiler_params=_cparams(("arbitrary", "arbitrary")),
        name="ada",
    )(cc, w_ada, b_ada.reshape(depth, 1, n))


def _mod_spec(layer, k, n_lat_tiles, tiles_per_batch, n_batch, mod_rows):
    def idx(i):
        row = jnp.where(i < n_lat_tiles, i // tiles_per_batch, n_batch)
        return (layer * mod_rows + row, 0, k)

    return pl.BlockSpec((None, 1, D), idx)


def _ffn_kernel(*refs, final, split_at):
    refs = list(refs)
    x_ref = refs.pop(0)
    if split_at is not None:
        xc_ref = refs.pop(0)
    g_ref, sh_ref, sc_ref, gt_ref, w13_ref, w2_ref = refs[:6]
    if final:
        gf_ref, o_ref, hm_ref = refs[6:]
    else:
        o_ref, hm_ref = refs[6:]
    x = x_ref[...]
    if split_at is not None:
        x = jnp.where(pl.program_id(0) >= split_at, xc_ref[...], x)
    u = _norm_mod(x, g_ref[...], sh_ref[...], sc_ref[...]).astype(BF16)
    for lo in range(0, D_FF, TF):
        hi = min(lo + TF, D_FF)
        a = _dot(u, w13_ref[:, lo:hi])
        b = _dot(u, w13_ref[:, D_FF + lo:D_FF + hi])
        hm_ref[:, lo:hi] = (a * _sigmoid(a) * b).astype(BF16)
    f = _dot(hm_ref[...], w2_ref[...])
    out = x + 0.5 * gt_ref[...] * f
    if final:
        out = out * lax.rsqrt(jnp.mean(out * out, axis=-1, keepdims=True) + EPS) * gf_ref[...]
    o_ref[...] = out


def _ffn(h, n_tiles, g, mod3, layer, mod_k, w13, w2, tile_info, g_final=None, h_ctx=None):
    final = g_final is not None
    n_lat_tiles = tile_info[0]
    tile_spec = pl.BlockSpec((TM, D), lambda i: (i, 0))
    vec_spec = pl.BlockSpec((1, D), lambda i: (0, 0))
    if h_ctx is None:
        in_specs, args = [tile_spec], [h]
    else:
        in_specs = [pl.BlockSpec((TM, D), lambda i: (jnp.minimum(i, n_lat_tiles - 1), 0)),
                    pl.BlockSpec((TM, D), lambda i: (jnp.maximum(i - n_lat_tiles, 0), 0))]
        args = [h, h_ctx]
    in_specs += [
        vec_spec,
        _mod_spec(layer, mod_k, *tile_info),
        _mod_spec(layer, mod_k + 1, *tile_info),
        _mod_spec(layer, mod_k + 2, *tile_info),
        _layer_spec(w13, layer),
        _layer_spec(w2, layer),
    ]
    args += [g.reshape(1, D), mod3, mod3, mod3, w13, w2]
    if final:
        in_specs.append(vec_spec)
        args.append(g_final.reshape(1, D))
    return pl.pallas_call(
        functools.partial(_ffn_kernel, final=final, split_at=None if h_ctx is None else n_lat_tiles),
        grid=(n_tiles,),
        in_specs=in_specs,
        out_specs=tile_spec,
        out_shape=jax.ShapeDtypeStruct((n_tiles * TM, D), F32),
        scratch_shapes=[pltpu.VMEM((TM, D_FF), BF16)],
        compiler_params=_cparams(("arbitrary",)),
        name="ffn_final" if final else ("ffn" if h_ctx is None else "ffn_split"),
    )(*args)


PROJ_W = 5 * BW + 2 * LANES


def _rope(t, cos, sin_signed, first_half):
    partner = jnp.where(first_half, pltpu.roll(t, LANES - HD // 2, axis=1), pltpu.roll(t, HD // 2, axis=1))
    return t * cos + partner * sin_signed


def _dup_heads(t, lo):
    sw = pltpu.roll(t, HD, axis=1)
    return jnp.where(lo, t, sw), jnp.where(lo, sw, t)


def _proj_kernel(x_ref, g_ref, sh_ref, sc_ref, w_ref, cos_ref, sin_ref,
                 qn_ref, kn_ref, vn_ref, fu_ref, qw_ref, kw_ref, vw_ref):
    u = _norm_mod(x_ref[...], g_ref[...], sh_ref[...], sc_ref[...]).astype(BF16)
    scale = HD ** -0.5 * LOG2E
    cos = cos_ref[...]
    sin = sin_ref[...]
    lane = lax.broadcasted_iota(jnp.int32, (TM, LANES), 1)
    first_half = (lane & (HD - 1)) < (HD // 2)
    lo = lane < HD
    kv = _dot(u, w_ref[:, 5 * BW:PROJ_W])
    k0, k1 = _dup_heads(_rope(kv[:, :LANES], cos, sin, first_half), lo)
    kw_ref[:, :LANES] = k0.astype(BF16)
    kw_ref[:, LANES:] = k1.astype(BF16)
    v0, v1 = _dup_heads(kv[:, LANES:], lo)
    vw_ref[:, :LANES] = v0.astype(BF16)
    vw_ref[:, LANES:] = v1.astype(BF16)
    wq = _dot(u, w_ref[:, 4 * BW:5 * BW])
    for j in range(BW // LANES):
        sl = slice(j * LANES, (j + 1) * LANES)
        qw_ref[:, sl] = (_rope(wq[:, sl], cos, sin, first_half) * scale).astype(BF16)
    qn_ref[...] = (_dot(u, w_ref[:, 0:BW]) * scale).astype(BF16)
    kn_ref[...] = _dot(u, w_ref[:, BW:2 * BW]).astype(BF16)
    vn_ref[...] = _dot(u, w_ref[:, 2 * BW:3 * BW]).astype(BF16)
    fu_ref[...] = _dot(u, w_ref[:, 3 * BW:4 * BW]).astype(BF16)


def _proj(h, n_tiles, g, mod3, layer, w_in, cos_t, sin_t, tile_info):
    n_lat_tiles, tiles_per_batch, _, _ = tile_info
    tile_spec = pl.BlockSpec((TM, D), lambda i: (i, 0))
    vec_spec = pl.BlockSpec((1, D), lambda i: (0, 0))
    rope_spec = pl.BlockSpec((TM, LANES), lambda i: (jnp.where(i < n_lat_tiles, i % tiles_per_batch, tiles_per_batch), 0))
    rows = n_tiles * TM

    def out_spec(w):
        return pl.BlockSpec((TM, w), lambda i: (i, 0))

    widths = [BW, BW, BW, BW, BW, 2 * LANES, 2 * LANES]
    return pl.pallas_call(
        _proj_kernel,
        grid=(n_tiles,),
        in_specs=[tile_spec, vec_spec,
                  _mod_spec(layer, 3, *tile_info), _mod_spec(layer, 4, *tile_info),
                  _layer_spec(w_in, layer), rope_spec, rope_spec],
        out_specs=[out_spec(w) for w in widths],
        out_shape=[jax.ShapeDtypeStruct((rows, w), BF16) for w in widths],
        compiler_params=_cparams(("arbitrary",)),
        name="proj",
    )(h, g.reshape(1, D), mod3, mod3, w_in, cos_t, sin_t)


def _lane_tiles(s):
    return [s[:, j * LANES:(j + 1) * LANES] for j in range(s.shape[1] // LANES)]


def _attend(qs, key_sets, sink_tile=None):
    scores = []
    for k, _, bias in key_sets:
        s = _dot_nt(qs, k)
        if bias is not None:
            s = s + bias
        scores.append(s)
    tiles = [t for s in scores for t in _lane_tiles(s)]
    if sink_tile is not None:
        tiles.append(sink_tile)
    m = functools.reduce(jnp.maximum, tiles).max(axis=-1, keepdims=True)
    psum = None
    acc = None
    for s, (_, v, _) in zip(scores, key_sets):
        p = jnp.exp2(s - m)
        for t in _lane_tiles(p):
            psum = t if psum is None else psum + t
        o = _dot(p.astype(BF16), v)
        acc = o if acc is None else acc + o
    if sink_tile is not None:
        lane = lax.broadcasted_iota(jnp.int32, sink_tile.shape, 1)
        psum = psum + jnp.where(lane == 0, jnp.exp2(sink_tile - m), 0.0)
    return acc / psum.sum(axis=-1, keepdims=True)


def _stack_heads(q, rows):
    lane = lax.broadcasted_iota(jnp.int32, (rows, LANES), 1)
    lo = lane < HD
    zero = jnp.zeros_like(q)
    return jnp.concatenate([jnp.where(lo, q, zero), jnp.where(lo, zero, q)], axis=0), lo


def _na_kernel(q_ref, k_ref, v_ref, kc_ref, vc_ref, bias_ref, o_ref, *, rows):
    j = pl.program_id(1)
    krow = jnp.clip(NA_QR * j - NA_WIN_R // 2, 0, rows - NA_KR)
    kstart = pl.multiple_of(krow * GRID_W, GRID_W)
    for p in range(NA_HEADS // 2):
        sl = slice(p * LANES, (p + 1) * LANES)
        kb = k_ref[pl.ds(kstart, NA_K), sl]
        vb = v_ref[pl.ds(kstart, NA_K), sl]
        qs, lo = _stack_heads(q_ref[:, sl], NA_Q)
        o = _attend(qs, [(kb, vb, bias_ref[p]), (kc_ref[:, sl], vc_ref[:, sl], None)])
        o_ref[:, sl] = jnp.where(lo, o[:NA_Q], o[NA_Q:]).astype(BF16)


def _na(qn, kn, vn, bias_tab, n_batch, seq, ctx_len):
    rows = seq // GRID_W
    n_blk = rows // NA_QR
    n_pair = NA_HEADS // 2
    ctx_blk0 = (n_batch * seq) // ctx_len

    def pat(j):
        return jnp.where(j == 0, 0, jnp.where(j == n_blk - 1, 2, 1))

    return pl.pallas_call(
        functools.partial(_na_kernel, rows=rows),
        grid=(n_batch, n_blk),
        in_specs=[
            pl.BlockSpec((NA_Q, BW), lambda b, j: (b * n_blk + j, 0)),
            pl.BlockSpec((seq, BW), lambda b, j: (b, 0)),
            pl.BlockSpec((seq, BW), lambda b, j: (b, 0)),
            pl.BlockSpec((ctx_len, BW), lambda b, j: (ctx_blk0 + b, 0)),
            pl.BlockSpec((ctx_len, BW), lambda b, j: (ctx_blk0 + b, 0)),
            pl.BlockSpec((None, n_pair, 2 * NA_Q, NA_K), lambda b, j: (pat(j), 0, 0, 0)),
        ],
        out_specs=pl.BlockSpec((NA_Q, BW), lambda b, j: (b * n_blk + j, 0)),
        out_shape=jax.ShapeDtypeStruct((n_batch * seq, BW), BF16),
        compiler_params=_cparams(("arbitrary", "arbitrary")),
        name="na",
    )(qn, kn, vn, kn, vn, bias_tab)


def _na_bias_table(bias, rows):
    n_blk = rows // NA_QR
    h = bias.shape[0]
    n_dc = 2 * NA_WIN_C - 1
    qc = np.arange(GRID_W)[:, None]
    kc = np.arange(GRID_W)[None, :]
    ws = np.clip(qc - NA_WIN_C // 2, 0, GRID_W - NA_WIN_C)
    col_ok = (kc >= ws) & (kc < ws + NA_WIN_C)
    dc = np.clip(kc - qc, -(NA_WIN_C - 1), NA_WIN_C - 1) + NA_WIN_C - 1
    onehot = (dc[None] == np.arange(n_dc)[:, None, None]).astype(np.float32)
    toep = jnp.einsum('hrd,dqk->hrqk', bias.astype(F32), jnp.asarray(onehot), precision=lax.Precision.HIGHEST)
    toep = jnp.where(col_ok[None, None], toep * LOG2E, NEG_INF)
    masked = jnp.full((h, GRID_W, GRID_W), NEG_INF, F32)
    pats = []
    for blk in (0, 1, n_blk - 1):
        k0 = min(max(NA_QR * blk - NA_WIN_R // 2, 0), rows - NA_KR)
        q_rows = []
        for i in range(NA_QR):
            r = NA_QR * blk + i
            rs = min(max(r - NA_WIN_R // 2, 0), rows - NA_WIN_R)
            blocks = []
            for t in range(NA_KR):
                kr = k0 + t
                blocks.append(toep[:, kr - r + NA_WIN_R - 1] if rs <= kr < rs + NA_WIN_R else masked)
            q_rows.append(jnp.concatenate(blocks, axis=-1))
        pats.append(jnp.concatenate(q_rows, axis=1).reshape(h // 2, 2 * NA_Q, NA_K))
    return jnp.stack(pats, axis=0)


def _stack_group(q):
    rows = q.shape[0]
    a, lo = _stack_heads(q[:, :LANES], rows)
    b, _ = _stack_heads(q[:, LANES:], rows)
    return jnp.concatenate([a, b], axis=0), lo


def _unstack_group(o, rows, lo):
    oa = jnp.where(lo, o[0:rows], o[rows:2 * rows])
    ob = jnp.where(lo, o[2 * rows:3 * rows], o[3 * rows:4 * rows])
    return jnp.concatenate([oa, ob], axis=1)


def _sink_tile(sink_ref, g, rows):
    gq = WA_HEADS // WA_KV_HEADS
    return jnp.concatenate([jnp.full((rows, LANES), sink_ref[g * gq + h] * LOG2E, F32) for h in range(gq)], axis=0)


def _wa_kernel(sink_ref, q_ref, k_ref, v_ref, kc_ref, vc_ref, o_ref, *, seq):
    for t in range(WA_QB):
        n = pl.program_id(1) * WA_QB + t
        rows = slice(t * WA_Q, (t + 1) * WA_Q)
        kstart = pl.multiple_of(jnp.clip(n * WA_Q - WA_Q, 0, seq - WA_K), WA_Q)
        qpos = n * WA_Q + lax.broadcasted_iota(jnp.int32, (WA_Q, WA_K), 0)
        kpos = kstart + lax.broadcasted_iota(jnp.int32, (WA_Q, WA_K), 1)
        band = jnp.where(jnp.abs(kpos - qpos) <= WA_WINDOW, 0.0, NEG_INF).astype(F32)
        band = jnp.concatenate([band] * (WA_HEADS // WA_KV_HEADS), axis=0)
        for g in range(WA_KV_HEADS):
            sl = slice(g * LANES, (g + 1) * LANES)
            sl2 = slice(g * 2 * LANES, (g + 1) * 2 * LANES)
            kb = k_ref[pl.ds(kstart, WA_K), sl]
            vb = v_ref[pl.ds(kstart, WA_K), sl]
            qs, lo = _stack_group(q_ref[rows, sl2])
            o = _attend(qs, [(kb, vb, band), (kc_ref[:, sl], vc_ref[:, sl], None)], _sink_tile(sink_ref, g, WA_Q))
            o_ref[rows, sl2] = _unstack_group(o, WA_Q, lo).astype(BF16)


def _wa(sink, qw, kw, vw, n_batch, seq, ctx_len):
    n_blk = seq // (WA_Q * WA_QB)
    ctx_blk0 = (n_batch * seq) // ctx_len
    gw = 2 * LANES
    return pl.pallas_call(
        functools.partial(_wa_kernel, seq=seq),
        grid=(n_batch, n_blk),
        in_specs=[
            pl.BlockSpec(memory_space=pltpu.SMEM),
            pl.BlockSpec((WA_Q * WA_QB, BW), lambda b, n: (b * n_blk + n, 0)),
            pl.BlockSpec((seq, gw), lambda b, n: (b, 0)),
            pl.BlockSpec((seq, gw), lambda b, n: (b, 0)),
            pl.BlockSpec((ctx_len, gw), lambda b, n: (ctx_blk0 + b, 0)),
            pl.BlockSpec((ctx_len, gw), lambda b, n: (ctx_blk0 + b, 0)),
        ],
        out_specs=pl.BlockSpec((WA_Q * WA_QB, BW), lambda b, n: (b * n_blk + n, 0)),
        out_shape=jax.ShapeDtypeStruct((n_batch * seq, BW), BF16),
        compiler_params=_cparams(("arbitrary", "arbitrary")),
        name="wa",
    )(sink, qw, kw, vw, kw, vw)


FFT_BLK = 16


def _swap_major(x):
    return jnp.swapaxes(x, 0, 1)


def _fft1_kernel(f_ref, x_ref, ar_ref, ai_ref, a_scr):
    xt = _swap_major(x_ref[...].astype(F32)).astype(BF16)
    f = f_ref[...]
    for i in range(FFT_BLK):
        a_scr[i] = _dot(f, xt[i])
    at = _swap_major(a_scr[...])
    ar_ref[...] = at[:GRID_W].astype(BF16)
    ai_ref[...] = at[GRID_W:].astype(BF16)


def _fft2_kernel(m_ref, ar_ref, ai_ref, zr_ref, zi_ref, z_scr):
    for t in range(FFT_BLK):
        a = jnp.concatenate([ar_ref[t], ai_ref[t]], axis=0)
        z_scr[t] = _dot(m_ref[t], a)
    zt = _swap_major(z_scr[...])
    zr_ref[...] = zt[:GRID_W].astype(BF16)
    zi_ref[...] = zt[GRID_W:].astype(BF16)


def _fft_tables():
    n = GRID_W
    k = np.arange(n)
    ang1 = 2.0 * np.pi * ((k[:, None] * k[None, :]) % n) / n
    f1 = np.concatenate([np.cos(ang1), -np.sin(ang1)], axis=0) / 8.0
    ka = k[:, None, None]
    kb = k[None, :, None]
    n1 = k[None, None, :]
    ang2 = 2.0 * np.pi * ((n1 * (ka + n * kb)) % (n * n)) / (n * n)
    mr, mi = np.cos(ang2), -np.sin(ang2)
    m2 = np.concatenate([np.concatenate([mr, -mi], axis=2), np.concatenate([mi, mr], axis=2)], axis=1) / 8.0
    c = np.arange(FN_GROUP_DIM)
    angc = 2.0 * np.pi * ((c[:, None] * c[None, :]) % FN_GROUP_DIM) / FN_GROUP_DIM
    eye = np.eye(FN_GROUPS)
    cbd = np.kron(eye, np.cos(angc)) / 8.0
    sbd = np.kron(eye, np.sin(angc)) / 8.0
    return f1, m2, cbd, sbd


def _ctx_dft_table(ctx_len):
    k = np.arange(ctx_len)
    ang = 2.0 * np.pi * ((k[:, None] * k[None, :]) % ctx_len) / ctx_len
    return np.concatenate([np.cos(ang), -np.sin(ang)], axis=0) / np.sqrt(ctx_len)


def _fft(fu, f1, m2, n_batch, seq):
    n = GRID_W
    n_blk = n // FFT_BLK
    x3 = fu.reshape(-1, n, BW)
    col_spec = pl.BlockSpec((n, FFT_BLK, BW), lambda b, j: (b, j, 0))
    row_spec = pl.BlockSpec((FFT_BLK, n, BW), lambda b, j: (b * n_blk + j, 0, 0))
    shape3 = jax.ShapeDtypeStruct((n_batch * n, n, BW), BF16)
    scratch = [pltpu.VMEM((FFT_BLK, 2 * n, BW), F32)]
    ar, ai = pl.pallas_call(
        _fft1_kernel,
        grid=(n_batch, n_blk),
        in_specs=[pl.BlockSpec((2 * n, n), lambda b, j: (0, 0)), col_spec],
        out_specs=[col_spec, col_spec],
        out_shape=[shape3, shape3],
        scratch_shapes=scratch,
        compiler_params=_cparams(("arbitrary", "arbitrary")),
        name="fft1",
    )(f1, x3)
    zr, zi = pl.pallas_call(
        _fft2_kernel,
        grid=(n_batch, n_blk),
        in_specs=[pl.BlockSpec((FFT_BLK, 2 * n, 2 * n), lambda b, j: (j, 0, 0)), row_spec, row_spec],
        out_specs=[col_spec, col_spec],
        out_shape=[shape3, shape3],
        scratch_shapes=scratch,
        compiler_params=_cparams(("arbitrary", "arbitrary")),
        name="fft2",
    )(m2, ar, ai)
    return zr.reshape(n_batch * seq, BW), zi.reshape(n_batch * seq, BW)


def _ctx_kernel(sink_ref, qn_ref, kn_ref, vn_ref, fu_ref, qw_ref, kw_ref, vw_ref, dft_ref,
                a_ref, w_ref, zr_ref, zi_ref, *, ctx_len):
    for p in range(NA_HEADS // 2):
        sl = slice(p * LANES, (p + 1) * LANES)
        qs, lo = _stack_heads(qn_ref[:, sl], ctx_len)
        o = _attend(qs, [(kn_ref[:, sl], vn_ref[:, sl], None)])
        a_ref[:, sl] = jnp.where(lo, o[:ctx_len], o[ctx_len:]).astype(BF16)
    for g in range(WA_KV_HEADS):
        sl = slice(g * LANES, (g + 1) * LANES)
        sl2 = slice(g * 2 * LANES, (g + 1) * 2 * LANES)
        qs, lo = _stack_group(qw_ref[:, sl2])
        o = _attend(qs, [(kw_ref[:, sl], vw_ref[:, sl], None)], _sink_tile(sink_ref, g, ctx_len))
        w_ref[:, sl2] = _unstack_group(o, ctx_len, lo).astype(BF16)
    z = _dot(dft_ref[...], fu_ref[...])
    zr_ref[...] = z[:ctx_len].astype(BF16)
    zi_ref[...] = z[ctx_len:].astype(BF16)


def _ctx_mix(sink, qn, kn, vn, fu, qw, kw, vw, dft_c, n_batch, seq, ctx_len):
    blk0 = (n_batch * seq) // ctx_len

    def in_spec(w):
        return pl.BlockSpec((ctx_len, w), lambda b: (blk0 + b, 0))

    out_spec = pl.BlockSpec((ctx_len, BW), lambda b: (b, 0))
    return pl.pallas_call(
        functools.partial(_ctx_kernel, ctx_len=ctx_len),
        grid=(n_batch,),
        in_specs=[pl.BlockSpec(memory_space=pltpu.SMEM),
                  in_spec(BW), in_spec(BW), in_spec(BW), in_spec(BW), in_spec(BW),
                  in_spec(2 * LANES), in_spec(2 * LANES),
                  pl.BlockSpec(dft_c.shape, lambda b: (0, 0))],
        out_specs=[out_spec] * 4,
        out_shape=[jax.ShapeDtypeStruct((n_batch * ctx_len, BW), BF16)] * 4,
        compiler_params=_cparams(("arbitrary",)),
        name="ctx_mix",
    )(sink, qn, kn, vn, fu, qw, kw, vw, dft_c)


def _merge_kernel(x_ref, g_ref, sh_ref, sc_ref, gt_ref, a_ref, zr_ref, zi_ref, w_ref, *rest, n_lat_tiles):
    if n_lat_tiles is None:
        wg_ref, wbr_ref, cbd_ref, sbd_ref, wout_ref, o_ref = rest
        a, zr, zi, w = a_ref[...], zr_ref[...], zi_ref[...], w_ref[...]
    else:
        ac_ref, zrc_ref, zic_ref, wc_ref, wg_ref, wbr_ref, cbd_ref, sbd_ref, wout_ref, o_ref = rest
        is_ctx = pl.program_id(0) >= n_lat_tiles
        a = jnp.where(is_ctx, ac_ref[...], a_ref[...])
        zr = jnp.where(is_ctx, zrc_ref[...], zr_ref[...])
        zi = jnp.where(is_ctx, zic_ref[...], zi_ref[...])
        w = jnp.where(is_ctx, wc_ref[...], w_ref[...])
    x = x_ref[...]
    u = _norm_mod(x, g_ref[...], sh_ref[...], sc_ref[...]).astype(BF16)
    f = (_dot(zr, cbd_ref[...]) + _dot(zi, sbd_ref[...])).astype(BF16)
    acc = None
    for i, br in enumerate((a, f, w)):
        gate = _sigmoid(_dot(u, wg_ref[:, PROJ_W + i * D:PROJ_W + (i + 1) * D]))
        term = gate * _dot(br, wbr_ref[i])
        acc = term if acc is None else acc + term
    o_ref[...] = x + gt_ref[...] * _dot(acc.astype(BF16), wout_ref[...])


def _merge(h, n_tiles, g, mod3, layer, branches, ctx_branches, weights, tile_info):
    n_lat_tiles = tile_info[0]
    tile_spec = pl.BlockSpec((TM, D), lambda i: (i, 0))
    vec_spec = pl.BlockSpec((1, D), lambda i: (0, 0))
    has_ctx = ctx_branches is not None
    lat_spec = pl.BlockSpec((TM, BW), lambda i: (jnp.minimum(i, n_lat_tiles - 1), 0))
    ctx_spec = pl.BlockSpec((TM, BW), lambda i: (jnp.maximum(i - n_lat_tiles, 0), 0))
    in_specs = [tile_spec, vec_spec,
                _mod_spec(layer, 3, *tile_info), _mod_spec(layer, 4, *tile_info), _mod_spec(layer, 5, *tile_info)]
    in_specs += [lat_spec] * 4
    args = [h, g.reshape(1, D), mod3, mod3, mod3, *branches]
    if has_ctx:
        in_specs += [ctx_spec] * 4
        args += list(ctx_branches)
    w_in, w_br, cbd, sbd, w_out = weights
    in_specs += [_layer_spec(w_in, layer), _layer_spec(w_br, layer), _const_spec(cbd.shape), _const_spec(sbd.shape),
                 _layer_spec(w_out, layer)]
    args += list(weights)
    return pl.pallas_call(
        functools.partial(_merge_kernel, n_lat_tiles=n_lat_tiles if has_ctx else None),
        grid=(n_tiles,),
        in_specs=in_specs,
        out_specs=tile_spec,
        out_shape=jax.ShapeDtypeStruct((n_tiles * TM, D), F32),
        compiler_params=_cparams(("arbitrary",)),
        name="merge_ctx" if has_ctx else "merge",
    )(*args)


def _rope_tables(seq):
    t = np.arange(seq)
    row = (t // GRID_W).astype(np.float64)
    col = (t % GRID_W).astype(np.float64)
    n_freq = HD // 4
    inv = ROPE_BASE ** (-np.arange(n_freq, dtype=np.float64) / n_freq)
    ang = np.concatenate([row[:, None] * inv, col[:, None] * inv], axis=-1)
    cos, sin = np.cos(ang), np.sin(ang)
    cos_h = np.concatenate([cos, cos], axis=1)
    sin_h = np.concatenate([-sin, sin], axis=1)
    cos2 = np.concatenate([np.tile(cos_h, (1, 2)), np.ones((TM, LANES), np.float32)], axis=0)
    sin2 = np.concatenate([np.tile(sin_h, (1, 2)), np.zeros((TM, LANES), np.float32)], axis=0)
    return jnp.asarray(cos2, F32), jnp.asarray(sin2, F32)


def kernel(x, c, ctx, c_ctx, w_ada, b_ada, g_ffn1, ffn1_w13, ffn1_w2, g_mix, w_in, na_bias, wa_sink,
           w_br, w_out, g_ffn2, ffn2_w13, ffn2_w2, g_final):
    n_batch, seq, _ = x.shape
    ctx_len = ctx.shape[1]
    depth = w_ada.shape[0]
    rows = seq // GRID_W
    n_lat = n_batch * seq
    n_lat_tiles = n_lat // TM
    n_all_tiles = (n_lat + n_batch * ctx_len) // TM
    mod_rows = 8
    tile_info = (n_lat_tiles, seq // TM, n_batch, mod_rows)

    cc = jnp.concatenate([c, c_ctx[None], jnp.zeros((mod_rows - n_batch - 1, D), F32)], axis=0)
    mod3 = _ada(cc, w_ada, b_ada).reshape(depth * mod_rows, 1, N_MOD * D)

    cos_t, sin_t = _rope_tables(seq)
    f1, m2, cbd, sbd = (jnp.asarray(t, F32).astype(BF16) for t in _fft_tables())
    dft_c = jnp.asarray(_ctx_dft_table(ctx_len), F32).astype(BF16)

    w13a, w2a = ffn1_w13.astype(BF16), ffn1_w2.astype(BF16)
    w13b, w2b = ffn2_w13.astype(BF16), ffn2_w2.astype(BF16)
    w_in_b = w_in.astype(BF16)
    merge_w = (w_in_b, w_br.astype(BF16), cbd, sbd, w_out.astype(BF16))

    h = x.reshape(n_lat, D)
    h_ctx = ctx.reshape(n_batch * ctx_len, D)
    for l in range(depth):
        last = l == depth - 1
        bias_tab = _na_bias_table(na_bias[l], rows)

        h = _ffn(h, n_all_tiles, g_ffn1[l], mod3, l, 0, w13a, w2a, tile_info, h_ctx=h_ctx if l == 0 else None)
        qn, kn, vn, fu, qw, kw, vw = _proj(h, n_all_tiles, g_mix[l], mod3, l, w_in_b, cos_t, sin_t, tile_info)
        a = _na(qn, kn, vn, bias_tab, n_batch, seq, ctx_len)
        w = _wa(wa_sink[l], qw, kw, vw, n_batch, seq, ctx_len)
        zr, zi = _fft(fu, f1, m2, n_batch, seq)
        if last:
            h = _merge(h, n_lat_tiles, g_mix[l], mod3, l, (a, zr, zi, w), None, merge_w, tile_info)
            h = _ffn(h, n_lat_tiles, g_ffn2[l], mod3, l, 6, w13b, w2b, tile_info, g_final=g_final)
        else:
            ctx_br = _ctx_mix(wa_sink[l], qn, kn, vn, fu, qw, kw, vw, dft_c, n_batch, seq, ctx_len)
            ac, wc, zrc, zic = ctx_br
            h = _merge(h, n_all_tiles, g_mix[l], mod3, l, (a, zr, zi, w), (ac, zrc, zic, wc), merge_w, tile_info)
            h = _ffn(h, n_all_tiles, g_ffn2[l], mod3, l, 6, w13b, w2b, tile_info)
    return h.reshape(n_batch, seq, D)
```

```python
import functools

import numpy as np
import jax
import jax.numpy as jnp
from jax import lax
from jax.experimental import pallas as pl
from jax.experimental.pallas import tpu as pltpu

D = 1024
GRID_W = 64
HD = 64
NA_HEADS = 8
NA_WIN_R = 8
NA_WIN_C = 16
FN_GROUPS = 8
FN_GROUP_DIM = 64
WA_HEADS = 8
WA_KV_HEADS = 2
WA_WINDOW = 128
D_FF = 2816
N_MOD = 9
ROPE_BASE = 10000.0
EPS = 1e-6
NEG_INF = -1e30
LOG2E = 1.4426950408889634
BW = 512

LANES = 128
MXU_DIM = 256
TM = 512
TF = MXU_DIM
VMEM_LIMIT = 56 * 1024 * 1024

NA_QR = 4
NA_KR = 12
NA_Q = NA_QR * GRID_W
NA_K = NA_KR * GRID_W
WA_Q = 128
WA_K = 3 * WA_Q
WA_QB = 4

F32 = jnp.float32
BF16 = jnp.bfloat16


def _cparams(sem):
    return pltpu.CompilerParams(dimension_semantics=sem, vmem_limit_bytes=VMEM_LIMIT)


def _const_spec(shape):
    nd = len(shape)
    return pl.BlockSpec(shape, lambda *_: (0,) * nd, pipeline_mode=pl.Buffered(1))


def _layer_spec(stacked, layer):
    nd = stacked.ndim
    return pl.BlockSpec((None,) + stacked.shape[1:], lambda *_: (layer,) + (0,) * (nd - 1),
                        pipeline_mode=pl.Buffered(1))


def _sigmoid(x):
    return 1.0 / (1.0 + jnp.exp(-x))


def _norm_mod(x, g, shift, scale):
    y = x * lax.rsqrt(jnp.mean(x * x, axis=-1, keepdims=True) + EPS) * g
    return y * (1.0 + scale) + shift


def _dot(a, b):
    return jnp.dot(a, b, preferred_element_type=F32)


def _dot_nt(a, b):
    return lax.dot_general(a, b, (((1,), (1,)), ((), ())), preferred_element_type=F32)


def _ada_kernel(c_ref, w_ref, b_ref, o_ref):
    x = c_ref[...]
    sx = (x * _sigmoid(x)).astype(BF16)
    o_ref[...] = _dot(sx, w_ref[...].astype(BF16)) + b_ref[...]


def _ada(cc, w_ada, b_ada):
    depth = w_ada.shape[0]
    n = w_ada.shape[2]
    tn = 1536
    rows = cc.shape[0]
    return pl.pallas_call(
        _ada_kernel,
        grid=(depth, n // tn),
        in_specs=[
            pl.BlockSpec((rows, D), lambda l, j: (0, 0)),
            pl.BlockSpec((None, D, tn), lambda l, j: (l, 0, j)),
            pl.BlockSpec((None, 1, tn), lambda l, j: (l, 0, j)),
        ],
        out_specs=pl.BlockSpec((None, rows, tn), lambda l, j: (l, 0, j)),
        out_shape=jax.ShapeDtypeStruct((depth, rows, n), F32),
        compiler_params=_cparams(("arbitrary", "arbitrary")),
        name="ada",
    )(cc, w_ada, b_ada.reshape(depth, 1, n))


def _mod_spec(layer, k, n_lat_tiles, tiles_per_batch, n_batch, mod_rows):
    def idx(i):
        row = jnp.where(i < n_lat_tiles, i // tiles_per_batch, n_batch)
        return (layer * mod_rows + row, 0, k)

    return pl.BlockSpec((None, 1, D), idx)


def _ffn_kernel(*refs, final, split_at):
    refs = list(refs)
    x_ref = refs.pop(0)
    if split_at is not None:
        xc_ref = refs.pop(0)
    g_ref, sh_ref, sc_ref, gt_ref, w13_ref, w2_ref = refs[:6]
    if final:
        gf_ref, o_ref, hm_ref = refs[6:]
    else:
        o_ref, hm_ref = refs[6:]
    x = x_ref[...]
    if split_at is not None:
        x = jnp.where(pl.program_id(0) >= split_at, xc_ref[...], x)
    u = _norm_mod(x, g_ref[...], sh_ref[...], sc_ref[...]).astype(BF16)
    for lo in range(0, D_FF, TF):
        hi = min(lo + TF, D_FF)
        a = _dot(u, w13_ref[:, lo:hi])
        b = _dot(u, w13_ref[:, D_FF + lo:D_FF + hi])
        hm_ref[:, lo:hi] = (a * _sigmoid(a) * b).astype(BF16)
    f = _dot(hm_ref[...], w2_ref[...])
    out = x + 0.5 * gt_ref[...] * f
    if final:
        out = out * lax.rsqrt(jnp.mean(out * out, axis=-1, keepdims=True) + EPS) * gf_ref[...]
    o_ref[...] = out


def _ffn(h, n_tiles, g, mod3, layer, mod_k, w13, w2, tile_info, g_final=None, h_ctx=None):
    final = g_final is not None
    n_lat_tiles = tile_info[0]
    tile_spec = pl.BlockSpec((TM, D), lambda i: (i, 0))
    vec_spec = pl.BlockSpec((1, D), lambda i: (0, 0))
    if h_ctx is None:
        in_specs, args = [tile_spec], [h]
    else:
        in_specs = [pl.BlockSpec((TM, D), lambda i: (jnp.minimum(i, n_lat_tiles - 1), 0)),
                    pl.BlockSpec((TM, D), lambda i: (jnp.maximum(i - n_lat_tiles, 0), 0))]
        args = [h, h_ctx]
    in_specs += [
        vec_spec,
        _mod_spec(layer, mod_k, *tile_info),
        _mod_spec(layer, mod_k + 1, *tile_info),
        _mod_spec(layer, mod_k + 2, *tile_info),
        _layer_spec(w13, layer),
        _layer_spec(w2, layer),
    ]
    args += [g.reshape(1, D), mod3, mod3, mod3, w13, w2]
    if final:
        in_specs.append(vec_spec)
        args.append(g_final.reshape(1, D))
    return pl.pallas_call(
        functools.partial(_ffn_kernel, final=final, split_at=None if h_ctx is None else n_lat_tiles),
        grid=(n_tiles,),
        in_specs=in_specs,
        out_specs=tile_spec,
        out_shape=jax.ShapeDtypeStruct((n_tiles * TM, D), F32),
        scratch_shapes=[pltpu.VMEM((TM, D_FF), BF16)],
        compiler_params=_cparams(("arbitrary",)),
        name="ffn_final" if final else ("ffn" if h_ctx is None else "ffn_split"),
    )(*args)


PROJ_W = 5 * BW + 2 * LANES


def _rope(t, cos, sin_signed, first_half):
    partner = jnp.where(first_half, pltpu.roll(t, LANES - HD // 2, axis=1), pltpu.roll(t, HD // 2, axis=1))
    return t * cos + partner * sin_signed


def _dup_heads(t, lo):
    sw = pltpu.roll(t, HD, axis=1)
    return jnp.where(lo, t, sw), jnp.where(lo, sw, t)


def _proj_kernel(x_ref, g_ref, sh_ref, sc_ref, w_ref, cos_ref, sin_ref,
                 qn_ref, kn_ref, vn_ref, fu_ref, qw_ref, kw_ref, vw_ref):
    u = _norm_mod(x_ref[...], g_ref[...], sh_ref[...], sc_ref[...]).astype(BF16)
    scale = HD ** -0.5 * LOG2E
    cos = cos_ref[...]
    sin = sin_ref[...]
    lane = lax.broadcasted_iota(jnp.int32, (TM, LANES), 1)
    first_half = (lane & (HD - 1)) < (HD // 2)
    lo = lane < HD
    kv = _dot(u, w_ref[:, 5 * BW:PROJ_W])
    k0, k1 = _dup_heads(_rope(kv[:, :LANES], cos, sin, first_half), lo)
    kw_ref[:, :LANES] = k0.astype(BF16)
    kw_ref[:, LANES:] = k1.astype(BF16)
    v0, v1 = _dup_heads(kv[:, LANES:], lo)
    vw_ref[:, :LANES] = v0.astype(BF16)
    vw_ref[:, LANES:] = v1.astype(BF16)
    wq = _dot(u, w_ref[:, 4 * BW:5 * BW])
    for j in range(BW // LANES):
        sl = slice(j * LANES, (j + 1) * LANES)
        qw_ref[:, sl] = (_rope(wq[:, sl], cos, sin, first_half) * scale).astype(BF16)
    qn_ref[...] = (_dot(u, w_ref[:, 0:BW]) * scale).astype(BF16)
    kn_ref[...] = _dot(u, w_ref[:, BW:2 * BW]).astype(BF16)
    vn_ref[...] = _dot(u, w_ref[:, 2 * BW:3 * BW]).astype(BF16)
    fu_ref[...] = _dot(u, w_ref[:, 3 * BW:4 * BW]).astype(BF16)


def _proj(h, n_tiles, g, mod3, layer, w_in, cos_t, sin_t, tile_info):
    n_lat_tiles, tiles_per_batch, _, _ = tile_info
    tile_spec = pl.BlockSpec((TM, D), lambda i: (i, 0))
    vec_spec = pl.BlockSpec((1, D), lambda i: (0, 0))
    rope_spec = pl.BlockSpec((TM, LANES), lambda i: (jnp.where(i < n_lat_tiles, i % tiles_per_batch, tiles_per_batch), 0))
    rows = n_tiles * TM

    def out_spec(w):
        return pl.BlockSpec((TM, w), lambda i: (i, 0))

    widths = [BW, BW, BW, BW, BW, 2 * LANES, 2 * LANES]
    return pl.pallas_call(
        _proj_kernel,
        grid=(n_tiles,),
        in_specs=[tile_spec, vec_spec,
                  _mod_spec(layer, 3, *tile_info), _mod_spec(layer, 4, *tile_info),
                  _layer_spec(w_in, layer), rope_spec, rope_spec],
        out_specs=[out_spec(w) for w in widths],
        out_shape=[jax.ShapeDtypeStruct((rows, w), BF16) for w in widths],
        compiler_params=_cparams(("arbitrary",)),
        name="proj",
    )(h, g.reshape(1, D), mod3, mod3, w_in, cos_t, sin_t)


def _lane_tiles(s):
    return [s[:, j * LANES:(j + 1) * LANES] for j in range(s.shape[1] // LANES)]


def _attend(qs, key_sets, sink_tile=None):
    scores = []
    for k, _, bias in key_sets:
        s = _dot_nt(qs, k)
        if bias is not None:
            s = s + bias
        scores.append(s)
    tiles = [t for s in scores for t in _lane_tiles(s)]
    if sink_tile is not None:
        tiles.append(sink_tile)
    m = functools.reduce(jnp.maximum, tiles).max(axis=-1, keepdims=True)
    psum = None
    acc = None
    for s, (_, v, _) in zip(scores, key_sets):
        p = jnp.exp2(s - m)
        for t in _lane_tiles(p):
            psum = t if psum is None else psum + t
        o = _dot(p.astype(BF16), v)
        acc = o if acc is None else acc + o
    if sink_tile is not None:
        lane = lax.broadcasted_iota(jnp.int32, sink_tile.shape, 1)
        psum = psum + jnp.where(lane == 0, jnp.exp2(sink_tile - m), 0.0)
    return acc / psum.sum(axis=-1, keepdims=True)


def _stack_heads(q, rows):
    lane = lax.broadcasted_iota(jnp.int32, (rows, LANES), 1)
    lo = lane < HD
    zero = jnp.zeros_like(q)
    return jnp.concatenate([jnp.where(lo, q, zero), jnp.where(lo, zero, q)], axis=0), lo


def _na_kernel(q_ref, k_ref, v_ref, kc_ref, vc_ref, bias_ref, o_ref, *, rows):
    j = pl.program_id(1)
    krow = jnp.clip(NA_QR * j - NA_WIN_R // 2, 0, rows - NA_KR)
    kstart = pl.multiple_of(krow * GRID_W, GRID_W)
    for p in range(NA_HEADS // 2):
        sl = slice(p * LANES, (p + 1) * LANES)
        kb = k_ref[pl.ds(kstart, NA_K), sl]
        vb = v_ref[pl.ds(kstart, NA_K), sl]
        qs, lo = _stack_heads(q_ref[:, sl], NA_Q)
        o = _attend(qs, [(kb, vb, bias_ref[p]), (kc_ref[:, sl], vc_ref[:, sl], None)])
        o_ref[:, sl] = jnp.where(lo, o[:NA_Q], o[NA_Q:]).astype(BF16)


def _na(qn, kn, vn, bias_tab, n_batch, seq, ctx_len):
    rows = seq // GRID_W
    n_blk = rows // NA_QR
    n_pair = NA_HEADS // 2
    ctx_blk0 = (n_batch * seq) // ctx_len

    def pat(j):
        return jnp.where(j == 0, 0, jnp.where(j == n_blk - 1, 2, 1))

    return pl.pallas_call(
        functools.partial(_na_kernel, rows=rows),
        grid=(n_batch, n_blk),
        in_specs=[
            pl.BlockSpec((NA_Q, BW), lambda b, j: (b * n_blk + j, 0)),
            pl.BlockSpec((seq, BW), lambda b, j: (b, 0)),
            pl.BlockSpec((seq, BW), lambda b, j: (b, 0)),
            pl.BlockSpec((ctx_len, BW), lambda b, j: (ctx_blk0 + b, 0)),
            pl.BlockSpec((ctx_len, BW), lambda b, j: (ctx_blk0 + b, 0)),
            pl.BlockSpec((None, n_pair, 2 * NA_Q, NA_K), lambda b, j: (pat(j), 0, 0, 0)),
        ],
        out_specs=pl.BlockSpec((NA_Q, BW), lambda b, j: (b * n_blk + j, 0)),
        out_shape=jax.ShapeDtypeStruct((n_batch * seq, BW), BF16),
        compiler_params=_cparams(("arbitrary", "arbitrary")),
        name="na",
    )(qn, kn, vn, kn, vn, bias_tab)


def _na_bias_table(bias, rows):
    n_blk = rows // NA_QR
    h = bias.shape[0]
    n_dc = 2 * NA_WIN_C - 1
    qc = np.arange(GRID_W)[:, None]
    kc = np.arange(GRID_W)[None, :]
    ws = np.clip(qc - NA_WIN_C // 2, 0, GRID_W - NA_WIN_C)
    col_ok = (kc >= ws) & (kc < ws + NA_WIN_C)
    dc = np.clip(kc - qc, -(NA_WIN_C - 1), NA_WIN_C - 1) + NA_WIN_C - 1
    onehot = (dc[None] == np.arange(n_dc)[:, None, None]).astype(np.float32)
    toep = jnp.einsum('hrd,dqk->hrqk', bias.astype(F32), jnp.asarray(onehot), precision=lax.Precision.HIGHEST)
    toep = jnp.where(col_ok[None, None], toep * LOG2E, NEG_INF)
    n_dr = toep.shape[1]
    toep = jnp.concatenate([toep, jnp.full((h, 1, GRID_W, GRID_W), NEG_INF, F32)], axis=1)
    toep2 = jnp.concatenate([toep, toep], axis=-1)
    plan = []
    for blk in (0, 1, n_blk - 1):
        k0 = min(max(NA_QR * blk - NA_WIN_R // 2, 0), rows - NA_KR)
        per_row = []
        for i in range(NA_QR):
            r = NA_QR * blk + i
            rs = min(max(r - NA_WIN_R // 2, 0), rows - NA_WIN_R)
            slabs = [k0 + t - r + NA_WIN_R - 1 if rs <= k0 + t < rs + NA_WIN_R else n_dr for t in range(NA_KR)]
            per_row.append([(slabs[2 * j], slabs[2 * j + 1]) for j in range(NA_KR // 2)])
        plan.append(per_row)
    return pl.pallas_call(
        functools.partial(_bias_expand_kernel, plan=plan),
        out_shape=jax.ShapeDtypeStruct((len(plan), h // 2, 2 * NA_Q, NA_K), F32),
        compiler_params=pltpu.CompilerParams(vmem_limit_bytes=VMEM_LIMIT),
        name="na_bias_expand",
    )(toep2)


def _bias_expand_kernel(t_ref, o_ref, *, plan):
    lo = lax.broadcasted_iota(jnp.int32, (GRID_W, LANES), 1) < GRID_W
    for pat, per_row in enumerate(plan):
        for hd in range(t_ref.shape[0]):
            for i, pairs in enumerate(per_row):
                r0 = (hd % 2) * NA_Q + i * GRID_W
                for j, (da, db) in enumerate(pairs):
                    o_ref[pat, hd // 2, r0:r0 + GRID_W, j * LANES:(j + 1) * LANES] = (
                        jnp.where(lo, t_ref[hd, da], t_ref[hd, db]))


def _stack_group(q):
    rows = q.shape[0]
    a, lo = _stack_heads(q[:, :LANES], rows)
    b, _ = _stack_heads(q[:, LANES:], rows)
    return jnp.concatenate([a, b], axis=0), lo


def _unstack_group(o, rows, lo):
    oa = jnp.where(lo, o[0:rows], o[rows:2 * rows])
    ob = jnp.where(lo, o[2 * rows:3 * rows], o[3 * rows:4 * rows])
    return jnp.concatenate([oa, ob], axis=1)


def _sink_tile(sink_ref, g, rows):
    gq = WA_HEADS // WA_KV_HEADS
    return jnp.concatenate([jnp.full((rows, LANES), sink_ref[g * gq + h] * LOG2E, F32) for h in range(gq)], axis=0)


def _wa_kernel(sink_ref, q_ref, k_ref, v_ref, kc_ref, vc_ref, o_ref, *, seq):
    for t in range(WA_QB):
        n = pl.program_id(1) * WA_QB + t
        rows = slice(t * WA_Q, (t + 1) * WA_Q)
        kstart = pl.multiple_of(jnp.clip(n * WA_Q - WA_Q, 0, seq - WA_K), WA_Q)
        qpos = n * WA_Q + lax.broadcasted_iota(jnp.int32, (WA_Q, WA_K), 0)
        kpos = kstart + lax.broadcasted_iota(jnp.int32, (WA_Q, WA_K), 1)
        band = jnp.where(jnp.abs(kpos - qpos) <= WA_WINDOW, 0.0, NEG_INF).astype(F32)
        band = jnp.concatenate([band] * (WA_HEADS // WA_KV_HEADS), axis=0)
        for g in range(WA_KV_HEADS):
            sl = slice(g * LANES, (g + 1) * LANES)
            sl2 = slice(g * 2 * LANES, (g + 1) * 2 * LANES)
            kb = k_ref[pl.ds(kstart, WA_K), sl]
            vb = v_ref[pl.ds(kstart, WA_K), sl]
            qs, lo = _stack_group(q_ref[rows, sl2])
            o = _attend(qs, [(kb, vb, band), (kc_ref[:, sl], vc_ref[:, sl], None)], _sink_tile(sink_ref, g, WA_Q))
            o_ref[rows, sl2] = _unstack_group(o, WA_Q, lo).astype(BF16)


def _wa(sink, qw, kw, vw, n_batch, seq, ctx_len):
    n_blk = seq // (WA_Q * WA_QB)
    ctx_blk0 = (n_batch * seq) // ctx_len
    gw = 2 * LANES
    return pl.pallas_call(
        functools.partial(_wa_kernel, seq=seq),
        grid=(n_batch, n_blk),
        in_specs=[
            pl.BlockSpec(memory_space=pltpu.SMEM),
            pl.BlockSpec((WA_Q * WA_QB, BW), lambda b, n: (b * n_blk + n, 0)),
            pl.BlockSpec((seq, gw), lambda b, n: (b, 0)),
            pl.BlockSpec((seq, gw), lambda b, n: (b, 0)),
            pl.BlockSpec((ctx_len, gw), lambda b, n: (ctx_blk0 + b, 0)),
            pl.BlockSpec((ctx_len, gw), lambda b, n: (ctx_blk0 + b, 0)),
        ],
        out_specs=pl.BlockSpec((WA_Q * WA_QB, BW), lambda b, n: (b * n_blk + n, 0)),
        out_shape=jax.ShapeDtypeStruct((n_batch * seq, BW), BF16),
        compiler_params=_cparams(("arbitrary", "arbitrary")),
        name="wa",
    )(sink, qw, kw, vw, kw, vw)


FFT_BLK = 16


def _swap_major(x):
    return jnp.swapaxes(x, 0, 1)


def _fft1_kernel(f_ref, x_ref, ar_ref, ai_ref, a_scr):
    xt = _swap_major(x_ref[...].astype(F32)).astype(BF16)
    f = f_ref[...]
    for i in range(FFT_BLK):
        a_scr[i] = _dot(f, xt[i])
    at = _swap_major(a_scr[...])
    ar_ref[...] = at[:GRID_W].astype(BF16)
    ai_ref[...] = at[GRID_W:].astype(BF16)


def _fft2_kernel(m_ref, ar_ref, ai_ref, zr_ref, zi_ref, z_scr):
    for t in range(FFT_BLK):
        a = jnp.concatenate([ar_ref[t], ai_ref[t]], axis=0)
        z_scr[t] = _dot(m_ref[t], a)
    zt = _swap_major(z_scr[...])
    zr_ref[...] = zt[:GRID_W].astype(BF16)
    zi_ref[...] = zt[GRID_W:].astype(BF16)


def _fft_tables():
    n = GRID_W
    k = np.arange(n)
    ang1 = 2.0 * np.pi * ((k[:, None] * k[None, :]) % n) / n
    f1 = np.concatenate([np.cos(ang1), -np.sin(ang1)], axis=0) / 8.0
    ka = k[:, None, None]
    kb = k[None, :, None]
    n1 = k[None, None, :]
    ang2 = 2.0 * np.pi * ((n1 * (ka + n * kb)) % (n * n)) / (n * n)
    mr, mi = np.cos(ang2), -np.sin(ang2)
    m2 = np.concatenate([np.concatenate([mr, -mi], axis=2), np.concatenate([mi, mr], axis=2)], axis=1) / 8.0
    c = np.arange(FN_GROUP_DIM)
    angc = 2.0 * np.pi * ((c[:, None] * c[None, :]) % FN_GROUP_DIM) / FN_GROUP_DIM
    eye = np.eye(FN_GROUPS)
    cbd = np.kron(eye, np.cos(angc)) / 8.0
    sbd = np.kron(eye, np.sin(angc)) / 8.0
    return f1, m2, cbd, sbd


def _ctx_dft_table(ctx_len):
    k = np.arange(ctx_len)
    ang = 2.0 * np.pi * ((k[:, None] * k[None, :]) % ctx_len) / ctx_len
    return np.concatenate([np.cos(ang), -np.sin(ang)], axis=0) / np.sqrt(ctx_len)


def _fft(fu, f1, m2, n_batch, seq):
    n = GRID_W
    n_blk = n // FFT_BLK
    x3 = fu.reshape(-1, n, BW)
    col_spec = pl.BlockSpec((n, FFT_BLK, BW), lambda b, j: (b, j, 0))
    row_spec = pl.BlockSpec((FFT_BLK, n, BW), lambda b, j: (b * n_blk + j, 0, 0))
    shape3 = jax.ShapeDtypeStruct((n_batch * n, n, BW), BF16)
    scratch = [pltpu.VMEM((FFT_BLK, 2 * n, BW), F32)]
    ar, ai = pl.pallas_call(
        _fft1_kernel,
        grid=(n_batch, n_blk),
        in_specs=[pl.BlockSpec((2 * n, n), lambda b, j: (0, 0)), col_spec],
        out_specs=[col_spec, col_spec],
        out_shape=[shape3, shape3],
        scratch_shapes=scratch,
        compiler_params=_cparams(("arbitrary", "arbitrary")),
        name="fft1",
    )(f1, x3)
    zr, zi = pl.pallas_call(
        _fft2_kernel,
        grid=(n_batch, n_blk),
        in_specs=[pl.BlockSpec((FFT_BLK, 2 * n, 2 * n), lambda b, j: (j, 0, 0)), row_spec, row_spec],
        out_specs=[col_spec, col_spec],
        out_shape=[shape3, shape3],
        scratch_shapes=scratch,
        compiler_params=_cparams(("arbitrary", "arbitrary")),
        name="fft2",
    )(m2, ar, ai)
    return zr.reshape(n_batch * seq, BW), zi.reshape(n_batch * seq, BW)


def _ctx_kernel(sink_ref, qn_ref, kn_ref, vn_ref, fu_ref, qw_ref, kw_ref, vw_ref, dft_ref,
                a_ref, w_ref, zr_ref, zi_ref, *, ctx_len):
    for p in range(NA_HEADS // 2):
        sl = slice(p * LANES, (p + 1) * LANES)
        qs, lo = _stack_heads(qn_ref[:, sl], ctx_len)
        o = _attend(qs, [(kn_ref[:, sl], vn_ref[:, sl], None)])
        a_ref[:, sl] = jnp.where(lo, o[:ctx_len], o[ctx_len:]).astype(BF16)
    for g in range(WA_KV_HEADS):
        sl = slice(g * LANES, (g + 1) * LANES)
        sl2 = slice(g * 2 * LANES, (g + 1) * 2 * LANES)
        qs, lo = _stack_group(qw_ref[:, sl2])
        o = _attend(qs, [(kw_ref[:, sl], vw_ref[:, sl], None)], _sink_tile(sink_ref, g, ctx_len))
        w_ref[:, sl2] = _unstack_group(o, ctx_len, lo).astype(BF16)
    z = _dot(dft_ref[...], fu_ref[...])
    zr_ref[...] = z[:ctx_len].astype(BF16)
    zi_ref[...] = z[ctx_len:].astype(BF16)


def _ctx_mix(sink, qn, kn, vn, fu, qw, kw, vw, dft_c, n_batch, seq, ctx_len):
    blk0 = (n_batch * seq) // ctx_len

    def in_spec(w):
        return pl.BlockSpec((ctx_len, w), lambda b: (blk0 + b, 0))

    out_spec = pl.BlockSpec((ctx_len, BW), lambda b: (b, 0))
    return pl.pallas_call(
        functools.partial(_ctx_kernel, ctx_len=ctx_len),
        grid=(n_batch,),
        in_specs=[pl.BlockSpec(memory_space=pltpu.SMEM),
                  in_spec(BW), in_spec(BW), in_spec(BW), in_spec(BW), in_spec(BW),
                  in_spec(2 * LANES), in_spec(2 * LANES),
                  pl.BlockSpec(dft_c.shape, lambda b: (0, 0))],
        out_specs=[out_spec] * 4,
        out_shape=[jax.ShapeDtypeStruct((n_batch * ctx_len, BW), BF16)] * 4,
        compiler_params=_cparams(("arbitrary",)),
        name="ctx_mix",
    )(sink, qn, kn, vn, fu, qw, kw, vw, dft_c)


def _merge_kernel(x_ref, g_ref, sh_ref, sc_ref, gt_ref, a_ref, zr_ref, zi_ref, w_ref, *rest, n_lat_tiles):
    if n_lat_tiles is None:
        wg_ref, wbr_ref, cbd_ref, sbd_ref, wout_ref, o_ref = rest
        a, zr, zi, w = a_ref[...], zr_ref[...], zi_ref[...], w_ref[...]
    else:
        ac_ref, zrc_ref, zic_ref, wc_ref, wg_ref, wbr_ref, cbd_ref, sbd_ref, wout_ref, o_ref = rest
        is_ctx = pl.program_id(0) >= n_lat_tiles
        a = jnp.where(is_ctx, ac_ref[...], a_ref[...])
        zr = jnp.where(is_ctx, zrc_ref[...], zr_ref[...])
        zi = jnp.where(is_ctx, zic_ref[...], zi_ref[...])
        w = jnp.where(is_ctx, wc_ref[...], w_ref[...])
    x = x_ref[...]
    u = _norm_mod(x, g_ref[...], sh_ref[...], sc_ref[...]).astype(BF16)
    f = (_dot(zr, cbd_ref[...]) + _dot(zi, sbd_ref[...])).astype(BF16)
    acc = None
    for i, br in enumerate((a, f, w)):
        gate = _sigmoid(_dot(u, wg_ref[:, i * D:(i + 1) * D]))
        term = gate * _dot(br, wbr_ref[i])
        acc = term if acc is None else acc + term
    o_ref[...] = x + gt_ref[...] * _dot(acc.astype(BF16), wout_ref[...])


def _merge(h, n_tiles, g, mod3, layer, branches, ctx_branches, weights, tile_info):
    n_lat_tiles = tile_info[0]
    tile_spec = pl.BlockSpec((TM, D), lambda i: (i, 0))
    vec_spec = pl.BlockSpec((1, D), lambda i: (0, 0))
    has_ctx = ctx_branches is not None
    lat_spec = pl.BlockSpec((TM, BW), lambda i: (jnp.minimum(i, n_lat_tiles - 1), 0))
    ctx_spec = pl.BlockSpec((TM, BW), lambda i: (jnp.maximum(i - n_lat_tiles, 0), 0))
    in_specs = [tile_spec, vec_spec,
                _mod_spec(layer, 3, *tile_info), _mod_spec(layer, 4, *tile_info), _mod_spec(layer, 5, *tile_info)]
    in_specs += [lat_spec] * 4
    args = [h, g.reshape(1, D), mod3, mod3, mod3, *branches]
    if has_ctx:
        in_specs += [ctx_spec] * 4
        args += list(ctx_branches)
    w_gate, w_br, cbd, sbd, w_out = weights
    in_specs += [_layer_spec(w_gate, layer), _layer_spec(w_br, layer), _const_spec(cbd.shape), _const_spec(sbd.shape),
                 _layer_spec(w_out, layer)]
    args += list(weights)
    return pl.pallas_call(
        functools.partial(_merge_kernel, n_lat_tiles=n_lat_tiles if has_ctx else None),
        grid=(n_tiles,),
        in_specs=in_specs,
        out_specs=tile_spec,
        out_shape=jax.ShapeDtypeStruct((n_tiles * TM, D), F32),
        compiler_params=_cparams(("arbitrary",)),
        name="merge_ctx" if has_ctx else "merge",
    )(*args)


def _rope_tables(seq):
    t = np.arange(seq)
    row = (t // GRID_W).astype(np.float64)
    col = (t % GRID_W).astype(np.float64)
    n_freq = HD // 4
    inv = ROPE_BASE ** (-np.arange(n_freq, dtype=np.float64) / n_freq)
    ang = np.concatenate([row[:, None] * inv, col[:, None] * inv], axis=-1)
    cos, sin = np.cos(ang), np.sin(ang)
    cos_h = np.concatenate([cos, cos], axis=1)
    sin_h = np.concatenate([-sin, sin], axis=1)
    cos2 = np.concatenate([np.tile(cos_h, (1, 2)), np.ones((TM, LANES), np.float32)], axis=0)
    sin2 = np.concatenate([np.tile(sin_h, (1, 2)), np.zeros((TM, LANES), np.float32)], axis=0)
    return jnp.asarray(cos2, F32), jnp.asarray(sin2, F32)


def kernel(x, c, ctx, c_ctx, w_ada, b_ada, g_ffn1, ffn1_w13, ffn1_w2, g_mix, w_in, na_bias, wa_sink,
           w_br, w_out, g_ffn2, ffn2_w13, ffn2_w2, g_final):
    n_batch, seq, _ = x.shape
    ctx_len = ctx.shape[1]
    depth = w_ada.shape[0]
    rows = seq // GRID_W
    n_lat = n_batch * seq
    n_lat_tiles = n_lat // TM
    n_all_tiles = (n_lat + n_batch * ctx_len) // TM
    mod_rows = 8
    tile_info = (n_lat_tiles, seq // TM, n_batch, mod_rows)

    cc = jnp.concatenate([c, c_ctx[None], jnp.zeros((mod_rows - n_batch - 1, D), F32)], axis=0)
    mod3 = _ada(cc, w_ada, b_ada).reshape(depth * mod_rows, 1, N_MOD * D)

    cos_t, sin_t = _rope_tables(seq)
    f1, m2, cbd, sbd = (jnp.asarray(t, F32).astype(BF16) for t in _fft_tables())
    dft_c = jnp.asarray(_ctx_dft_table(ctx_len), F32).astype(BF16)

    w13a, w2a = ffn1_w13.astype(BF16), ffn1_w2.astype(BF16)
    w13b, w2b = ffn2_w13.astype(BF16), ffn2_w2.astype(BF16)
    w_in_b = w_in[:, :, :PROJ_W].astype(BF16)
    merge_w = (w_in[:, :, PROJ_W:].astype(BF16), w_br.astype(BF16), cbd, sbd, w_out.astype(BF16))

    h = x.reshape(n_lat, D)
    h_ctx = ctx.reshape(n_batch * ctx_len, D)
    for l in range(depth):
        last = l == depth - 1
        bias_tab = _na_bias_table(na_bias[l], rows)

        h = _ffn(h, n_all_tiles, g_ffn1[l], mod3, l, 0, w13a, w2a, tile_info, h_ctx=h_ctx if l == 0 else None)
        qn, kn, vn, fu, qw, kw, vw = _proj(h, n_all_tiles, g_mix[l], mod3, l, w_in_b, cos_t, sin_t, tile_info)
        a = _na(qn, kn, vn, bias_tab, n_batch, seq, ctx_len)
        w = _wa(wa_sink[l], qw, kw, vw, n_batch, seq, ctx_len)
        zr, zi = _fft(fu, f1, m2, n_batch, seq)
        if last:
            h = _merge(h, n_lat_tiles, g_mix[l], mod3, l, (a, zr, zi, w), None, merge_w, tile_info)
            h = _ffn(h, n_lat_tiles, g_ffn2[l], mod3, l, 6, w13b, w2b, tile_info, g_final=g_final)
        else:
            ctx_br = _ctx_mix(wa_sink[l], qn, kn, vn, fu, qw, kw, vw, dft_c, n_batch, seq, ctx_len)
            ac, wc, zrc, zic = ctx_br
            h = _merge(h, n_all_tiles, g_mix[l], mod3, l, (a, zr, zi, w), (ac, zrc, zic, wc), merge_w, tile_info)
            h = _ffn(h, n_all_tiles, g_ffn2[l], mod3, l, 6, w13b, w2b, tile_info)
    return h.reshape(n_batch, seq, D)
```

```python
import functools

import numpy as np
import jax
import jax.numpy as jnp
from jax import lax
from jax.experimental import pallas as pl
from jax.experimental.pallas import tpu as pltpu

D = 1024
GRID_W = 64
HD = 64
NA_HEADS = 8
NA_WIN_R = 8
NA_WIN_C = 16
FN_GROUPS = 8
FN_GROUP_DIM = 64
WA_HEADS = 8
WA_KV_HEADS = 2
WA_WINDOW = 128
D_FF = 2816
N_MOD = 9
ROPE_BASE = 10000.0
EPS = 1e-6
NEG_INF = -1e30
LOG2E = 1.4426950408889634
BW = 512

LANES = 128
MXU_DIM = 256
TM = 512
TF = MXU_DIM
VMEM_LIMIT = 56 * 1024 * 1024

NA_QR = 4
NA_KR = 12
NA_Q = NA_QR * GRID_W
NA_K = NA_KR * GRID_W
WA_Q = 128
WA_K = 3 * WA_Q
WA_QB = 4

F32 = jnp.float32
BF16 = jnp.bfloat16


def _cparams(sem):
    return pltpu.CompilerParams(dimension_semantics=sem, vmem_limit_bytes=VMEM_LIMIT)


def _const_spec(shape):
    nd = len(shape)
    return pl.BlockSpec(shape, lambda *_: (0,) * nd, pipeline_mode=pl.Buffered(1))


def _layer_spec(stacked, layer):
    nd = stacked.ndim
    return pl.BlockSpec((None,) + stacked.shape[1:], lambda *_: (layer,) + (0,) * (nd - 1),
                        pipeline_mode=pl.Buffered(1))


def _sigmoid(x):
    return 1.0 / (1.0 + jnp.exp(-x))


def _norm_mod(x, g, shift, scale):
    y = x * lax.rsqrt(jnp.mean(x * x, axis=-1, keepdims=True) + EPS) * g
    return y * (1.0 + scale) + shift


def _dot(a, b):
    return jnp.dot(a, b, preferred_element_type=F32)


def _dot_nt(a, b):
    return lax.dot_general(a, b, (((1,), (1,)), ((), ())), preferred_element_type=F32)


def _ada_kernel(c_ref, w_ref, b_ref, o_ref):
    x = c_ref[...]
    sx = (x * _sigmoid(x)).astype(BF16)
    o_ref[...] = _dot(sx, w_ref[...].astype(BF16)) + b_ref[...]


def _ada(cc, w_ada, b_ada):
    depth = w_ada.shape[0]
    n = w_ada.shape[2]
    tn = 1536
    rows = cc.shape[0]
    return pl.pallas_call(
        _ada_kernel,
        grid=(depth, n // tn),
        in_specs=[
            pl.BlockSpec((rows, D), lambda l, j: (0, 0)),
            pl.BlockSpec((None, D, tn), lambda l, j: (l, 0, j)),
            pl.BlockSpec((None, 1, tn), lambda l, j: (l, 0, j)),
        ],
        out_specs=pl.BlockSpec((None, rows, tn), lambda l, j: (l, 0, j)),
        out_shape=jax.ShapeDtypeStruct((depth, rows, n), F32),
        compiler_params=_cparams(("arbitrary", "arbitrary")),
        name="ada",
    )(cc, w_ada, b_ada.reshape(depth, 1, n))


def _mod_spec(layer, k, n_lat_tiles, tiles_per_batch, n_batch, mod_rows, tile_of=lambda i: i):
    def idx(i):
        t = tile_of(i)
        row = jnp.where(t < n_lat_tiles, t // tiles_per_batch, n_batch)
        return (layer * mod_rows + row, 0, k)

    return pl.BlockSpec((None, 1, D), idx)


FFN_W13_CH = 2 * TF
FFN_WSTEPS = 2 * D_FF // FFN_W13_CH
FFN_W2_CH = D_FF // FFN_WSTEPS


def _ffn_kernel(*refs, final, split_at):
    refs = list(refs)
    x_ref = refs.pop(0)
    if split_at is not None:
        xc_ref = refs.pop(0)
    g_ref, sh_ref, sc_ref, gt_ref, w13_ref, w2_ref = refs[:6]
    if final:
        gf_ref, o_ref, w13_s, w2_s, hm_ref = refs[6:]
    else:
        o_ref, w13_s, w2_s, hm_ref = refs[6:]
    i = pl.program_id(0)

    @pl.when(i < FFN_WSTEPS)
    def _():
        w13_s[i] = w13_ref[...].astype(BF16)
        w2_s[i] = w2_ref[...].astype(BF16)

    def w13_cols(lo):
        off = lo % FFN_W13_CH
        return w13_s[lo // FFN_W13_CH][:, off:off + TF]

    @pl.when(i >= FFN_WSTEPS)
    def _():
        x = x_ref[...]
        if split_at is not None:
            x = jnp.where(i - FFN_WSTEPS >= split_at, xc_ref[...], x)
        u = _norm_mod(x, g_ref[...], sh_ref[...], sc_ref[...]).astype(BF16)
        for lo in range(0, D_FF, TF):
            a = _dot(u, w13_cols(lo))
            b = _dot(u, w13_cols(D_FF + lo))
            hm_ref[:, lo:lo + TF] = (a * _sigmoid(a) * b).astype(BF16)
        f = _dot(hm_ref[...], w2_s[...].reshape(D_FF, D))
        out = x + 0.5 * gt_ref[...] * f
        if final:
            out = out * lax.rsqrt(jnp.mean(out * out, axis=-1, keepdims=True) + EPS) * gf_ref[...]
        o_ref[...] = out


def _ffn(h, n_tiles, g, mod3, layer, mod_k, w13, w2, tile_info, g_final=None, h_ctx=None):
    final = g_final is not None
    n_lat_tiles = tile_info[0]

    def tile_of(i):
        return jnp.maximum(i - FFN_WSTEPS, 0)

    def wstep_of(i):
        return jnp.minimum(i, FFN_WSTEPS - 1)

    tile_spec = pl.BlockSpec((TM, D), lambda i: (tile_of(i), 0))
    vec_spec = pl.BlockSpec((1, D), lambda i: (0, 0))
    if h_ctx is None:
        in_specs, args = [tile_spec], [h]
    else:
        in_specs = [pl.BlockSpec((TM, D), lambda i: (jnp.minimum(tile_of(i), n_lat_tiles - 1), 0)),
                    pl.BlockSpec((TM, D), lambda i: (jnp.maximum(tile_of(i) - n_lat_tiles, 0), 0))]
        args = [h, h_ctx]
    in_specs += [
        vec_spec,
        _mod_spec(layer, mod_k, *tile_info, tile_of=tile_of),
        _mod_spec(layer, mod_k + 1, *tile_info, tile_of=tile_of),
        _mod_spec(layer, mod_k + 2, *tile_info, tile_of=tile_of),
        pl.BlockSpec((None, D, FFN_W13_CH), lambda i: (layer, 0, wstep_of(i))),
        pl.BlockSpec((None, FFN_W2_CH, D), lambda i: (layer, wstep_of(i), 0)),
    ]
    args += [g.reshape(1, D), mod3, mod3, mod3, w13, w2]
    if final:
        in_specs.append(vec_spec)
        args.append(g_final.reshape(1, D))
    return pl.pallas_call(
        functools.partial(_ffn_kernel, final=final, split_at=None if h_ctx is None else n_lat_tiles),
        grid=(FFN_WSTEPS + n_tiles,),
        in_specs=in_specs,
        out_specs=tile_spec,
        out_shape=jax.ShapeDtypeStruct((n_tiles * TM, D), F32),
        scratch_shapes=[pltpu.VMEM((FFN_WSTEPS, D, FFN_W13_CH), BF16),
                        pltpu.VMEM((FFN_WSTEPS, FFN_W2_CH, D), BF16),
                        pltpu.VMEM((TM, D_FF), BF16)],
        compiler_params=_cparams(("arbitrary",)),
        name="ffn_final" if final else ("ffn" if h_ctx is None else "ffn_split"),
    )(*args)


PROJ_W = 5 * BW + 2 * LANES


def _rope(t, cos, sin_signed, first_half):
    partner = jnp.where(first_half, pltpu.roll(t, LANES - HD // 2, axis=1), pltpu.roll(t, HD // 2, axis=1))
    return t * cos + partner * sin_signed


def _dup_heads(t, lo):
    sw = pltpu.roll(t, HD, axis=1)
    return jnp.where(lo, t, sw), jnp.where(lo, sw, t)


def _proj_kernel(x_ref, g_ref, sh_ref, sc_ref, w_ref, cos_ref, sin_ref,
                 qn_ref, kn_ref, vn_ref, fu_ref, qw_ref, kw_ref, vw_ref):
    u = _norm_mod(x_ref[...], g_ref[...], sh_ref[...], sc_ref[...]).astype(BF16)
    scale = HD ** -0.5 * LOG2E
    cos = cos_ref[...]
    sin = sin_ref[...]
    lane = lax.broadcasted_iota(jnp.int32, (TM, LANES), 1)
    first_half = (lane & (HD - 1)) < (HD // 2)
    lo = lane < HD
    kv = _dot(u, w_ref[:, 5 * BW:PROJ_W])
    k0, k1 = _dup_heads(_rope(kv[:, :LANES], cos, sin, first_half), lo)
    kw_ref[:, :LANES] = k0.astype(BF16)
    kw_ref[:, LANES:] = k1.astype(BF16)
    v0, v1 = _dup_heads(kv[:, LANES:], lo)
    vw_ref[:, :LANES] = v0.astype(BF16)
    vw_ref[:, LANES:] = v1.astype(BF16)
    wq = _dot(u, w_ref[:, 4 * BW:5 * BW])
    for j in range(BW // LANES):
        sl = slice(j * LANES, (j + 1) * LANES)
        qw_ref[:, sl] = (_rope(wq[:, sl], cos, sin, first_half) * scale).astype(BF16)
    qn_ref[...] = (_dot(u, w_ref[:, 0:BW]) * scale).astype(BF16)
    kn_ref[...] = _dot(u, w_ref[:, BW:2 * BW]).astype(BF16)
    vn_ref[...] = _dot(u, w_ref[:, 2 * BW:3 * BW]).astype(BF16)
    fu_ref[...] = _dot(u, w_ref[:, 3 * BW:4 * BW]).astype(BF16)


def _proj(h, n_tiles, g, mod3, layer, w_in, cos_t, sin_t, tile_info):
    n_lat_tiles, tiles_per_batch, _, _ = tile_info
    tile_spec = pl.BlockSpec((TM, D), lambda i: (i, 0))
    vec_spec = pl.BlockSpec((1, D), lambda i: (0, 0))
    rope_spec = pl.BlockSpec((TM, LANES), lambda i: (jnp.where(i < n_lat_tiles, i % tiles_per_batch, tiles_per_batch), 0))
    rows = n_tiles * TM

    def out_spec(w):
        return pl.BlockSpec((TM, w), lambda i: (i, 0))

    widths = [BW, BW, BW, BW, BW, 2 * LANES, 2 * LANES]
    return pl.pallas_call(
        _proj_kernel,
        grid=(n_tiles,),
        in_specs=[tile_spec, vec_spec,
                  _mod_spec(layer, 3, *tile_info), _mod_spec(layer, 4, *tile_info),
                  _layer_spec(w_in, layer), rope_spec, rope_spec],
        out_specs=[out_spec(w) for w in widths],
        out_shape=[jax.ShapeDtypeStruct((rows, w), BF16) for w in widths],
        compiler_params=_cparams(("arbitrary",)),
        name="proj",
    )(h, g.reshape(1, D), mod3, mod3, w_in, cos_t, sin_t)


def _lane_tiles(s):
    return [s[:, j * LANES:(j + 1) * LANES] for j in range(s.shape[1] // LANES)]


def _attend(qs, key_sets, sink_tile=None):
    scores = []
    for k, _, bias in key_sets:
        s = _dot_nt(qs, k)
        if bias is not None:
            s = s + bias
        scores.append(s)
    tiles = [t for s in scores for t in _lane_tiles(s)]
    if sink_tile is not None:
        tiles.append(sink_tile)
    m = functools.reduce(jnp.maximum, tiles).max(axis=-1, keepdims=True)
    psum = None
    acc = None
    for s, (_, v, _) in zip(scores, key_sets):
        p = jnp.exp2(s - m)
        for t in _lane_tiles(p):
            psum = t if psum is None else psum + t
        o = _dot(p.astype(BF16), v)
        acc = o if acc is None else acc + o
    if sink_tile is not None:
        lane = lax.broadcasted_iota(jnp.int32, sink_tile.shape, 1)
        psum = psum + jnp.where(lane == 0, jnp.exp2(sink_tile - m), 0.0)
    return acc / psum.sum(axis=-1, keepdims=True)


def _stack_heads(q, rows):
    lane = lax.broadcasted_iota(jnp.int32, (rows, LANES), 1)
    lo = lane < HD
    zero = jnp.zeros_like(q)
    return jnp.concatenate([jnp.where(lo, q, zero), jnp.where(lo, zero, q)], axis=0), lo


def _na_kernel(q_ref, k_ref, v_ref, kc_ref, vc_ref, bias_ref, o_ref, *, rows):
    j = pl.program_id(1)
    krow = jnp.clip(NA_QR * j - NA_WIN_R // 2, 0, rows - NA_KR)
    kstart = pl.multiple_of(krow * GRID_W, GRID_W)
    for p in range(NA_HEADS // 2):
        sl = slice(p * LANES, (p + 1) * LANES)
        kb = k_ref[pl.ds(kstart, NA_K), sl]
        vb = v_ref[pl.ds(kstart, NA_K), sl]
        qs, lo = _stack_heads(q_ref[:, sl], NA_Q)
        o = _attend(qs, [(kb, vb, bias_ref[p]), (kc_ref[:, sl], vc_ref[:, sl], None)])
        o_ref[:, sl] = jnp.where(lo, o[:NA_Q], o[NA_Q:]).astype(BF16)


def _na(qn, kn, vn, bias_tab, n_batch, seq, ctx_len):
    rows = seq // GRID_W
    n_blk = rows // NA_QR
    n_pair = NA_HEADS // 2
    ctx_blk0 = (n_batch * seq) // ctx_len

    def pat(j):
        return jnp.where(j == 0, 0, jnp.where(j == n_blk - 1, 2, 1))

    return pl.pallas_call(
        functools.partial(_na_kernel, rows=rows),
        grid=(n_batch, n_blk),
        in_specs=[
            pl.BlockSpec((NA_Q, BW), lambda b, j: (b * n_blk + j, 0)),
            pl.BlockSpec((seq, BW), lambda b, j: (b, 0)),
            pl.BlockSpec((seq, BW), lambda b, j: (b, 0)),
            pl.BlockSpec((ctx_len, BW), lambda b, j: (ctx_blk0 + b, 0)),
            pl.BlockSpec((ctx_len, BW), lambda b, j: (ctx_blk0 + b, 0)),
            pl.BlockSpec((None, n_pair, 2 * NA_Q, NA_K), lambda b, j: (pat(j), 0, 0, 0)),
        ],
        out_specs=pl.BlockSpec((NA_Q, BW), lambda b, j: (b * n_blk + j, 0)),
        out_shape=jax.ShapeDtypeStruct((n_batch * seq, BW), BF16),
        compiler_params=_cparams(("arbitrary", "arbitrary")),
        name="na",
    )(qn, kn, vn, kn, vn, bias_tab)


def _na_bias_table(bias, rows):
    n_blk = rows // NA_QR
    h = bias.shape[0]
    n_dc = 2 * NA_WIN_C - 1
    qc = np.arange(GRID_W)[:, None]
    kc = np.arange(GRID_W)[None, :]
    ws = np.clip(qc - NA_WIN_C // 2, 0, GRID_W - NA_WIN_C)
    col_ok = (kc >= ws) & (kc < ws + NA_WIN_C)
    dc = np.clip(kc - qc, -(NA_WIN_C - 1), NA_WIN_C - 1) + NA_WIN_C - 1
    onehot = (dc[None] == np.arange(n_dc)[:, None, None]).astype(np.float32)
    toep = jnp.einsum('hrd,dqk->hrqk', bias.astype(F32), jnp.asarray(onehot), precision=lax.Precision.HIGHEST)
    toep = jnp.where(col_ok[None, None], toep * LOG2E, NEG_INF)
    n_dr = toep.shape[1]
    toep = jnp.concatenate([toep, jnp.full((h, 1, GRID_W, GRID_W), NEG_INF, F32)], axis=1)
    toep2 = jnp.concatenate([toep, toep], axis=-1)
    plan = []
    for blk in (0, 1, n_blk - 1):
        k0 = min(max(NA_QR * blk - NA_WIN_R // 2, 0), rows - NA_KR)
        per_row = []
        for i in range(NA_QR):
            r = NA_QR * blk + i
            rs = min(max(r - NA_WIN_R // 2, 0), rows - NA_WIN_R)
            slabs = [k0 + t - r + NA_WIN_R - 1 if rs <= k0 + t < rs + NA_WIN_R else n_dr for t in range(NA_KR)]
            per_row.append([(slabs[2 * j], slabs[2 * j + 1]) for j in range(NA_KR // 2)])
        plan.append(per_row)
    return pl.pallas_call(
        functools.partial(_bias_expand_kernel, plan=plan),
        out_shape=jax.ShapeDtypeStruct((len(plan), h // 2, 2 * NA_Q, NA_K), F32),
        compiler_params=pltpu.CompilerParams(vmem_limit_bytes=VMEM_LIMIT),
        name="na_bias_expand",
    )(toep2)


def _bias_expand_kernel(t_ref, o_ref, *, plan):
    lo = lax.broadcasted_iota(jnp.int32, (GRID_W, LANES), 1) < GRID_W
    for pat, per_row in enumerate(plan):
        for hd in range(t_ref.shape[0]):
            for i, pairs in enumerate(per_row):
                r0 = (hd % 2) * NA_Q + i * GRID_W
                for j, (da, db) in enumerate(pairs):
                    o_ref[pat, hd // 2, r0:r0 + GRID_W, j * LANES:(j + 1) * LANES] = (
                        jnp.where(lo, t_ref[hd, da], t_ref[hd, db]))


def _stack_group(q):
    rows = q.shape[0]
    a, lo = _stack_heads(q[:, :LANES], rows)
    b, _ = _stack_heads(q[:, LANES:], rows)
    return jnp.concatenate([a, b], axis=0), lo


def _unstack_group(o, rows, lo):
    oa = jnp.where(lo, o[0:rows], o[rows:2 * rows])
    ob = jnp.where(lo, o[2 * rows:3 * rows], o[3 * rows:4 * rows])
    return jnp.concatenate([oa, ob], axis=1)


def _sink_tile(sink_ref, g, rows):
    gq = WA_HEADS // WA_KV_HEADS
    return jnp.concatenate([jnp.full((rows, LANES), sink_ref[g * gq + h] * LOG2E, F32) for h in range(gq)], axis=0)


def _wa_kernel(sink_ref, q_ref, k_ref, v_ref, kc_ref, vc_ref, o_ref, *, seq):
    for t in range(WA_QB):
        n = pl.program_id(1) * WA_QB + t
        rows = slice(t * WA_Q, (t + 1) * WA_Q)
        kstart = pl.multiple_of(jnp.clip(n * WA_Q - WA_Q, 0, seq - WA_K), WA_Q)
        qpos = n * WA_Q + lax.broadcasted_iota(jnp.int32, (WA_Q, WA_K), 0)
        kpos = kstart + lax.broadcasted_iota(jnp.int32, (WA_Q, WA_K), 1)
        band = jnp.where(jnp.abs(kpos - qpos) <= WA_WINDOW, 0.0, NEG_INF).astype(F32)
        band = jnp.concatenate([band] * (WA_HEADS // WA_KV_HEADS), axis=0)
        for g in range(WA_KV_HEADS):
            sl = slice(g * LANES, (g + 1) * LANES)
            sl2 = slice(g * 2 * LANES, (g + 1) * 2 * LANES)
            kb = k_ref[pl.ds(kstart, WA_K), sl]
            vb = v_ref[pl.ds(kstart, WA_K), sl]
            qs, lo = _stack_group(q_ref[rows, sl2])
            o = _attend(qs, [(kb, vb, band), (kc_ref[:, sl], vc_ref[:, sl], None)], _sink_tile(sink_ref, g, WA_Q))
            o_ref[rows, sl2] = _unstack_group(o, WA_Q, lo).astype(BF16)


def _wa(sink, qw, kw, vw, n_batch, seq, ctx_len):
    n_blk = seq // (WA_Q * WA_QB)
    ctx_blk0 = (n_batch * seq) // ctx_len
    gw = 2 * LANES
    return pl.pallas_call(
        functools.partial(_wa_kernel, seq=seq),
        grid=(n_batch, n_blk),
        in_specs=[
            pl.BlockSpec(memory_space=pltpu.SMEM),
            pl.BlockSpec((WA_Q * WA_QB, BW), lambda b, n: (b * n_blk + n, 0)),
            pl.BlockSpec((seq, gw), lambda b, n: (b, 0)),
            pl.BlockSpec((seq, gw), lambda b, n: (b, 0)),
            pl.BlockSpec((ctx_len, gw), lambda b, n: (ctx_blk0 + b, 0)),
            pl.BlockSpec((ctx_len, gw), lambda b, n: (ctx_blk0 + b, 0)),
        ],
        out_specs=pl.BlockSpec((WA_Q * WA_QB, BW), lambda b, n: (b * n_blk + n, 0)),
        out_shape=jax.ShapeDtypeStruct((n_batch * seq, BW), BF16),
        compiler_params=_cparams(("arbitrary", "arbitrary")),
        name="wa",
    )(sink, qw, kw, vw, kw, vw)


FFT_BLK = 16


def _swap_major(x):
    return jnp.swapaxes(x, 0, 1)


def _fft1_kernel(f_ref, x_ref, ar_ref, ai_ref, a_scr):
    xt = _swap_major(x_ref[...].astype(F32)).astype(BF16)
    f = f_ref[...]
    for i in range(FFT_BLK):
        a_scr[i] = _dot(f, xt[i])
    at = _swap_major(a_scr[...])
    ar_ref[...] = at[:GRID_W].astype(BF16)
    ai_ref[...] = at[GRID_W:].astype(BF16)


def _fft2_kernel(m_ref, ar_ref, ai_ref, zr_ref, zi_ref, z_scr):
    for t in range(FFT_BLK):
        a = jnp.concatenate([ar_ref[t], ai_ref[t]], axis=0)
        z_scr[t] = _dot(m_ref[t], a)
    zt = _swap_major(z_scr[...])
    zr_ref[...] = zt[:GRID_W].astype(BF16)
    zi_ref[...] = zt[GRID_W:].astype(BF16)


def _fft_tables():
    n = GRID_W
    k = np.arange(n)
    ang1 = 2.0 * np.pi * ((k[:, None] * k[None, :]) % n) / n
    f1 = np.concatenate([np.cos(ang1), -np.sin(ang1)], axis=0) / 8.0
    ka = k[:, None, None]
    kb = k[None, :, None]
    n1 = k[None, None, :]
    ang2 = 2.0 * np.pi * ((n1 * (ka + n * kb)) % (n * n)) / (n * n)
    mr, mi = np.cos(ang2), -np.sin(ang2)
    m2 = np.concatenate([np.concatenate([mr, -mi], axis=2), np.concatenate([mi, mr], axis=2)], axis=1) / 8.0
    c = np.arange(FN_GROUP_DIM)
    angc = 2.0 * np.pi * ((c[:, None] * c[None, :]) % FN_GROUP_DIM) / FN_GROUP_DIM
    eye = np.eye(FN_GROUPS)
    cbd = np.kron(eye, np.cos(angc)) / 8.0
    sbd = np.kron(eye, np.sin(angc)) / 8.0
    return f1, m2, cbd, sbd


def _ctx_dft_table(ctx_len):
    k = np.arange(ctx_len)
    ang = 2.0 * np.pi * ((k[:, None] * k[None, :]) % ctx_len) / ctx_len
    return np.concatenate([np.cos(ang), -np.sin(ang)], axis=0) / np.sqrt(ctx_len)


def _fft(fu, f1, m2, n_batch, seq):
    n = GRID_W
    n_blk = n // FFT_BLK
    x3 = fu.reshape(-1, n, BW)
    col_spec = pl.BlockSpec((n, FFT_BLK, BW), lambda b, j: (b, j, 0))
    row_spec = pl.BlockSpec((FFT_BLK, n, BW), lambda b, j: (b * n_blk + j, 0, 0))
    shape3 = jax.ShapeDtypeStruct((n_batch * n, n, BW), BF16)
    scratch = [pltpu.VMEM((FFT_BLK, 2 * n, BW), F32)]
    ar, ai = pl.pallas_call(
        _fft1_kernel,
        grid=(n_batch, n_blk),
        in_specs=[pl.BlockSpec((2 * n, n), lambda b, j: (0, 0)), col_spec],
        out_specs=[col_spec, col_spec],
        out_shape=[shape3, shape3],
        scratch_shapes=scratch,
        compiler_params=_cparams(("arbitrary", "arbitrary")),
        name="fft1",
    )(f1, x3)
    zr, zi = pl.pallas_call(
        _fft2_kernel,
        grid=(n_batch, n_blk),
        in_specs=[pl.BlockSpec((FFT_BLK, 2 * n, 2 * n), lambda b, j: (j, 0, 0)), row_spec, row_spec],
        out_specs=[col_spec, col_spec],
        out_shape=[shape3, shape3],
        scratch_shapes=scratch,
        compiler_params=_cparams(("arbitrary", "arbitrary")),
        name="fft2",
    )(m2, ar, ai)
    return zr.reshape(n_batch * seq, BW), zi.reshape(n_batch * seq, BW)


def _ctx_kernel(sink_ref, qn_ref, kn_ref, vn_ref, fu_ref, qw_ref, kw_ref, vw_ref, dft_ref,
                a_ref, w_ref, zr_ref, zi_ref, *, ctx_len):
    for p in range(NA_HEADS // 2):
        sl = slice(p * LANES, (p + 1) * LANES)
        qs, lo = _stack_heads(qn_ref[:, sl], ctx_len)
        o = _attend(qs, [(kn_ref[:, sl], vn_ref[:, sl], None)])
        a_ref[:, sl] = jnp.where(lo, o[:ctx_len], o[ctx_len:]).astype(BF16)
    for g in range(WA_KV_HEADS):
        sl = slice(g * LANES, (g + 1) * LANES)
        sl2 = slice(g * 2 * LANES, (g + 1) * 2 * LANES)
        qs, lo = _stack_group(qw_ref[:, sl2])
        o = _attend(qs, [(kw_ref[:, sl], vw_ref[:, sl], None)], _sink_tile(sink_ref, g, ctx_len))
        w_ref[:, sl2] = _unstack_group(o, ctx_len, lo).astype(BF16)
    z = _dot(dft_ref[...], fu_ref[...])
    zr_ref[...] = z[:ctx_len].astype(BF16)
    zi_ref[...] = z[ctx_len:].astype(BF16)


def _ctx_mix(sink, qn, kn, vn, fu, qw, kw, vw, dft_c, n_batch, seq, ctx_len):
    blk0 = (n_batch * seq) // ctx_len

    def in_spec(w):
        return pl.BlockSpec((ctx_len, w), lambda b: (blk0 + b, 0))

    out_spec = pl.BlockSpec((ctx_len, BW), lambda b: (b, 0))
    return pl.pallas_call(
        functools.partial(_ctx_kernel, ctx_len=ctx_len),
        grid=(n_batch,),
        in_specs=[pl.BlockSpec(memory_space=pltpu.SMEM),
                  in_spec(BW), in_spec(BW), in_spec(BW), in_spec(BW), in_spec(BW),
                  in_spec(2 * LANES), in_spec(2 * LANES),
                  pl.BlockSpec(dft_c.shape, lambda b: (0, 0))],
        out_specs=[out_spec] * 4,
        out_shape=[jax.ShapeDtypeStruct((n_batch * ctx_len, BW), BF16)] * 4,
        compiler_params=_cparams(("arbitrary",)),
        name="ctx_mix",
    )(sink, qn, kn, vn, fu, qw, kw, vw, dft_c)


def _merge_kernel(x_ref, g_ref, sh_ref, sc_ref, gt_ref, a_ref, zr_ref, zi_ref, w_ref, *rest, n_lat_tiles):
    if n_lat_tiles is None:
        wg_ref, wbr_ref, cbd_ref, sbd_ref, wout_ref, o_ref = rest
        a, zr, zi, w = a_ref[...], zr_ref[...], zi_ref[...], w_ref[...]
    else:
        ac_ref, zrc_ref, zic_ref, wc_ref, wg_ref, wbr_ref, cbd_ref, sbd_ref, wout_ref, o_ref = rest
        is_ctx = pl.program_id(0) >= n_lat_tiles
        a = jnp.where(is_ctx, ac_ref[...], a_ref[...])
        zr = jnp.where(is_ctx, zrc_ref[...], zr_ref[...])
        zi = jnp.where(is_ctx, zic_ref[...], zi_ref[...])
        w = jnp.where(is_ctx, wc_ref[...], w_ref[...])
    x = x_ref[...]
    u = _norm_mod(x, g_ref[...], sh_ref[...], sc_ref[...]).astype(BF16)
    f = (_dot(zr, cbd_ref[...]) + _dot(zi, sbd_ref[...])).astype(BF16)
    acc = None
    for i, br in enumerate((a, f, w)):
        gate = _sigmoid(_dot(u, wg_ref[:, i * D:(i + 1) * D]))
        term = gate * _dot(br, wbr_ref[i])
        acc = term if acc is None else acc + term
    o_ref[...] = x + gt_ref[...] * _dot(acc.astype(BF16), wout_ref[...])


def _merge(h, n_tiles, g, mod3, layer, branches, ctx_branches, weights, tile_info):
    n_lat_tiles = tile_info[0]
    tile_spec = pl.BlockSpec((TM, D), lambda i: (i, 0))
    vec_spec = pl.BlockSpec((1, D), lambda i: (0, 0))
    has_ctx = ctx_branches is not None
    lat_spec = pl.BlockSpec((TM, BW), lambda i: (jnp.minimum(i, n_lat_tiles - 1), 0))
    ctx_spec = pl.BlockSpec((TM, BW), lambda i: (jnp.maximum(i - n_lat_tiles, 0), 0))
    in_specs = [tile_spec, vec_spec,
                _mod_spec(layer, 3, *tile_info), _mod_spec(layer, 4, *tile_info), _mod_spec(layer, 5, *tile_info)]
    in_specs += [lat_spec] * 4
    args = [h, g.reshape(1, D), mod3, mod3, mod3, *branches]
    if has_ctx:
        in_specs += [ctx_spec] * 4
        args += list(ctx_branches)
    w_gate, w_br, cbd, sbd, w_out = weights
    in_specs += [_layer_spec(w_gate, layer), _layer_spec(w_br, layer), _const_spec(cbd.shape), _const_spec(sbd.shape),
                 _layer_spec(w_out, layer)]
    args += list(weights)
    return pl.pallas_call(
        functools.partial(_merge_kernel, n_lat_tiles=n_lat_tiles if has_ctx else None),
        grid=(n_tiles,),
        in_specs=in_specs,
        out_specs=tile_spec,
        out_shape=jax.ShapeDtypeStruct((n_tiles * TM, D), F32),
        compiler_params=_cparams(("arbitrary",)),
        name="merge_ctx" if has_ctx else "merge",
    )(*args)


def _rope_tables(seq):
    t = np.arange(seq)
    row = (t // GRID_W).astype(np.float64)
    col = (t % GRID_W).astype(np.float64)
    n_freq = HD // 4
    inv = ROPE_BASE ** (-np.arange(n_freq, dtype=np.float64) / n_freq)
    ang = np.concatenate([row[:, None] * inv, col[:, None] * inv], axis=-1)
    cos, sin = np.cos(ang), np.sin(ang)
    cos_h = np.concatenate([cos, cos], axis=1)
    sin_h = np.concatenate([-sin, sin], axis=1)
    cos2 = np.concatenate([np.tile(cos_h, (1, 2)), np.ones((TM, LANES), np.float32)], axis=0)
    sin2 = np.concatenate([np.tile(sin_h, (1, 2)), np.zeros((TM, LANES), np.float32)], axis=0)
    return jnp.asarray(cos2, F32), jnp.asarray(sin2, F32)


def kernel(x, c, ctx, c_ctx, w_ada, b_ada, g_ffn1, ffn1_w13, ffn1_w2, g_mix, w_in, na_bias, wa_sink,
           w_br, w_out, g_ffn2, ffn2_w13, ffn2_w2, g_final):
    n_batch, seq, _ = x.shape
    ctx_len = ctx.shape[1]
    depth = w_ada.shape[0]
    rows = seq // GRID_W
    n_lat = n_batch * seq
    n_lat_tiles = n_lat // TM
    n_all_tiles = (n_lat + n_batch * ctx_len) // TM
    mod_rows = 8
    tile_info = (n_lat_tiles, seq // TM, n_batch, mod_rows)

    cc = jnp.concatenate([c, c_ctx[None], jnp.zeros((mod_rows - n_batch - 1, D), F32)], axis=0)
    mod3 = _ada(cc, w_ada, b_ada).reshape(depth * mod_rows, 1, N_MOD * D)

    cos_t, sin_t = _rope_tables(seq)
    f1, m2, cbd, sbd = (jnp.asarray(t, F32).astype(BF16) for t in _fft_tables())
    dft_c = jnp.asarray(_ctx_dft_table(ctx_len), F32).astype(BF16)

    w13a, w2a = ffn1_w13, ffn1_w2
    w13b, w2b = ffn2_w13, ffn2_w2
    w_in_b = w_in[:, :, :PROJ_W].astype(BF16)
    merge_w = (w_in[:, :, PROJ_W:].astype(BF16), w_br.astype(BF16), cbd, sbd, w_out.astype(BF16))

    h = x.reshape(n_lat, D)
    h_ctx = ctx.reshape(n_batch * ctx_len, D)
    for l in range(depth):
        last = l == depth - 1
        bias_tab = _na_bias_table(na_bias[l], rows)

        h = _ffn(h, n_all_tiles, g_ffn1[l], mod3, l, 0, w13a, w2a, tile_info, h_ctx=h_ctx if l == 0 else None)
        qn, kn, vn, fu, qw, kw, vw = _proj(h, n_all_tiles, g_mix[l], mod3, l, w_in_b, cos_t, sin_t, tile_info)
        a = _na(qn, kn, vn, bias_tab, n_batch, seq, ctx_len)
        w = _wa(wa_sink[l], qw, kw, vw, n_batch, seq, ctx_len)
        zr, zi = _fft(fu, f1, m2, n_batch, seq)
        if last:
            h = _merge(h, n_lat_tiles, g_mix[l], mod3, l, (a, zr, zi, w), None, merge_w, tile_info)
            h = _ffn(h, n_lat_tiles, g_ffn2[l], mod3, l, 6, w13b, w2b, tile_info, g_final=g_final)
        else:
            ctx_br = _ctx_mix(wa_sink[l], qn, kn, vn, fu, qw, kw, vw, dft_c, n_batch, seq, ctx_len)
            ac, wc, zrc, zic = ctx_br
            h = _merge(h, n_all_tiles, g_mix[l], mod3, l, (a, zr, zi, w), (ac, zrc, zic, wc), merge_w, tile_info)
            h = _ffn(h, n_all_tiles, g_ffn2[l], mod3, l, 6, w13b, w2b, tile_info)
    return h.reshape(n_batch, seq, D)
```

```python
import functools

import numpy as np
import jax
import jax.numpy as jnp
from jax import lax
from jax.experimental import pallas as pl
from jax.experimental.pallas import tpu as pltpu

D = 1024
GRID_W = 64
HD = 64
NA_HEADS = 8
NA_WIN_R = 8
NA_WIN_C = 16
FN_GROUPS = 8
FN_GROUP_DIM = 64
WA_HEADS = 8
WA_KV_HEADS = 2
WA_WINDOW = 128
D_FF = 2816
N_MOD = 9
ROPE_BASE = 10000.0
EPS = 1e-6
NEG_INF = -1e30
LOG2E = 1.4426950408889634
BW = 512

LANES = 128
MXU_DIM = 256
TM = 512
TF = MXU_DIM
VMEM_LIMIT = 56 * 1024 * 1024

NA_QR = 4
NA_KR = 12
NA_Q = NA_QR * GRID_W
NA_K = NA_KR * GRID_W
WA_Q = 128
WA_K = 3 * WA_Q
WA_QB = 4

F32 = jnp.float32
BF16 = jnp.bfloat16


def _cparams(sem):
    return pltpu.CompilerParams(dimension_semantics=sem, vmem_limit_bytes=VMEM_LIMIT)


def _const_spec(shape):
    nd = len(shape)
    return pl.BlockSpec(shape, lambda *_: (0,) * nd, pipeline_mode=pl.Buffered(1))


def _sigmoid(x):
    return 1.0 / (1.0 + jnp.exp(-x))


def _norm_mod(x, g, shift, scale):
    y = x * lax.rsqrt(jnp.mean(x * x, axis=-1, keepdims=True) + EPS) * g
    return y * (1.0 + scale) + shift


def _dot(a, b):
    return jnp.dot(a, b, preferred_element_type=F32)


def _dot_nt(a, b):
    return lax.dot_general(a, b, (((1,), (1,)), ((), ())), preferred_element_type=F32)


def _ada_kernel(c_ref, w_ref, b_ref, o_ref):
    x = c_ref[...]
    sx = (x * _sigmoid(x)).astype(BF16)
    o_ref[...] = _dot(sx, w_ref[...].astype(BF16)) + b_ref[...]


def _ada(cc, w_ada, b_ada):
    depth = w_ada.shape[0]
    n = w_ada.shape[2]
    tn = 1536
    rows = cc.shape[0]
    return pl.pallas_call(
        _ada_kernel,
        grid=(depth, n // tn),
        in_specs=[
            pl.BlockSpec((rows, D), lambda l, j: (0, 0)),
            pl.BlockSpec((None, D, tn), lambda l, j: (l, 0, j)),
            pl.BlockSpec((None, 1, tn), lambda l, j: (l, 0, j)),
        ],
        out_specs=pl.BlockSpec((None, rows, tn), lambda l, j: (l, 0, j)),
        out_shape=jax.ShapeDtypeStruct((depth, rows, n), F32),
        compiler_params=_cparams(("arbitrary", "arbitrary")),
        name="ada",
    )(cc, w_ada, b_ada.reshape(depth, 1, n))


def _mod_spec(layer, k, n_lat_tiles, tiles_per_batch, n_batch, mod_rows, tile_of=lambda i: i):
    def idx(i):
        t = tile_of(i)
        row = jnp.where(t < n_lat_tiles, t // tiles_per_batch, n_batch)
        return (layer * mod_rows + row, 0, k)

    return pl.BlockSpec((None, 1, D), idx)


FFN_W13_CH = 2 * TF
FFN_WSTEPS = 2 * D_FF // FFN_W13_CH
FFN_W2_CH = D_FF // FFN_WSTEPS


def _ffn_kernel(*refs, final, split_at):
    refs = list(refs)
    x_ref = refs.pop(0)
    if split_at is not None:
        xc_ref = refs.pop(0)
    g_ref, sh_ref, sc_ref, gt_ref, w13_ref, w2_ref = refs[:6]
    if final:
        gf_ref, o_ref, w13_s, w2_s, hm_ref = refs[6:]
    else:
        o_ref, w13_s, w2_s, hm_ref = refs[6:]
    i = pl.program_id(0)

    @pl.when(i < FFN_WSTEPS)
    def _():
        w13_s[i] = w13_ref[...].astype(BF16)
        w2_s[i] = w2_ref[...].astype(BF16)

    def w13_cols(lo):
        off = lo % FFN_W13_CH
        return w13_s[lo // FFN_W13_CH][:, off:off + TF]

    @pl.when(i >= FFN_WSTEPS)
    def _():
        x = x_ref[...]
        if split_at is not None:
            x = jnp.where(i - FFN_WSTEPS >= split_at, xc_ref[...], x)
        u = _norm_mod(x, g_ref[...], sh_ref[...], sc_ref[...]).astype(BF16)
        for lo in range(0, D_FF, TF):
            a = _dot(u, w13_cols(lo))
            b = _dot(u, w13_cols(D_FF + lo))
            hm_ref[:, lo:lo + TF] = (a * _sigmoid(a) * b).astype(BF16)
        f = _dot(hm_ref[...], w2_s[...].reshape(D_FF, D))
        out = x + 0.5 * gt_ref[...] * f
        if final:
            out = out * lax.rsqrt(jnp.mean(out * out, axis=-1, keepdims=True) + EPS) * gf_ref[...]
        o_ref[...] = out


def _ffn(h, n_tiles, g, mod3, layer, mod_k, w13, w2, tile_info, g_final=None, h_ctx=None):
    final = g_final is not None
    n_lat_tiles = tile_info[0]

    def tile_of(i):
        return jnp.maximum(i - FFN_WSTEPS, 0)

    def wstep_of(i):
        return jnp.minimum(i, FFN_WSTEPS - 1)

    tile_spec = pl.BlockSpec((TM, D), lambda i: (tile_of(i), 0))
    vec_spec = pl.BlockSpec((1, D), lambda i: (0, 0))
    if h_ctx is None:
        in_specs, args = [tile_spec], [h]
    else:
        in_specs = [pl.BlockSpec((TM, D), lambda i: (jnp.minimum(tile_of(i), n_lat_tiles - 1), 0)),
                    pl.BlockSpec((TM, D), lambda i: (jnp.maximum(tile_of(i) - n_lat_tiles, 0), 0))]
        args = [h, h_ctx]
    in_specs += [
        vec_spec,
        _mod_spec(layer, mod_k, *tile_info, tile_of=tile_of),
        _mod_spec(layer, mod_k + 1, *tile_info, tile_of=tile_of),
        _mod_spec(layer, mod_k + 2, *tile_info, tile_of=tile_of),
        pl.BlockSpec((None, D, FFN_W13_CH), lambda i: (layer, 0, wstep_of(i))),
        pl.BlockSpec((None, FFN_W2_CH, D), lambda i: (layer, wstep_of(i), 0)),
    ]
    args += [g.reshape(1, D), mod3, mod3, mod3, w13, w2]
    if final:
        in_specs.append(vec_spec)
        args.append(g_final.reshape(1, D))
    return pl.pallas_call(
        functools.partial(_ffn_kernel, final=final, split_at=None if h_ctx is None else n_lat_tiles),
        grid=(FFN_WSTEPS + n_tiles,),
        in_specs=in_specs,
        out_specs=tile_spec,
        out_shape=jax.ShapeDtypeStruct((n_tiles * TM, D), F32),
        scratch_shapes=[pltpu.VMEM((FFN_WSTEPS, D, FFN_W13_CH), BF16),
                        pltpu.VMEM((FFN_WSTEPS, FFN_W2_CH, D), BF16),
                        pltpu.VMEM((TM, D_FF), BF16)],
        compiler_params=_cparams(("arbitrary",)),
        name="ffn_final" if final else ("ffn" if h_ctx is None else "ffn_split"),
    )(*args)


PROJ_W = 5 * BW + 2 * LANES


def _rope(t, cos, sin_signed, first_half):
    partner = jnp.where(first_half, pltpu.roll(t, LANES - HD // 2, axis=1), pltpu.roll(t, HD // 2, axis=1))
    return t * cos + partner * sin_signed


def _dup_heads(t, lo):
    sw = pltpu.roll(t, HD, axis=1)
    return jnp.where(lo, t, sw), jnp.where(lo, sw, t)


W_IN_CH = MXU_DIM
PROJ_WSTEPS = PROJ_W // W_IN_CH


def _chunk_cols(ws_ref, lo, hi, ch):
    parts = [ws_ref[c] for c in range(lo // ch, hi // ch)]
    return parts[0] if len(parts) == 1 else jnp.concatenate(parts, axis=1)


def _proj_kernel(x_ref, g_ref, sh_ref, sc_ref, w_ref, cos_ref, sin_ref,
                 qn_ref, kn_ref, vn_ref, fu_ref, qw_ref, kw_ref, vw_ref, w_s):
    i = pl.program_id(0)

    @pl.when(i < PROJ_WSTEPS)
    def _():
        w_s[i] = w_ref[...].astype(BF16)

    def w(lo, hi):
        return _chunk_cols(w_s, lo, hi, W_IN_CH)

    @pl.when(i >= PROJ_WSTEPS)
    def _():
        u = _norm_mod(x_ref[...], g_ref[...], sh_ref[...], sc_ref[...]).astype(BF16)
        scale = HD ** -0.5 * LOG2E
        cos = cos_ref[...]
        sin = sin_ref[...]
        lane = lax.broadcasted_iota(jnp.int32, (TM, LANES), 1)
        first_half = (lane & (HD - 1)) < (HD // 2)
        lo = lane < HD
        kv = _dot(u, w(5 * BW, PROJ_W))
        k0, k1 = _dup_heads(_rope(kv[:, :LANES], cos, sin, first_half), lo)
        kw_ref[:, :LANES] = k0.astype(BF16)
        kw_ref[:, LANES:] = k1.astype(BF16)
        v0, v1 = _dup_heads(kv[:, LANES:], lo)
        vw_ref[:, :LANES] = v0.astype(BF16)
        vw_ref[:, LANES:] = v1.astype(BF16)
        wq = _dot(u, w(4 * BW, 5 * BW))
        for j in range(BW // LANES):
            sl = slice(j * LANES, (j + 1) * LANES)
            qw_ref[:, sl] = (_rope(wq[:, sl], cos, sin, first_half) * scale).astype(BF16)
        qn_ref[...] = (_dot(u, w(0, BW)) * scale).astype(BF16)
        kn_ref[...] = _dot(u, w(BW, 2 * BW)).astype(BF16)
        vn_ref[...] = _dot(u, w(2 * BW, 3 * BW)).astype(BF16)
        fu_ref[...] = _dot(u, w(3 * BW, 4 * BW)).astype(BF16)


def _proj(h, n_tiles, g, mod3, layer, w_in, cos_t, sin_t, tile_info):
    n_lat_tiles, tiles_per_batch, _, _ = tile_info

    def tile_of(i):
        return jnp.maximum(i - PROJ_WSTEPS, 0)

    def rope_blk(i):
        t = tile_of(i)
        return jnp.where(t < n_lat_tiles, t % tiles_per_batch, tiles_per_batch)

    tile_spec = pl.BlockSpec((TM, D), lambda i: (tile_of(i), 0))
    vec_spec = pl.BlockSpec((1, D), lambda i: (0, 0))
    rope_spec = pl.BlockSpec((TM, LANES), lambda i: (rope_blk(i), 0))
    w_spec = pl.BlockSpec((None, D, W_IN_CH), lambda i: (layer, 0, jnp.minimum(i, PROJ_WSTEPS - 1)))
    rows = n_tiles * TM

    def out_spec(w):
        return pl.BlockSpec((TM, w), lambda i: (tile_of(i), 0))

    widths = [BW, BW, BW, BW, BW, 2 * LANES, 2 * LANES]
    return pl.pallas_call(
        _proj_kernel,
        grid=(PROJ_WSTEPS + n_tiles,),
        in_specs=[tile_spec, vec_spec,
                  _mod_spec(layer, 3, *tile_info, tile_of=tile_of), _mod_spec(layer, 4, *tile_info, tile_of=tile_of),
                  w_spec, rope_spec, rope_spec],
        out_specs=[out_spec(w) for w in widths],
        out_shape=[jax.ShapeDtypeStruct((rows, w), BF16) for w in widths],
        scratch_shapes=[pltpu.VMEM((PROJ_WSTEPS, D, W_IN_CH), BF16)],
        compiler_params=_cparams(("arbitrary",)),
        name="proj",
    )(h, g.reshape(1, D), mod3, mod3, w_in, cos_t, sin_t)


def _lane_tiles(s):
    return [s[:, j * LANES:(j + 1) * LANES] for j in range(s.shape[1] // LANES)]


def _attend(qs, key_sets, sink_tile=None):
    scores = []
    for k, _, bias in key_sets:
        s = _dot_nt(qs, k)
        if bias is not None:
            s = s + bias
        scores.append(s)
    tiles = [t for s in scores for t in _lane_tiles(s)]
    if sink_tile is not None:
        tiles.append(sink_tile)
    m = functools.reduce(jnp.maximum, tiles).max(axis=-1, keepdims=True)
    psum = None
    acc = None
    for s, (_, v, _) in zip(scores, key_sets):
        p = jnp.exp2(s - m)
        for t in _lane_tiles(p):
            psum = t if psum is None else psum + t
        o = _dot(p.astype(BF16), v)
        acc = o if acc is None else acc + o
    if sink_tile is not None:
        lane = lax.broadcasted_iota(jnp.int32, sink_tile.shape, 1)
        psum = psum + jnp.where(lane == 0, jnp.exp2(sink_tile - m), 0.0)
    return acc / psum.sum(axis=-1, keepdims=True)


def _stack_heads(q, rows):
    lane = lax.broadcasted_iota(jnp.int32, (rows, LANES), 1)
    lo = lane < HD
    zero = jnp.zeros_like(q)
    return jnp.concatenate([jnp.where(lo, q, zero), jnp.where(lo, zero, q)], axis=0), lo


def _na_kernel(q_ref, k_ref, v_ref, kc_ref, vc_ref, bias_ref, o_ref, *, rows):
    j = pl.program_id(1)
    krow = jnp.clip(NA_QR * j - NA_WIN_R // 2, 0, rows - NA_KR)
    kstart = pl.multiple_of(krow * GRID_W, GRID_W)
    for p in range(NA_HEADS // 2):
        sl = slice(p * LANES, (p + 1) * LANES)
        kb = k_ref[pl.ds(kstart, NA_K), sl]
        vb = v_ref[pl.ds(kstart, NA_K), sl]
        qs, lo = _stack_heads(q_ref[:, sl], NA_Q)
        o = _attend(qs, [(kb, vb, bias_ref[p]), (kc_ref[:, sl], vc_ref[:, sl], None)])
        o_ref[:, sl] = jnp.where(lo, o[:NA_Q], o[NA_Q:]).astype(BF16)


def _na(qn, kn, vn, bias_tab, n_batch, seq, ctx_len):
    rows = seq // GRID_W
    n_blk = rows // NA_QR
    n_pair = NA_HEADS // 2
    ctx_blk0 = (n_batch * seq) // ctx_len

    def pat(j):
        return jnp.where(j == 0, 0, jnp.where(j == n_blk - 1, 2, 1))

    return pl.pallas_call(
        functools.partial(_na_kernel, rows=rows),
        grid=(n_batch, n_blk),
        in_specs=[
            pl.BlockSpec((NA_Q, BW), lambda b, j: (b * n_blk + j, 0)),
            pl.BlockSpec((seq, BW), lambda b, j: (b, 0)),
            pl.BlockSpec((seq, BW), lambda b, j: (b, 0)),
            pl.BlockSpec((ctx_len, BW), lambda b, j: (ctx_blk0 + b, 0)),
            pl.BlockSpec((ctx_len, BW), lambda b, j: (ctx_blk0 + b, 0)),
            pl.BlockSpec((None, n_pair, 2 * NA_Q, NA_K), lambda b, j: (pat(j), 0, 0, 0)),
        ],
        out_specs=pl.BlockSpec((NA_Q, BW), lambda b, j: (b * n_blk + j, 0)),
        out_shape=jax.ShapeDtypeStruct((n_batch * seq, BW), BF16),
        compiler_params=_cparams(("arbitrary", "arbitrary")),
        name="na",
    )(qn, kn, vn, kn, vn, bias_tab)


def _na_bias_table(bias, rows):
    n_blk = rows // NA_QR
    h = bias.shape[0]
    n_dc = 2 * NA_WIN_C - 1
    qc = np.arange(GRID_W)[:, None]
    kc = np.arange(GRID_W)[None, :]
    ws = np.clip(qc - NA_WIN_C // 2, 0, GRID_W - NA_WIN_C)
    col_ok = (kc >= ws) & (kc < ws + NA_WIN_C)
    dc = np.clip(kc - qc, -(NA_WIN_C - 1), NA_WIN_C - 1) + NA_WIN_C - 1
    onehot = (dc[None] == np.arange(n_dc)[:, None, None]).astype(np.float32)
    toep = jnp.einsum('hrd,dqk->hrqk', bias.astype(F32), jnp.asarray(onehot), precision=lax.Precision.HIGHEST)
    toep = jnp.where(col_ok[None, None], toep * LOG2E, NEG_INF)
    n_dr = toep.shape[1]
    toep = jnp.concatenate([toep, jnp.full((h, 1, GRID_W, GRID_W), NEG_INF, F32)], axis=1)
    toep2 = jnp.concatenate([toep, toep], axis=-1)
    plan = []
    for blk in (0, 1, n_blk - 1):
        k0 = min(max(NA_QR * blk - NA_WIN_R // 2, 0), rows - NA_KR)
        per_row = []
        for i in range(NA_QR):
            r = NA_QR * blk + i
            rs = min(max(r - NA_WIN_R // 2, 0), rows - NA_WIN_R)
            slabs = [k0 + t - r + NA_WIN_R - 1 if rs <= k0 + t < rs + NA_WIN_R else n_dr for t in range(NA_KR)]
            per_row.append([(slabs[2 * j], slabs[2 * j + 1]) for j in range(NA_KR // 2)])
        plan.append(per_row)
    return pl.pallas_call(
        functools.partial(_bias_expand_kernel, plan=plan),
        out_shape=jax.ShapeDtypeStruct((len(plan), h // 2, 2 * NA_Q, NA_K), F32),
        compiler_params=pltpu.CompilerParams(vmem_limit_bytes=VMEM_LIMIT),
        name="na_bias_expand",
    )(toep2)


def _bias_expand_kernel(t_ref, o_ref, *, plan):
    lo = lax.broadcasted_iota(jnp.int32, (GRID_W, LANES), 1) < GRID_W
    for pat, per_row in enumerate(plan):
        for hd in range(t_ref.shape[0]):
            for i, pairs in enumerate(per_row):
                r0 = (hd % 2) * NA_Q + i * GRID_W
                for j, (da, db) in enumerate(pairs):
                    o_ref[pat, hd // 2, r0:r0 + GRID_W, j * LANES:(j + 1) * LANES] = (
                        jnp.where(lo, t_ref[hd, da], t_ref[hd, db]))


def _stack_group(q):
    rows = q.shape[0]
    a, lo = _stack_heads(q[:, :LANES], rows)
    b, _ = _stack_heads(q[:, LANES:], rows)
    return jnp.concatenate([a, b], axis=0), lo


def _unstack_group(o, rows, lo):
    oa = jnp.where(lo, o[0:rows], o[rows:2 * rows])
    ob = jnp.where(lo, o[2 * rows:3 * rows], o[3 * rows:4 * rows])
    return jnp.concatenate([oa, ob], axis=1)


def _sink_tile(sink_ref, g, rows):
    gq = WA_HEADS // WA_KV_HEADS
    return jnp.concatenate([jnp.full((rows, LANES), sink_ref[g * gq + h] * LOG2E, F32) for h in range(gq)], axis=0)


def _wa_kernel(sink_ref, q_ref, k_ref, v_ref, kc_ref, vc_ref, o_ref, *, seq):
    for t in range(WA_QB):
        n = pl.program_id(1) * WA_QB + t
        rows = slice(t * WA_Q, (t + 1) * WA_Q)
        kstart = pl.multiple_of(jnp.clip(n * WA_Q - WA_Q, 0, seq - WA_K), WA_Q)
        qpos = n * WA_Q + lax.broadcasted_iota(jnp.int32, (WA_Q, WA_K), 0)
        kpos = kstart + lax.broadcasted_iota(jnp.int32, (WA_Q, WA_K), 1)
        band = jnp.where(jnp.abs(kpos - qpos) <= WA_WINDOW, 0.0, NEG_INF).astype(F32)
        band = jnp.concatenate([band] * (WA_HEADS // WA_KV_HEADS), axis=0)
        for g in range(WA_KV_HEADS):
            sl = slice(g * LANES, (g + 1) * LANES)
            sl2 = slice(g * 2 * LANES, (g + 1) * 2 * LANES)
            kb = k_ref[pl.ds(kstart, WA_K), sl]
            vb = v_ref[pl.ds(kstart, WA_K), sl]
            qs, lo = _stack_group(q_ref[rows, sl2])
            o = _attend(qs, [(kb, vb, band), (kc_ref[:, sl], vc_ref[:, sl], None)], _sink_tile(sink_ref, g, WA_Q))
            o_ref[rows, sl2] = _unstack_group(o, WA_Q, lo).astype(BF16)


def _wa(sink, qw, kw, vw, n_batch, seq, ctx_len):
    n_blk = seq // (WA_Q * WA_QB)
    ctx_blk0 = (n_batch * seq) // ctx_len
    gw = 2 * LANES
    return pl.pallas_call(
        functools.partial(_wa_kernel, seq=seq),
        grid=(n_batch, n_blk),
        in_specs=[
            pl.BlockSpec(memory_space=pltpu.SMEM),
            pl.BlockSpec((WA_Q * WA_QB, BW), lambda b, n: (b * n_blk + n, 0)),
            pl.BlockSpec((seq, gw), lambda b, n: (b, 0)),
            pl.BlockSpec((seq, gw), lambda b, n: (b, 0)),
            pl.BlockSpec((ctx_len, gw), lambda b, n: (ctx_blk0 + b, 0)),
            pl.BlockSpec((ctx_len, gw), lambda b, n: (ctx_blk0 + b, 0)),
        ],
        out_specs=pl.BlockSpec((WA_Q * WA_QB, BW), lambda b, n: (b * n_blk + n, 0)),
        out_shape=jax.ShapeDtypeStruct((n_batch * seq, BW), BF16),
        compiler_params=_cparams(("arbitrary", "arbitrary")),
        name="wa",
    )(sink, qw, kw, vw, kw, vw)


FFT_BLK = 16


def _swap_major(x):
    return jnp.swapaxes(x, 0, 1)


def _fft1_kernel(f_ref, x_ref, ar_ref, ai_ref, a_scr):
    xt = _swap_major(x_ref[...].astype(F32)).astype(BF16)
    f = f_ref[...]
    for i in range(FFT_BLK):
        a_scr[i] = _dot(f, xt[i])
    at = _swap_major(a_scr[...])
    ar_ref[...] = at[:GRID_W].astype(BF16)
    ai_ref[...] = at[GRID_W:].astype(BF16)


def _fft2_kernel(m_ref, ar_ref, ai_ref, zr_ref, zi_ref, z_scr):
    for t in range(FFT_BLK):
        a = jnp.concatenate([ar_ref[t], ai_ref[t]], axis=0)
        z_scr[t] = _dot(m_ref[t], a)
    zt = _swap_major(z_scr[...])
    zr_ref[...] = zt[:GRID_W].astype(BF16)
    zi_ref[...] = zt[GRID_W:].astype(BF16)


def _fft_tables():
    n = GRID_W
    k = np.arange(n)
    ang1 = 2.0 * np.pi * ((k[:, None] * k[None, :]) % n) / n
    f1 = np.concatenate([np.cos(ang1), -np.sin(ang1)], axis=0) / 8.0
    ka = k[:, None, None]
    kb = k[None, :, None]
    n1 = k[None, None, :]
    ang2 = 2.0 * np.pi * ((n1 * (ka + n * kb)) % (n * n)) / (n * n)
    mr, mi = np.cos(ang2), -np.sin(ang2)
    m2 = np.concatenate([np.concatenate([mr, -mi], axis=2), np.concatenate([mi, mr], axis=2)], axis=1) / 8.0
    c = np.arange(FN_GROUP_DIM)
    angc = 2.0 * np.pi * ((c[:, None] * c[None, :]) % FN_GROUP_DIM) / FN_GROUP_DIM
    eye = np.eye(FN_GROUPS)
    cbd = np.kron(eye, np.cos(angc)) / 8.0
    sbd = np.kron(eye, np.sin(angc)) / 8.0
    return f1, m2, cbd, sbd


def _ctx_dft_table(ctx_len):
    k = np.arange(ctx_len)
    ang = 2.0 * np.pi * ((k[:, None] * k[None, :]) % ctx_len) / ctx_len
    return np.concatenate([np.cos(ang), -np.sin(ang)], axis=0) / np.sqrt(ctx_len)


def _fft(fu, f1, m2, n_batch, seq):
    n = GRID_W
    n_blk = n // FFT_BLK
    x3 = fu.reshape(-1, n, BW)
    col_spec = pl.BlockSpec((n, FFT_BLK, BW), lambda b, j: (b, j, 0))
    row_spec = pl.BlockSpec((FFT_BLK, n, BW), lambda b, j: (b * n_blk + j, 0, 0))
    shape3 = jax.ShapeDtypeStruct((n_batch * n, n, BW), BF16)
    scratch = [pltpu.VMEM((FFT_BLK, 2 * n, BW), F32)]
    ar, ai = pl.pallas_call(
        _fft1_kernel,
        grid=(n_batch, n_blk),
        in_specs=[pl.BlockSpec((2 * n, n), lambda b, j: (0, 0)), col_spec],
        out_specs=[col_spec, col_spec],
        out_shape=[shape3, shape3],
        scratch_shapes=scratch,
        compiler_params=_cparams(("arbitrary", "arbitrary")),
        name="fft1",
    )(f1, x3)
    zr, zi = pl.pallas_call(
        _fft2_kernel,
        grid=(n_batch, n_blk),
        in_specs=[pl.BlockSpec((FFT_BLK, 2 * n, 2 * n), lambda b, j: (j, 0, 0)), row_spec, row_spec],
        out_specs=[col_spec, col_spec],
        out_shape=[shape3, shape3],
        scratch_shapes=scratch,
        compiler_params=_cparams(("arbitrary", "arbitrary")),
        name="fft2",
    )(m2, ar, ai)
    return zr.reshape(n_batch * seq, BW), zi.reshape(n_batch * seq, BW)


def _ctx_kernel(sink_ref, qn_ref, kn_ref, vn_ref, fu_ref, qw_ref, kw_ref, vw_ref, dft_ref,
                a_ref, w_ref, zr_ref, zi_ref, *, ctx_len):
    for p in range(NA_HEADS // 2):
        sl = slice(p * LANES, (p + 1) * LANES)
        qs, lo = _stack_heads(qn_ref[:, sl], ctx_len)
        o = _attend(qs, [(kn_ref[:, sl], vn_ref[:, sl], None)])
        a_ref[:, sl] = jnp.where(lo, o[:ctx_len], o[ctx_len:]).astype(BF16)
    for g in range(WA_KV_HEADS):
        sl = slice(g * LANES, (g + 1) * LANES)
        sl2 = slice(g * 2 * LANES, (g + 1) * 2 * LANES)
        qs, lo = _stack_group(qw_ref[:, sl2])
        o = _attend(qs, [(kw_ref[:, sl], vw_ref[:, sl], None)], _sink_tile(sink_ref, g, ctx_len))
        w_ref[:, sl2] = _unstack_group(o, ctx_len, lo).astype(BF16)
    z = _dot(dft_ref[...], fu_ref[...])
    zr_ref[...] = z[:ctx_len].astype(BF16)
    zi_ref[...] = z[ctx_len:].astype(BF16)


def _ctx_mix(sink, qn, kn, vn, fu, qw, kw, vw, dft_c, n_batch, seq, ctx_len):
    blk0 = (n_batch * seq) // ctx_len

    def in_spec(w):
        return pl.BlockSpec((ctx_len, w), lambda b: (blk0 + b, 0))

    out_spec = pl.BlockSpec((ctx_len, BW), lambda b: (b, 0))
    return pl.pallas_call(
        functools.partial(_ctx_kernel, ctx_len=ctx_len),
        grid=(n_batch,),
        in_specs=[pl.BlockSpec(memory_space=pltpu.SMEM),
                  in_spec(BW), in_spec(BW), in_spec(BW), in_spec(BW), in_spec(BW),
                  in_spec(2 * LANES), in_spec(2 * LANES),
                  pl.BlockSpec(dft_c.shape, lambda b: (0, 0))],
        out_specs=[out_spec] * 4,
        out_shape=[jax.ShapeDtypeStruct((n_batch * ctx_len, BW), BF16)] * 4,
        compiler_params=_cparams(("arbitrary",)),
        name="ctx_mix",
    )(sink, qn, kn, vn, fu, qw, kw, vw, dft_c)


N_BRANCH = 3
MERGE_WSTEPS = N_BRANCH * D // W_IN_CH
MERGE_BR_CH = N_BRANCH * BW // MERGE_WSTEPS
MERGE_OUT_CH = MERGE_BR_CH
MERGE_OUT_STEPS = D // MERGE_OUT_CH


def _merge_kernel(x_ref, g_ref, sh_ref, sc_ref, gt_ref, a_ref, zr_ref, zi_ref, w_ref, *rest, n_lat_tiles):
    if n_lat_tiles is None:
        wg_ref, wbr_ref, cbd_ref, sbd_ref, wout_ref, o_ref, wg_s, wbr_s, wout_s = rest
    else:
        (ac_ref, zrc_ref, zic_ref, wc_ref, wg_ref, wbr_ref, cbd_ref, sbd_ref, wout_ref, o_ref,
         wg_s, wbr_s, wout_s) = rest
    i = pl.program_id(0)

    @pl.when(i < MERGE_WSTEPS)
    def _():
        wg_s[i] = wg_ref[...].astype(BF16)
        wbr_s[i] = wbr_ref[...].astype(BF16)

    @pl.when(i < MERGE_OUT_STEPS)
    def _():
        wout_s[i] = wout_ref[...].astype(BF16)

    @pl.when(i >= MERGE_WSTEPS)
    def _():
        if n_lat_tiles is None:
            a, zr, zi, w = a_ref[...], zr_ref[...], zi_ref[...], w_ref[...]
        else:
            is_ctx = i - MERGE_WSTEPS >= n_lat_tiles
            a = jnp.where(is_ctx, ac_ref[...], a_ref[...])
            zr = jnp.where(is_ctx, zrc_ref[...], zr_ref[...])
            zi = jnp.where(is_ctx, zic_ref[...], zi_ref[...])
            w = jnp.where(is_ctx, wc_ref[...], w_ref[...])
        x = x_ref[...]
        u = _norm_mod(x, g_ref[...], sh_ref[...], sc_ref[...]).astype(BF16)
        f = (_dot(zr, cbd_ref[...]) + _dot(zi, sbd_ref[...])).astype(BF16)
        w_br = wbr_s[...].reshape(N_BRANCH * BW, D)
        acc = None
        for b, br in enumerate((a, f, w)):
            gate = _sigmoid(_dot(u, _chunk_cols(wg_s, b * D, (b + 1) * D, W_IN_CH)))
            term = gate * _dot(br, w_br[b * BW:(b + 1) * BW])
            acc = term if acc is None else acc + term
        o_ref[...] = x + gt_ref[...] * _dot(acc.astype(BF16), wout_s[...].reshape(D, D))


def _merge(h, n_tiles, g, mod3, layer, branches, ctx_branches, weights, tile_info):
    n_lat_tiles = tile_info[0]

    def tile_of(i):
        return jnp.maximum(i - MERGE_WSTEPS, 0)

    def wstep_of(i):
        return jnp.minimum(i, MERGE_WSTEPS - 1)

    tile_spec = pl.BlockSpec((TM, D), lambda i: (tile_of(i), 0))
    vec_spec = pl.BlockSpec((1, D), lambda i: (0, 0))
    has_ctx = ctx_branches is not None
    lat_spec = pl.BlockSpec((TM, BW), lambda i: (jnp.minimum(tile_of(i), n_lat_tiles - 1), 0))
    ctx_spec = pl.BlockSpec((TM, BW), lambda i: (jnp.maximum(tile_of(i) - n_lat_tiles, 0), 0))
    in_specs = [tile_spec, vec_spec] + [_mod_spec(layer, k, *tile_info, tile_of=tile_of) for k in (3, 4, 5)]
    in_specs += [lat_spec] * 4
    args = [h, g.reshape(1, D), mod3, mod3, mod3, *branches]
    if has_ctx:
        in_specs += [ctx_spec] * 4
        args += list(ctx_branches)
    w_in, w_br, cbd, sbd, w_out = weights
    gate_blk0 = PROJ_W // W_IN_CH
    in_specs += [
        pl.BlockSpec((None, D, W_IN_CH), lambda i: (layer, 0, gate_blk0 + wstep_of(i))),
        pl.BlockSpec((None, MERGE_BR_CH, D), lambda i: (layer, wstep_of(i), 0)),
        _const_spec(cbd.shape), _const_spec(sbd.shape),
        pl.BlockSpec((None, MERGE_OUT_CH, D), lambda i: (layer, jnp.minimum(i, MERGE_OUT_STEPS - 1), 0)),
    ]
    args += list(weights)
    return pl.pallas_call(
        functools.partial(_merge_kernel, n_lat_tiles=n_lat_tiles if has_ctx else None),
        grid=(MERGE_WSTEPS + n_tiles,),
        in_specs=in_specs,
        out_specs=tile_spec,
        out_shape=jax.ShapeDtypeStruct((n_tiles * TM, D), F32),
        scratch_shapes=[pltpu.VMEM((MERGE_WSTEPS, D, W_IN_CH), BF16),
                        pltpu.VMEM((MERGE_WSTEPS, MERGE_BR_CH, D), BF16),
                        pltpu.VMEM((MERGE_OUT_STEPS, MERGE_OUT_CH, D), BF16)],
        compiler_params=_cparams(("arbitrary",)),
        name="merge_ctx" if has_ctx else "merge",
    )(*args)


def _rope_tables(seq):
    t = np.arange(seq)
    row = (t // GRID_W).astype(np.float64)
    col = (t % GRID_W).astype(np.float64)
    n_freq = HD // 4
    inv = ROPE_BASE ** (-np.arange(n_freq, dtype=np.float64) / n_freq)
    ang = np.concatenate([row[:, None] * inv, col[:, None] * inv], axis=-1)
    cos, sin = np.cos(ang), np.sin(ang)
    cos_h = np.concatenate([cos, cos], axis=1)
    sin_h = np.concatenate([-sin, sin], axis=1)
    cos2 = np.concatenate([np.tile(cos_h, (1, 2)), np.ones((TM, LANES), np.float32)], axis=0)
    sin2 = np.concatenate([np.tile(sin_h, (1, 2)), np.zeros((TM, LANES), np.float32)], axis=0)
    return jnp.asarray(cos2, F32), jnp.asarray(sin2, F32)


def kernel(x, c, ctx, c_ctx, w_ada, b_ada, g_ffn1, ffn1_w13, ffn1_w2, g_mix, w_in, na_bias, wa_sink,
           w_br, w_out, g_ffn2, ffn2_w13, ffn2_w2, g_final):
    n_batch, seq, _ = x.shape
    ctx_len = ctx.shape[1]
    depth = w_ada.shape[0]
    rows = seq // GRID_W
    n_lat = n_batch * seq
    n_lat_tiles = n_lat // TM
    n_all_tiles = (n_lat + n_batch * ctx_len) // TM
    mod_rows = 8
    tile_info = (n_lat_tiles, seq // TM, n_batch, mod_rows)

    cc = jnp.concatenate([c, c_ctx[None], jnp.zeros((mod_rows - n_batch - 1, D), F32)], axis=0)
    mod3 = _ada(cc, w_ada, b_ada).reshape(depth * mod_rows, 1, N_MOD * D)

    cos_t, sin_t = _rope_tables(seq)
    f1, m2, cbd, sbd = (jnp.asarray(t, F32).astype(BF16) for t in _fft_tables())
    dft_c = jnp.asarray(_ctx_dft_table(ctx_len), F32).astype(BF16)

    w13a, w2a = ffn1_w13, ffn1_w2
    w13b, w2b = ffn2_w13, ffn2_w2
    w_in_b = w_in
    merge_w = (w_in, w_br.reshape(depth, N_BRANCH * BW, D), cbd, sbd, w_out)

    h = x.reshape(n_lat, D)
    h_ctx = ctx.reshape(n_batch * ctx_len, D)
    for l in range(depth):
        last = l == depth - 1
        bias_tab = _na_bias_table(na_bias[l], rows)

        h = _ffn(h, n_all_tiles, g_ffn1[l], mod3, l, 0, w13a, w2a, tile_info, h_ctx=h_ctx if l == 0 else None)
        qn, kn, vn, fu, qw, kw, vw = _proj(h, n_all_tiles, g_mix[l], mod3, l, w_in_b, cos_t, sin_t, tile_info)
        a = _na(qn, kn, vn, bias_tab, n_batch, seq, ctx_len)
        w = _wa(wa_sink[l], qw, kw, vw, n_batch, seq, ctx_len)
        zr, zi = _fft(fu, f1, m2, n_batch, seq)
        if last:
            h = _merge(h, n_lat_tiles, g_mix[l], mod3, l, (a, zr, zi, w), None, merge_w, tile_info)
            h = _ffn(h, n_lat_tiles, g_ffn2[l], mod3, l, 6, w13b, w2b, tile_info, g_final=g_final)
        else:
            ctx_br = _ctx_mix(wa_sink[l], qn, kn, vn, fu, qw, kw, vw, dft_c, n_batch, seq, ctx_len)
            ac, wc, zrc, zic = ctx_br
            h = _merge(h, n_all_tiles, g_mix[l], mod3, l, (a, zr, zi, w), (ac, zrc, zic, wc), merge_w, tile_info)
            h = _ffn(h, n_all_tiles, g_ffn2[l], mod3, l, 6, w13b, w2b, tile_info)
    return h.reshape(n_batch, seq, D)
```

```python
import functools

import numpy as np
import jax
import jax.numpy as jnp
from jax import lax
from jax.experimental import pallas as pl
from jax.experimental.pallas import tpu as pltpu

D = 1024
GRID_W = 64
HD = 64
NA_HEADS = 8
NA_WIN_R = 8
NA_WIN_C = 16
FN_GROUPS = 8
FN_GROUP_DIM = 64
WA_HEADS = 8
WA_KV_HEADS = 2
WA_WINDOW = 128
D_FF = 2816
N_MOD = 9
ROPE_BASE = 10000.0
EPS = 1e-6
NEG_INF = -1e30
LOG2E = 1.4426950408889634
BW = 512

LANES = 128
MXU_DIM = 256
TM = 512
TF = MXU_DIM
VMEM_LIMIT = 56 * 1024 * 1024

NA_QR = 4
NA_KR = 12
NA_Q = NA_QR * GRID_W
NA_K = NA_KR * GRID_W
WA_Q = 128
WA_K = 3 * WA_Q
WA_QB = 4

F32 = jnp.float32
BF16 = jnp.bfloat16


def _cparams(sem):
    return pltpu.CompilerParams(dimension_semantics=sem, vmem_limit_bytes=VMEM_LIMIT)


def _const_spec(shape):
    nd = len(shape)
    return pl.BlockSpec(shape, lambda *_: (0,) * nd, pipeline_mode=pl.Buffered(1))


def _sigmoid(x):
    return 1.0 / (1.0 + jnp.exp(-x))


def _norm_mod(x, g, shift, scale):
    y = x * lax.rsqrt(jnp.mean(x * x, axis=-1, keepdims=True) + EPS) * g
    return y * (1.0 + scale) + shift


def _dot(a, b):
    return jnp.dot(a, b, preferred_element_type=F32)


def _dot_nt(a, b):
    return lax.dot_general(a, b, (((1,), (1,)), ((), ())), preferred_element_type=F32)


def _ada_kernel(c_ref, w_ref, b_ref, o_ref):
    x = c_ref[...]
    sx = (x * _sigmoid(x)).astype(BF16)
    o_ref[...] = _dot(sx, w_ref[...].astype(BF16)) + b_ref[...]


def _ada(cc, w_ada, b_ada):
    depth = w_ada.shape[0]
    n = w_ada.shape[2]
    tn = 1536
    rows = cc.shape[0]
    return pl.pallas_call(
        _ada_kernel,
        grid=(depth, n // tn),
        in_specs=[
            pl.BlockSpec((rows, D), lambda l, j: (0, 0)),
            pl.BlockSpec((None, D, tn), lambda l, j: (l, 0, j)),
            pl.BlockSpec((None, 1, tn), lambda l, j: (l, 0, j)),
        ],
        out_specs=pl.BlockSpec((None, rows, tn), lambda l, j: (l, 0, j)),
        out_shape=jax.ShapeDtypeStruct((depth, rows, n), F32),
        compiler_params=_cparams(("arbitrary", "arbitrary")),
        name="ada",
    )(cc, w_ada, b_ada.reshape(depth, 1, n))


def _mod_spec(layer, k, n_lat_tiles, tiles_per_batch, n_batch, mod_rows, tile_of=lambda i: i):
    def idx(i):
        t = tile_of(i)
        row = jnp.where(t < n_lat_tiles, t // tiles_per_batch, n_batch)
        return (layer * mod_rows + row, 0, k)

    return pl.BlockSpec((None, 1, D), idx)


FFN_W13_CH = 2 * TF
FFN_WSTEPS = 2 * D_FF // FFN_W13_CH
FFN_W2_CH = D_FF // FFN_WSTEPS


def _ffn_kernel(*refs, final, split_at):
    refs = list(refs)
    x_ref = refs.pop(0)
    if split_at is not None:
        xc_ref = refs.pop(0)
    g_ref, sh_ref, sc_ref, gt_ref, w13_ref, w2_ref = refs[:6]
    if final:
        gf_ref, o_ref, w13_s, w2_s, hm_ref = refs[6:]
    else:
        o_ref, w13_s, w2_s, hm_ref = refs[6:]
    i = pl.program_id(0)

    @pl.when(i < FFN_WSTEPS)
    def _():
        w13_s[i] = w13_ref[...].astype(BF16)
        w2_s[i] = w2_ref[...].astype(BF16)

    def w13_cols(lo):
        off = lo % FFN_W13_CH
        return w13_s[lo // FFN_W13_CH][:, off:off + TF]

    @pl.when(i >= FFN_WSTEPS)
    def _():
        x = x_ref[...]
        if split_at is not None:
            x = jnp.where(i - FFN_WSTEPS >= split_at, xc_ref[...], x)
        u = _norm_mod(x, g_ref[...], sh_ref[...], sc_ref[...]).astype(BF16)
        for lo in range(0, D_FF, TF):
            a = _dot(u, w13_cols(lo))
            b = _dot(u, w13_cols(D_FF + lo))
            hm_ref[:, lo:lo + TF] = (a * _sigmoid(a) * b).astype(BF16)
        f = _dot(hm_ref[...], w2_s[...].reshape(D_FF, D))
        out = x + 0.5 * gt_ref[...] * f
        if final:
            out = out * lax.rsqrt(jnp.mean(out * out, axis=-1, keepdims=True) + EPS) * gf_ref[...]
        o_ref[...] = out


def _ffn(h, n_tiles, g, mod3, layer, mod_k, w13, w2, tile_info, g_final=None, h_ctx=None):
    final = g_final is not None
    n_lat_tiles = tile_info[0]

    def tile_of(i):
        return jnp.maximum(i - FFN_WSTEPS, 0)

    def wstep_of(i):
        return jnp.minimum(i, FFN_WSTEPS - 1)

    tile_spec = pl.BlockSpec((TM, D), lambda i: (tile_of(i), 0))
    vec_spec = pl.BlockSpec((1, D), lambda i: (0, 0))
    if h_ctx is None:
        in_specs, args = [tile_spec], [h]
    else:
        in_specs = [pl.BlockSpec((TM, D), lambda i: (jnp.minimum(tile_of(i), n_lat_tiles - 1), 0)),
                    pl.BlockSpec((TM, D), lambda i: (jnp.maximum(tile_of(i) - n_lat_tiles, 0), 0))]
        args = [h, h_ctx]
    in_specs += [
        vec_spec,
        _mod_spec(layer, mod_k, *tile_info, tile_of=tile_of),
        _mod_spec(layer, mod_k + 1, *tile_info, tile_of=tile_of),
        _mod_spec(layer, mod_k + 2, *tile_info, tile_of=tile_of),
        pl.BlockSpec((None, D, FFN_W13_CH), lambda i: (layer, 0, wstep_of(i))),
        pl.BlockSpec((None, FFN_W2_CH, D), lambda i: (layer, wstep_of(i), 0)),
    ]
    args += [g.reshape(1, D), mod3, mod3, mod3, w13, w2]
    if final:
        in_specs.append(vec_spec)
        args.append(g_final.reshape(1, D))
    return pl.pallas_call(
        functools.partial(_ffn_kernel, final=final, split_at=None if h_ctx is None else n_lat_tiles),
        grid=(FFN_WSTEPS + n_tiles,),
        in_specs=in_specs,
        out_specs=tile_spec,
        out_shape=jax.ShapeDtypeStruct((n_tiles * TM, D), F32),
        scratch_shapes=[pltpu.VMEM((FFN_WSTEPS, D, FFN_W13_CH), BF16),
                        pltpu.VMEM((FFN_WSTEPS, FFN_W2_CH, D), BF16),
                        pltpu.VMEM((TM, D_FF), BF16)],
        compiler_params=_cparams(("arbitrary",)),
        name="ffn_final" if final else ("ffn" if h_ctx is None else "ffn_split"),
    )(*args)


PROJ_W = 5 * BW + 2 * LANES


def _rope(t, cos, sin_signed, first_half):
    partner = jnp.where(first_half, pltpu.roll(t, LANES - HD // 2, axis=1), pltpu.roll(t, HD // 2, axis=1))
    return t * cos + partner * sin_signed


def _dup_heads(t, lo):
    sw = pltpu.roll(t, HD, axis=1)
    return jnp.where(lo, t, sw), jnp.where(lo, sw, t)


W_IN_CH = MXU_DIM
PROJ_WSTEPS = PROJ_W // W_IN_CH


def _chunk_cols(ws_ref, lo, hi, ch):
    parts = [ws_ref[c] for c in range(lo // ch, hi // ch)]
    return parts[0] if len(parts) == 1 else jnp.concatenate(parts, axis=1)


def _proj_kernel(x_ref, g_ref, sh_ref, sc_ref, w_ref, cos_ref, sin_ref,
                 qn_ref, kn_ref, vn_ref, fu_ref, qw_ref, kw_ref, vw_ref, w_s):
    i = pl.program_id(0)

    @pl.when(i < PROJ_WSTEPS)
    def _():
        w_s[i] = w_ref[...].astype(BF16)

    def w(lo, hi):
        return _chunk_cols(w_s, lo, hi, W_IN_CH)

    @pl.when(i >= PROJ_WSTEPS)
    def _():
        u = _norm_mod(x_ref[...], g_ref[...], sh_ref[...], sc_ref[...]).astype(BF16)
        scale = HD ** -0.5 * LOG2E
        cos = cos_ref[...]
        sin = sin_ref[...]
        lane = lax.broadcasted_iota(jnp.int32, (TM, LANES), 1)
        first_half = (lane & (HD - 1)) < (HD // 2)
        lo = lane < HD
        kv = _dot(u, w(5 * BW, PROJ_W))
        k0, k1 = _dup_heads(_rope(kv[:, :LANES], cos, sin, first_half), lo)
        kw_ref[:, :LANES] = k0.astype(BF16)
        kw_ref[:, LANES:] = k1.astype(BF16)
        v0, v1 = _dup_heads(kv[:, LANES:], lo)
        vw_ref[:, :LANES] = v0.astype(BF16)
        vw_ref[:, LANES:] = v1.astype(BF16)
        wq = _dot(u, w(4 * BW, 5 * BW))
        for j in range(BW // LANES):
            sl = slice(j * LANES, (j + 1) * LANES)
            qw_ref[:, sl] = (_rope(wq[:, sl], cos, sin, first_half) * scale).astype(BF16)
        qn_ref[...] = (_dot(u, w(0, BW)) * scale).astype(BF16)
        kn_ref[...] = _dot(u, w(BW, 2 * BW)).astype(BF16)
        vn_ref[...] = _dot(u, w(2 * BW, 3 * BW)).astype(BF16)
        fu_ref[...] = _dot(u, w(3 * BW, 4 * BW)).astype(BF16)


def _proj(h, n_tiles, g, mod3, layer, w_in, cos_t, sin_t, tile_info):
    n_lat_tiles, tiles_per_batch, _, _ = tile_info

    def tile_of(i):
        return jnp.maximum(i - PROJ_WSTEPS, 0)

    def rope_blk(i):
        t = tile_of(i)
        return jnp.where(t < n_lat_tiles, t % tiles_per_batch, tiles_per_batch)

    tile_spec = pl.BlockSpec((TM, D), lambda i: (tile_of(i), 0))
    vec_spec = pl.BlockSpec((1, D), lambda i: (0, 0))
    rope_spec = pl.BlockSpec((TM, LANES), lambda i: (rope_blk(i), 0))
    w_spec = pl.BlockSpec((None, D, W_IN_CH), lambda i: (layer, 0, jnp.minimum(i, PROJ_WSTEPS - 1)))
    rows = n_tiles * TM

    def out_spec(w):
        return pl.BlockSpec((TM, w), lambda i: (tile_of(i), 0))

    widths = [BW, BW, BW, BW, BW, 2 * LANES, 2 * LANES]
    return pl.pallas_call(
        _proj_kernel,
        grid=(PROJ_WSTEPS + n_tiles,),
        in_specs=[tile_spec, vec_spec,
                  _mod_spec(layer, 3, *tile_info, tile_of=tile_of), _mod_spec(layer, 4, *tile_info, tile_of=tile_of),
                  w_spec, rope_spec, rope_spec],
        out_specs=[out_spec(w) for w in widths],
        out_shape=[jax.ShapeDtypeStruct((rows, w), BF16) for w in widths],
        scratch_shapes=[pltpu.VMEM((PROJ_WSTEPS, D, W_IN_CH), BF16)],
        compiler_params=_cparams(("arbitrary",)),
        name="proj",
    )(h, g.reshape(1, D), mod3, mod3, w_in, cos_t, sin_t)


def _lane_tiles(s):
    return [s[:, j * LANES:(j + 1) * LANES] for j in range(s.shape[1] // LANES)]


def _attend(qs, key_sets, sink_tile=None, n_slabs=1, dead=frozenset()):
    raw = [_dot_nt(qs, k) for k, _, _ in key_sets]
    rows_per_slab = qs.shape[0] // n_slabs
    p_rows = [[] for _ in key_sets]
    dens = []
    for sl in range(n_slabs):
        rows = slice(sl * rows_per_slab, (sl + 1) * rows_per_slab)
        pieces = []
        for ks, (s, (_, _, bias)) in enumerate(zip(raw, key_sets)):
            for t in range(s.shape[1] // LANES):
                if (sl, ks, t) in dead:
                    continue
                cols = slice(t * LANES, (t + 1) * LANES)
                piece = s[rows, cols]
                if bias is not None:
                    piece = piece + bias[rows, cols]
                pieces.append((ks, t, piece))
        tiles = [pc for _, _, pc in pieces]
        if sink_tile is not None:
            tiles.append(sink_tile[rows])
        m = functools.reduce(jnp.maximum, tiles).max(axis=-1, keepdims=True)
        probs = {(ks, t): jnp.exp2(pc - m) for ks, t, pc in pieces}
        psum = functools.reduce(lambda a, b: a + b, list(probs.values()))
        if sink_tile is not None:
            lane = lax.broadcasted_iota(jnp.int32, (rows_per_slab, LANES), 1)
            psum = psum + jnp.where(lane == 0, jnp.exp2(sink_tile[rows] - m), 0.0)
        dens.append(psum.sum(axis=-1, keepdims=True))
        for ks, s in enumerate(raw):
            blocks = [probs[(ks, t)].astype(BF16) if (ks, t) in probs else jnp.zeros((rows_per_slab, LANES), BF16)
                      for t in range(s.shape[1] // LANES)]
            p_rows[ks].append(jnp.concatenate(blocks, axis=1))
    acc = None
    for ks, (_, v, _) in enumerate(key_sets):
        o = _dot(jnp.concatenate(p_rows[ks], axis=0), v)
        acc = o if acc is None else acc + o
    outs = [acc[sl * rows_per_slab:(sl + 1) * rows_per_slab] / dens[sl] for sl in range(n_slabs)]
    return outs[0] if n_slabs == 1 else jnp.concatenate(outs, axis=0)


def _stack_heads(q, rows):
    lane = lax.broadcasted_iota(jnp.int32, (rows, LANES), 1)
    lo = lane < HD
    zero = jnp.zeros_like(q)
    return jnp.concatenate([jnp.where(lo, q, zero), jnp.where(lo, zero, q)], axis=0), lo


def _na_kernel(q_ref, k_ref, v_ref, kc_ref, vc_ref, bias_ref, o_ref, *, rows, interior_dead):
    j = pl.program_id(1)
    krow = jnp.clip(NA_QR * j - NA_WIN_R // 2, 0, rows - NA_KR)
    kstart = pl.multiple_of(krow * GRID_W, GRID_W)

    def body(n_slabs, dead):
        for p in range(NA_HEADS // 2):
            sl = slice(p * LANES, (p + 1) * LANES)
            kb = k_ref[pl.ds(kstart, NA_K), sl]
            vb = v_ref[pl.ds(kstart, NA_K), sl]
            qs, lo = _stack_heads(q_ref[:, sl], NA_Q)
            o = _attend(qs, [(kb, vb, bias_ref[p]), (kc_ref[:, sl], vc_ref[:, sl], None)],
                        n_slabs=n_slabs, dead=dead)
            o_ref[:, sl] = jnp.where(lo, o[:NA_Q], o[NA_Q:]).astype(BF16)

    interior = jnp.logical_and(j > 0, j < rows // NA_QR - 1)

    @pl.when(interior)
    def _():
        body(2 * NA_QR, interior_dead)

    @pl.when(jnp.logical_not(interior))
    def _():
        body(1, frozenset())


def _na(qn, kn, vn, bias_tab, n_batch, seq, ctx_len):
    rows = seq // GRID_W
    n_blk = rows // NA_QR
    n_pair = NA_HEADS // 2
    ctx_blk0 = (n_batch * seq) // ctx_len

    def pat(j):
        return jnp.where(j == 0, 0, jnp.where(j == n_blk - 1, 2, 1))

    interior_dead = frozenset(
        (e * NA_QR + i, 0, t)
        for e in range(2) for i, pairs in enumerate(_na_plan(rows)[1]) for t, pair in enumerate(pairs)
        if pair == (NA_MASKED, NA_MASKED))
    return pl.pallas_call(
        functools.partial(_na_kernel, rows=rows, interior_dead=interior_dead),
        grid=(n_batch, n_blk),
        in_specs=[
            pl.BlockSpec((NA_Q, BW), lambda b, j: (b * n_blk + j, 0)),
            pl.BlockSpec((seq, BW), lambda b, j: (b, 0)),
            pl.BlockSpec((seq, BW), lambda b, j: (b, 0)),
            pl.BlockSpec((ctx_len, BW), lambda b, j: (ctx_blk0 + b, 0)),
            pl.BlockSpec((ctx_len, BW), lambda b, j: (ctx_blk0 + b, 0)),
            pl.BlockSpec((None, n_pair, 2 * NA_Q, NA_K), lambda b, j: (pat(j), 0, 0, 0)),
        ],
        out_specs=pl.BlockSpec((NA_Q, BW), lambda b, j: (b * n_blk + j, 0)),
        out_shape=jax.ShapeDtypeStruct((n_batch * seq, BW), BF16),
        compiler_params=_cparams(("arbitrary", "arbitrary")),
        name="na",
    )(qn, kn, vn, kn, vn, bias_tab)


NA_MASKED = 2 * NA_WIN_R - 1


def _na_plan(rows):
    n_blk = rows // NA_QR
    plan = []
    for blk in (0, 1, n_blk - 1):
        k0 = min(max(NA_QR * blk - NA_WIN_R // 2, 0), rows - NA_KR)
        per_row = []
        for i in range(NA_QR):
            r = NA_QR * blk + i
            rs = min(max(r - NA_WIN_R // 2, 0), rows - NA_WIN_R)
            slabs = [k0 + t - r + NA_WIN_R - 1 if rs <= k0 + t < rs + NA_WIN_R else NA_MASKED for t in range(NA_KR)]
            per_row.append([(slabs[2 * j], slabs[2 * j + 1]) for j in range(NA_KR // 2)])
        plan.append(per_row)
    return plan


def _na_bias_table(bias, rows):
    n_blk = rows // NA_QR
    h = bias.shape[0]
    n_dc = 2 * NA_WIN_C - 1
    qc = np.arange(GRID_W)[:, None]
    kc = np.arange(GRID_W)[None, :]
    ws = np.clip(qc - NA_WIN_C // 2, 0, GRID_W - NA_WIN_C)
    col_ok = (kc >= ws) & (kc < ws + NA_WIN_C)
    dc = np.clip(kc - qc, -(NA_WIN_C - 1), NA_WIN_C - 1) + NA_WIN_C - 1
    onehot = (dc[None] == np.arange(n_dc)[:, None, None]).astype(np.float32)
    toep = jnp.einsum('hrd,dqk->hrqk', bias.astype(F32), jnp.asarray(onehot), precision=lax.Precision.HIGHEST)
    toep = jnp.where(col_ok[None, None], toep * LOG2E, NEG_INF)
    toep = jnp.concatenate([toep, jnp.full((h, 1, GRID_W, GRID_W), NEG_INF, F32)], axis=1)
    toep2 = jnp.concatenate([toep, toep], axis=-1)
    plan = _na_plan(rows)
    return pl.pallas_call(
        functools.partial(_bias_expand_kernel, plan=plan),
        out_shape=jax.ShapeDtypeStruct((len(plan), h // 2, 2 * NA_Q, NA_K), F32),
        compiler_params=pltpu.CompilerParams(vmem_limit_bytes=VMEM_LIMIT),
        name="na_bias_expand",
    )(toep2)


def _bias_expand_kernel(t_ref, o_ref, *, plan):
    lo = lax.broadcasted_iota(jnp.int32, (GRID_W, LANES), 1) < GRID_W
    for pat, per_row in enumerate(plan):
        for hd in range(t_ref.shape[0]):
            for i, pairs in enumerate(per_row):
                r0 = (hd % 2) * NA_Q + i * GRID_W
                for j, (da, db) in enumerate(pairs):
                    o_ref[pat, hd // 2, r0:r0 + GRID_W, j * LANES:(j + 1) * LANES] = (
                        jnp.where(lo, t_ref[hd, da], t_ref[hd, db]))


def _stack_group(q):
    rows = q.shape[0]
    a, lo = _stack_heads(q[:, :LANES], rows)
    b, _ = _stack_heads(q[:, LANES:], rows)
    return jnp.concatenate([a, b], axis=0), lo


def _unstack_group(o, rows, lo):
    oa = jnp.where(lo, o[0:rows], o[rows:2 * rows])
    ob = jnp.where(lo, o[2 * rows:3 * rows], o[3 * rows:4 * rows])
    return jnp.concatenate([oa, ob], axis=1)


def _sink_tile(sink_ref, g, rows):
    gq = WA_HEADS // WA_KV_HEADS
    return jnp.concatenate([jnp.full((rows, LANES), sink_ref[g * gq + h] * LOG2E, F32) for h in range(gq)], axis=0)


def _wa_kernel(sink_ref, q_ref, k_ref, v_ref, kc_ref, vc_ref, o_ref, *, seq):
    for t in range(WA_QB):
        n = pl.program_id(1) * WA_QB + t
        rows = slice(t * WA_Q, (t + 1) * WA_Q)
        kstart = pl.multiple_of(jnp.clip(n * WA_Q - WA_Q, 0, seq - WA_K), WA_Q)
        qpos = n * WA_Q + lax.broadcasted_iota(jnp.int32, (WA_Q, WA_K), 0)
        kpos = kstart + lax.broadcasted_iota(jnp.int32, (WA_Q, WA_K), 1)
        band = jnp.where(jnp.abs(kpos - qpos) <= WA_WINDOW, 0.0, NEG_INF).astype(F32)
        band = jnp.concatenate([band] * (WA_HEADS // WA_KV_HEADS), axis=0)
        for g in range(WA_KV_HEADS):
            sl = slice(g * LANES, (g + 1) * LANES)
            sl2 = slice(g * 2 * LANES, (g + 1) * 2 * LANES)
            kb = k_ref[pl.ds(kstart, WA_K), sl]
            vb = v_ref[pl.ds(kstart, WA_K), sl]
            qs, lo = _stack_group(q_ref[rows, sl2])
            o = _attend(qs, [(kb, vb, band), (kc_ref[:, sl], vc_ref[:, sl], None)], _sink_tile(sink_ref, g, WA_Q))
            o_ref[rows, sl2] = _unstack_group(o, WA_Q, lo).astype(BF16)


def _wa(sink, qw, kw, vw, n_batch, seq, ctx_len):
    n_blk = seq // (WA_Q * WA_QB)
    ctx_blk0 = (n_batch * seq) // ctx_len
    gw = 2 * LANES
    return pl.pallas_call(
        functools.partial(_wa_kernel, seq=seq),
        grid=(n_batch, n_blk),
        in_specs=[
            pl.BlockSpec(memory_space=pltpu.SMEM),
            pl.BlockSpec((WA_Q * WA_QB, BW), lambda b, n: (b * n_blk + n, 0)),
            pl.BlockSpec((seq, gw), lambda b, n: (b, 0)),
            pl.BlockSpec((seq, gw), lambda b, n: (b, 0)),
            pl.BlockSpec((ctx_len, gw), lambda b, n: (ctx_blk0 + b, 0)),
            pl.BlockSpec((ctx_len, gw), lambda b, n: (ctx_blk0 + b, 0)),
        ],
        out_specs=pl.BlockSpec((WA_Q * WA_QB, BW), lambda b, n: (b * n_blk + n, 0)),
        out_shape=jax.ShapeDtypeStruct((n_batch * seq, BW), BF16),
        compiler_params=_cparams(("arbitrary", "arbitrary")),
        name="wa",
    )(sink, qw, kw, vw, kw, vw)


FFT_BLK = 16


def _swap_major(x):
    return jnp.swapaxes(x, 0, 1)


FFT_NBLK = GRID_W // FFT_BLK


def _fft_kernel(f_ref, m_ref, x_ref, zr_ref, zi_ref, ar_s, ai_s, t_scr):
    s = pl.program_id(1)

    @pl.when(s < FFT_NBLK)
    def _():
        xt = _swap_major(x_ref[...].astype(F32)).astype(BF16)
        f = f_ref[...]
        for i in range(FFT_BLK):
            t_scr[i] = _dot(f, xt[i])
        at = _swap_major(t_scr[...])
        ar_s[s] = at[:GRID_W].astype(BF16)
        ai_s[s] = at[GRID_W:].astype(BF16)

    @pl.when(s >= FFT_NBLK)
    def _():
        k0 = (s - FFT_NBLK) * FFT_BLK
        for t in range(FFT_BLK):
            a = jnp.concatenate([ar_s[jb, k0 + t] for jb in range(FFT_NBLK)]
                                + [ai_s[jb, k0 + t] for jb in range(FFT_NBLK)], axis=0)
            t_scr[t] = _dot(m_ref[t], a)
        zt = _swap_major(t_scr[...])
        zr_ref[...] = zt[:GRID_W].astype(BF16)
        zi_ref[...] = zt[GRID_W:].astype(BF16)


def _fft_tables():
    n = GRID_W
    k = np.arange(n)
    ang1 = 2.0 * np.pi * ((k[:, None] * k[None, :]) % n) / n
    f1 = np.concatenate([np.cos(ang1), -np.sin(ang1)], axis=0) / 8.0
    ka = k[:, None, None]
    kb = k[None, :, None]
    n1 = k[None, None, :]
    ang2 = 2.0 * np.pi * ((n1 * (ka + n * kb)) % (n * n)) / (n * n)
    mr, mi = np.cos(ang2), -np.sin(ang2)
    m2 = np.concatenate([np.concatenate([mr, -mi], axis=2), np.concatenate([mi, mr], axis=2)], axis=1) / 8.0
    c = np.arange(FN_GROUP_DIM)
    angc = 2.0 * np.pi * ((c[:, None] * c[None, :]) % FN_GROUP_DIM) / FN_GROUP_DIM
    eye = np.eye(FN_GROUPS)
    cbd = np.kron(eye, np.cos(angc)) / 8.0
    sbd = np.kron(eye, np.sin(angc)) / 8.0
    return f1, m2, cbd, sbd


def _ctx_dft_table(ctx_len):
    k = np.arange(ctx_len)
    ang = 2.0 * np.pi * ((k[:, None] * k[None, :]) % ctx_len) / ctx_len
    return np.concatenate([np.cos(ang), -np.sin(ang)], axis=0) / np.sqrt(ctx_len)


def _fft(fu, f1, m2, n_batch, seq):
    n = GRID_W
    x3 = fu.reshape(-1, n, BW)
    shape3 = jax.ShapeDtypeStruct((n_batch * n, n, BW), BF16)
    z_spec = pl.BlockSpec((n, FFT_BLK, BW), lambda b, s: (b, jnp.maximum(s - FFT_NBLK, 0), 0))
    zr, zi = pl.pallas_call(
        _fft_kernel,
        grid=(n_batch, 2 * FFT_NBLK),
        in_specs=[pl.BlockSpec((2 * n, n), lambda b, s: (0, 0)),
                  pl.BlockSpec((FFT_BLK, 2 * n, 2 * n), lambda b, s: (jnp.maximum(s - FFT_NBLK, 0), 0, 0)),
                  pl.BlockSpec((n, FFT_BLK, BW), lambda b, s: (b, jnp.minimum(s, FFT_NBLK - 1), 0))],
        out_specs=[z_spec, z_spec],
        out_shape=[shape3, shape3],
        scratch_shapes=[pltpu.VMEM((FFT_NBLK, n, FFT_BLK, BW), BF16), pltpu.VMEM((FFT_NBLK, n, FFT_BLK, BW), BF16),
                        pltpu.VMEM((FFT_BLK, 2 * n, BW), F32)],
        compiler_params=_cparams(("arbitrary", "arbitrary")),
        name="fft",
    )(f1, m2, x3)
    return zr.reshape(n_batch * seq, BW), zi.reshape(n_batch * seq, BW)


def _ctx_kernel(sink_ref, qn_ref, kn_ref, vn_ref, fu_ref, qw_ref, kw_ref, vw_ref, dft_ref,
                a_ref, w_ref, zr_ref, zi_ref, *, ctx_len):
    for p in range(NA_HEADS // 2):
        sl = slice(p * LANES, (p + 1) * LANES)
        qs, lo = _stack_heads(qn_ref[:, sl], ctx_len)
        o = _attend(qs, [(kn_ref[:, sl], vn_ref[:, sl], None)])
        a_ref[:, sl] = jnp.where(lo, o[:ctx_len], o[ctx_len:]).astype(BF16)
    for g in range(WA_KV_HEADS):
        sl = slice(g * LANES, (g + 1) * LANES)
        sl2 = slice(g * 2 * LANES, (g + 1) * 2 * LANES)
        qs, lo = _stack_group(qw_ref[:, sl2])
        o = _attend(qs, [(kw_ref[:, sl], vw_ref[:, sl], None)], _sink_tile(sink_ref, g, ctx_len))
        w_ref[:, sl2] = _unstack_group(o, ctx_len, lo).astype(BF16)
    z = _dot(dft_ref[...], fu_ref[...])
    zr_ref[...] = z[:ctx_len].astype(BF16)
    zi_ref[...] = z[ctx_len:].astype(BF16)


def _ctx_mix(sink, qn, kn, vn, fu, qw, kw, vw, dft_c, n_batch, seq, ctx_len):
    blk0 = (n_batch * seq) // ctx_len

    def in_spec(w):
        return pl.BlockSpec((ctx_len, w), lambda b: (blk0 + b, 0))

    out_spec = pl.BlockSpec((ctx_len, BW), lambda b: (b, 0))
    return pl.pallas_call(
        functools.partial(_ctx_kernel, ctx_len=ctx_len),
        grid=(n_batch,),
        in_specs=[pl.BlockSpec(memory_space=pltpu.SMEM),
                  in_spec(BW), in_spec(BW), in_spec(BW), in_spec(BW), in_spec(BW),
                  in_spec(2 * LANES), in_spec(2 * LANES),
                  pl.BlockSpec(dft_c.shape, lambda b: (0, 0))],
        out_specs=[out_spec] * 4,
        out_shape=[jax.ShapeDtypeStruct((n_batch * ctx_len, BW), BF16)] * 4,
        compiler_params=_cparams(("arbitrary",)),
        name="ctx_mix",
    )(sink, qn, kn, vn, fu, qw, kw, vw, dft_c)


N_BRANCH = 3
MERGE_WSTEPS = N_BRANCH * D // W_IN_CH
MERGE_BR_CH = N_BRANCH * BW // MERGE_WSTEPS
MERGE_OUT_CH = MERGE_BR_CH
MERGE_OUT_STEPS = D // MERGE_OUT_CH


def _merge_kernel(x_ref, g_ref, sh_ref, sc_ref, gt_ref, a_ref, zr_ref, zi_ref, w_ref, *rest, n_lat_tiles):
    if n_lat_tiles is None:
        wg_ref, wbr_ref, cbd_ref, sbd_ref, wout_ref, o_ref, wg_s, wbr_s, wout_s = rest
    else:
        (ac_ref, zrc_ref, zic_ref, wc_ref, wg_ref, wbr_ref, cbd_ref, sbd_ref, wout_ref, o_ref,
         wg_s, wbr_s, wout_s) = rest
    i = pl.program_id(0)

    @pl.when(i < MERGE_WSTEPS)
    def _():
        wg_s[i] = wg_ref[...].astype(BF16)
        wbr_s[i] = wbr_ref[...].astype(BF16)

    @pl.when(i < MERGE_OUT_STEPS)
    def _():
        wout_s[i] = wout_ref[...].astype(BF16)

    @pl.when(i >= MERGE_WSTEPS)
    def _():
        if n_lat_tiles is None:
            a, zr, zi, w = a_ref[...], zr_ref[...], zi_ref[...], w_ref[...]
        else:
            is_ctx = i - MERGE_WSTEPS >= n_lat_tiles
            a = jnp.where(is_ctx, ac_ref[...], a_ref[...])
            zr = jnp.where(is_ctx, zrc_ref[...], zr_ref[...])
            zi = jnp.where(is_ctx, zic_ref[...], zi_ref[...])
            w = jnp.where(is_ctx, wc_ref[...], w_ref[...])
        x = x_ref[...]
        u = _norm_mod(x, g_ref[...], sh_ref[...], sc_ref[...]).astype(BF16)
        f = (_dot(zr, cbd_ref[...]) + _dot(zi, sbd_ref[...])).astype(BF16)
        w_br = wbr_s[...].reshape(N_BRANCH * BW, D)
        acc = None
        for b, br in enumerate((a, f, w)):
            gate = _sigmoid(_dot(u, _chunk_cols(wg_s, b * D, (b + 1) * D, W_IN_CH)))
            term = gate * _dot(br, w_br[b * BW:(b + 1) * BW])
            acc = term if acc is None else acc + term
        o_ref[...] = x + gt_ref[...] * _dot(acc.astype(BF16), wout_s[...].reshape(D, D))


def _merge(h, n_tiles, g, mod3, layer, branches, ctx_branches, weights, tile_info):
    n_lat_tiles = tile_info[0]

    def tile_of(i):
        return jnp.maximum(i - MERGE_WSTEPS, 0)

    def wstep_of(i):
        return jnp.minimum(i, MERGE_WSTEPS - 1)

    tile_spec = pl.BlockSpec((TM, D), lambda i: (tile_of(i), 0))
    vec_spec = pl.BlockSpec((1, D), lambda i: (0, 0))
    has_ctx = ctx_branches is not None
    lat_spec = pl.BlockSpec((TM, BW), lambda i: (jnp.minimum(tile_of(i), n_lat_tiles - 1), 0))
    ctx_spec = pl.BlockSpec((TM, BW), lambda i: (jnp.maximum(tile_of(i) - n_lat_tiles, 0), 0))
    in_specs = [tile_spec, vec_spec] + [_mod_spec(layer, k, *tile_info, tile_of=tile_of) for k in (3, 4, 5)]
    in_specs += [lat_spec] * 4
    args = [h, g.reshape(1, D), mod3, mod3, mod3, *branches]
    if has_ctx:
        in_specs += [ctx_spec] * 4
        args += list(ctx_branches)
    w_in, w_br, cbd, sbd, w_out = weights
    gate_blk0 = PROJ_W // W_IN_CH
    in_specs += [
        pl.BlockSpec((None, D, W_IN_CH), lambda i: (layer, 0, gate_blk0 + wstep_of(i))),
        pl.BlockSpec((None, MERGE_BR_CH, D), lambda i: (layer, wstep_of(i), 0)),
        _const_spec(cbd.shape), _const_spec(sbd.shape),
        pl.BlockSpec((None, MERGE_OUT_CH, D), lambda i: (layer, jnp.minimum(i, MERGE_OUT_STEPS - 1), 0)),
    ]
    args += list(weights)
    return pl.pallas_call(
        functools.partial(_merge_kernel, n_lat_tiles=n_lat_tiles if has_ctx else None),
        grid=(MERGE_WSTEPS + n_tiles,),
        in_specs=in_specs,
        out_specs=tile_spec,
        out_shape=jax.ShapeDtypeStruct((n_tiles * TM, D), F32),
        scratch_shapes=[pltpu.VMEM((MERGE_WSTEPS, D, W_IN_CH), BF16),
                        pltpu.VMEM((MERGE_WSTEPS, MERGE_BR_CH, D), BF16),
                        pltpu.VMEM((MERGE_OUT_STEPS, MERGE_OUT_CH, D), BF16)],
        compiler_params=_cparams(("arbitrary",)),
        name="merge_ctx" if has_ctx else "merge",
    )(*args)


def _rope_tables(seq):
    t = np.arange(seq)
    row = (t // GRID_W).astype(np.float64)
    col = (t % GRID_W).astype(np.float64)
    n_freq = HD // 4
    inv = ROPE_BASE ** (-np.arange(n_freq, dtype=np.float64) / n_freq)
    ang = np.concatenate([row[:, None] * inv, col[:, None] * inv], axis=-1)
    cos, sin = np.cos(ang), np.sin(ang)
    cos_h = np.concatenate([cos, cos], axis=1)
    sin_h = np.concatenate([-sin, sin], axis=1)
    cos2 = np.concatenate([np.tile(cos_h, (1, 2)), np.ones((TM, LANES), np.float32)], axis=0)
    sin2 = np.concatenate([np.tile(sin_h, (1, 2)), np.zeros((TM, LANES), np.float32)], axis=0)
    return jnp.asarray(cos2, F32), jnp.asarray(sin2, F32)


def kernel(x, c, ctx, c_ctx, w_ada, b_ada, g_ffn1, ffn1_w13, ffn1_w2, g_mix, w_in, na_bias, wa_sink,
           w_br, w_out, g_ffn2, ffn2_w13, ffn2_w2, g_final):
    n_batch, seq, _ = x.shape
    ctx_len = ctx.shape[1]
    depth = w_ada.shape[0]
    rows = seq // GRID_W
    n_lat = n_batch * seq
    n_lat_tiles = n_lat // TM
    n_all_tiles = (n_lat + n_batch * ctx_len) // TM
    mod_rows = 8
    tile_info = (n_lat_tiles, seq // TM, n_batch, mod_rows)

    cc = jnp.concatenate([c, c_ctx[None], jnp.zeros((mod_rows - n_batch - 1, D), F32)], axis=0)
    mod3 = _ada(cc, w_ada, b_ada).reshape(depth * mod_rows, 1, N_MOD * D)

    cos_t, sin_t = _rope_tables(seq)
    f1, m2, cbd, sbd = (jnp.asarray(t, F32).astype(BF16) for t in _fft_tables())
    dft_c = jnp.asarray(_ctx_dft_table(ctx_len), F32).astype(BF16)

    w13a, w2a = ffn1_w13, ffn1_w2
    w13b, w2b = ffn2_w13, ffn2_w2
    w_in_b = w_in
    merge_w = (w_in, w_br.reshape(depth, N_BRANCH * BW, D), cbd, sbd, w_out)

    h = x.reshape(n_lat, D)
    h_ctx = ctx.reshape(n_batch * ctx_len, D)
    for l in range(depth):
        last = l == depth - 1
        bias_tab = _na_bias_table(na_bias[l], rows)

        h = _ffn(h, n_all_tiles, g_ffn1[l], mod3, l, 0, w13a, w2a, tile_info, h_ctx=h_ctx if l == 0 else None)
        qn, kn, vn, fu, qw, kw, vw = _proj(h, n_all_tiles, g_mix[l], mod3, l, w_in_b, cos_t, sin_t, tile_info)
        a = _na(qn, kn, vn, bias_tab, n_batch, seq, ctx_len)
        w = _wa(wa_sink[l], qw, kw, vw, n_batch, seq, ctx_len)
        zr, zi = _fft(fu, f1, m2, n_batch, seq)
        if last:
            h = _merge(h, n_lat_tiles, g_mix[l], mod3, l, (a, zr, zi, w), None, merge_w, tile_info)
            h = _ffn(h, n_lat_tiles, g_ffn2[l], mod3, l, 6, w13b, w2b, tile_info, g_final=g_final)
        else:
            ctx_br = _ctx_mix(wa_sink[l], qn, kn, vn, fu, qw, kw, vw, dft_c, n_batch, seq, ctx_len)
            ac, wc, zrc, zic = ctx_br
            h = _merge(h, n_all_tiles, g_mix[l], mod3, l, (a, zr, zi, w), (ac, zrc, zic, wc), merge_w, tile_info)
            h = _ffn(h, n_all_tiles, g_ffn2[l], mod3, l, 6, w13b, w2b, tile_info)
    return h.reshape(n_batch, seq, D)
```

```python
import functools

import numpy as np
import jax
import jax.numpy as jnp
from jax import lax
from jax.experimental import pallas as pl
from jax.experimental.pallas import tpu as pltpu

D = 1024
GRID_W = 64
HD = 64
NA_HEADS = 8
NA_WIN_R = 8
NA_WIN_C = 16
FN_GROUPS = 8
FN_GROUP_DIM = 64
WA_HEADS = 8
WA_KV_HEADS = 2
WA_WINDOW = 128
D_FF = 2816
N_MOD = 9
ROPE_BASE = 10000.0
EPS = 1e-6
NEG_INF = -1e30
LOG2E = 1.4426950408889634
BW = 512

LANES = 128
MXU_DIM = 256
TM = 512
TF = MXU_DIM
VMEM_LIMIT = 56 * 1024 * 1024

NA_QR = 4
NA_KR = 12
NA_Q = NA_QR * GRID_W
NA_K = NA_KR * GRID_W
WA_Q = 128
WA_K = 3 * WA_Q
WA_QB = 4

F32 = jnp.float32
BF16 = jnp.bfloat16


def _cparams(sem):
    return pltpu.CompilerParams(dimension_semantics=sem, vmem_limit_bytes=VMEM_LIMIT)


def _const_spec(shape):
    nd = len(shape)
    return pl.BlockSpec(shape, lambda *_: (0,) * nd, pipeline_mode=pl.Buffered(1))


def _sigmoid(x):
    return 1.0 / (1.0 + jnp.exp(-x))


def _norm_mod(x, g, shift, scale):
    y = x * lax.rsqrt(jnp.mean(x * x, axis=-1, keepdims=True) + EPS) * g
    return y * (1.0 + scale) + shift


def _dot(a, b):
    return jnp.dot(a, b, preferred_element_type=F32)


def _dot_nt(a, b):
    return lax.dot_general(a, b, (((1,), (1,)), ((), ())), preferred_element_type=F32)


def _ada_kernel(c_ref, w_ref, b_ref, o_ref):
    x = c_ref[...]
    sx = (x * _sigmoid(x)).astype(BF16)
    o_ref[...] = _dot(sx, w_ref[...].astype(BF16)) + b_ref[...]


def _ada(cc, w_ada, b_ada):
    depth = w_ada.shape[0]
    n = w_ada.shape[2]
    tn = 1536
    rows = cc.shape[0]
    return pl.pallas_call(
        _ada_kernel,
        grid=(depth, n // tn),
        in_specs=[
            pl.BlockSpec((rows, D), lambda l, j: (0, 0)),
            pl.BlockSpec((None, D, tn), lambda l, j: (l, 0, j)),
            pl.BlockSpec((None, 1, tn), lambda l, j: (l, 0, j)),
        ],
        out_specs=pl.BlockSpec((None, rows, tn), lambda l, j: (l, 0, j)),
        out_shape=jax.ShapeDtypeStruct((depth, rows, n), F32),
        compiler_params=_cparams(("arbitrary", "arbitrary")),
        name="ada",
    )(cc, w_ada, b_ada.reshape(depth, 1, n))


def _mod_spec(layer, k, n_lat_tiles, tiles_per_batch, n_batch, mod_rows, tile_of=lambda i: i):
    def idx(i):
        t = tile_of(i)
        row = jnp.where(t < n_lat_tiles, t // tiles_per_batch, n_batch)
        return (layer * mod_rows + row, 0, k)

    return pl.BlockSpec((None, 1, D), idx)


FFN_W13_CH = 2 * TF
FFN_WSTEPS = 2 * D_FF // FFN_W13_CH
FFN_W2_CH = D_FF // FFN_WSTEPS


def _ffn_kernel(*refs, final, split_at):
    refs = list(refs)
    x_ref = refs.pop(0)
    if split_at is not None:
        xc_ref = refs.pop(0)
    g_ref, sh_ref, sc_ref, gt_ref, w13_ref, w2_ref = refs[:6]
    if final:
        gf_ref, o_ref, w13_s, w2_s, hm_ref = refs[6:]
    else:
        o_ref, w13_s, w2_s, hm_ref = refs[6:]
    i = pl.program_id(0)

    @pl.when(i < FFN_WSTEPS)
    def _():
        w13_s[i] = w13_ref[...].astype(BF16)
        w2_s[i] = w2_ref[...].astype(BF16)

    def w13_cols(lo):
        off = lo % FFN_W13_CH
        return w13_s[lo // FFN_W13_CH][:, off:off + TF]

    @pl.when(i >= FFN_WSTEPS)
    def _():
        x = x_ref[...]
        if split_at is not None:
            x = jnp.where(i - FFN_WSTEPS >= split_at, xc_ref[...], x)
        u = _norm_mod(x, g_ref[...], sh_ref[...], sc_ref[...]).astype(BF16)
        for lo in range(0, D_FF, TF):
            a = _dot(u, w13_cols(lo))
            b = _dot(u, w13_cols(D_FF + lo))
            hm_ref[:, lo:lo + TF] = (a * _sigmoid(a) * b).astype(BF16)
        f = _dot(hm_ref[...], w2_s[...].reshape(D_FF, D))
        out = x + 0.5 * gt_ref[...] * f
        if final:
            out = out * lax.rsqrt(jnp.mean(out * out, axis=-1, keepdims=True) + EPS) * gf_ref[...]
        o_ref[...] = out


def _ffn(h, n_tiles, g, mod3, layer, mod_k, w13, w2, tile_info, g_final=None, h_ctx=None):
    final = g_final is not None
    n_lat_tiles = tile_info[0]

    def tile_of(i):
        return jnp.maximum(i - FFN_WSTEPS, 0)

    def wstep_of(i):
        return jnp.minimum(i, FFN_WSTEPS - 1)

    tile_spec = pl.BlockSpec((TM, D), lambda i: (tile_of(i), 0))
    vec_spec = pl.BlockSpec((1, D), lambda i: (0, 0))
    if h_ctx is None:
        in_specs, args = [tile_spec], [h]
    else:
        in_specs = [pl.BlockSpec((TM, D), lambda i: (jnp.minimum(tile_of(i), n_lat_tiles - 1), 0)),
                    pl.BlockSpec((TM, D), lambda i: (jnp.maximum(tile_of(i) - n_lat_tiles, 0), 0))]
        args = [h, h_ctx]
    in_specs += [
        vec_spec,
        _mod_spec(layer, mod_k, *tile_info, tile_of=tile_of),
        _mod_spec(layer, mod_k + 1, *tile_info, tile_of=tile_of),
        _mod_spec(layer, mod_k + 2, *tile_info, tile_of=tile_of),
        pl.BlockSpec((None, D, FFN_W13_CH), lambda i: (layer, 0, wstep_of(i))),
        pl.BlockSpec((None, FFN_W2_CH, D), lambda i: (layer, wstep_of(i), 0)),
    ]
    args += [g.reshape(1, D), mod3, mod3, mod3, w13, w2]
    if final:
        in_specs.append(vec_spec)
        args.append(g_final.reshape(1, D))
    return pl.pallas_call(
        functools.partial(_ffn_kernel, final=final, split_at=None if h_ctx is None else n_lat_tiles),
        grid=(FFN_WSTEPS + n_tiles,),
        in_specs=in_specs,
        out_specs=tile_spec,
        out_shape=jax.ShapeDtypeStruct((n_tiles * TM, D), F32),
        scratch_shapes=[pltpu.VMEM((FFN_WSTEPS, D, FFN_W13_CH), BF16),
                        pltpu.VMEM((FFN_WSTEPS, FFN_W2_CH, D), BF16),
                        pltpu.VMEM((TM, D_FF), BF16)],
        compiler_params=_cparams(("arbitrary",)),
        name="ffn_final" if final else ("ffn" if h_ctx is None else "ffn_split"),
    )(*args)


PROJ_W = 5 * BW + 2 * LANES


def _rope(t, cos, sin_signed, first_half):
    partner = jnp.where(first_half, pltpu.roll(t, LANES - HD // 2, axis=1), pltpu.roll(t, HD // 2, axis=1))
    return t * cos + partner * sin_signed


def _dup_heads(t, lo):
    sw = pltpu.roll(t, HD, axis=1)
    return jnp.where(lo, t, sw), jnp.where(lo, sw, t)


W_IN_CH = MXU_DIM
PROJ_WSTEPS = PROJ_W // W_IN_CH


def _chunk_cols(ws_ref, lo, hi, ch):
    parts = [ws_ref[c] for c in range(lo // ch, hi // ch)]
    return parts[0] if len(parts) == 1 else jnp.concatenate(parts, axis=1)


def _proj_kernel(x_ref, g_ref, sh_ref, sc_ref, w_ref, cos_ref, sin_ref,
                 qn_ref, kn_ref, vn_ref, fu_ref, qw_ref, kw_ref, vw_ref, w_s):
    i = pl.program_id(0)

    @pl.when(i < PROJ_WSTEPS)
    def _():
        w_s[i] = w_ref[...].astype(BF16)

    def w(lo, hi):
        return _chunk_cols(w_s, lo, hi, W_IN_CH)

    @pl.when(i >= PROJ_WSTEPS)
    def _():
        u = _norm_mod(x_ref[...], g_ref[...], sh_ref[...], sc_ref[...]).astype(BF16)
        scale = HD ** -0.5 * LOG2E
        cos = cos_ref[...]
        sin = sin_ref[...]
        lane = lax.broadcasted_iota(jnp.int32, (TM, LANES), 1)
        first_half = (lane & (HD - 1)) < (HD // 2)
        lo = lane < HD
        kv = _dot(u, w(5 * BW, PROJ_W))
        k0, k1 = _dup_heads(_rope(kv[:, :LANES], cos, sin, first_half), lo)
        kw_ref[:, :LANES] = k0.astype(BF16)
        kw_ref[:, LANES:] = k1.astype(BF16)
        v0, v1 = _dup_heads(kv[:, LANES:], lo)
        vw_ref[:, :LANES] = v0.astype(BF16)
        vw_ref[:, LANES:] = v1.astype(BF16)
        wq = _dot(u, w(4 * BW, 5 * BW))
        for j in range(BW // LANES):
            sl = slice(j * LANES, (j + 1) * LANES)
            qw_ref[:, sl] = (_rope(wq[:, sl], cos, sin, first_half) * scale).astype(BF16)
        qn_ref[...] = (_dot(u, w(0, BW)) * scale).astype(BF16)
        kn_ref[...] = _dot(u, w(BW, 2 * BW)).astype(BF16)
        vn_ref[...] = _dot(u, w(2 * BW, 3 * BW)).astype(BF16)
        fu_ref[...] = _dot(u, w(3 * BW, 4 * BW)).astype(BF16)


def _proj(h, n_tiles, g, mod3, layer, w_in, cos_t, sin_t, tile_info):
    n_lat_tiles, tiles_per_batch, _, _ = tile_info

    def tile_of(i):
        return jnp.maximum(i - PROJ_WSTEPS, 0)

    def rope_blk(i):
        t = tile_of(i)
        return jnp.where(t < n_lat_tiles, t % tiles_per_batch, tiles_per_batch)

    tile_spec = pl.BlockSpec((TM, D), lambda i: (tile_of(i), 0))
    vec_spec = pl.BlockSpec((1, D), lambda i: (0, 0))
    rope_spec = pl.BlockSpec((TM, LANES), lambda i: (rope_blk(i), 0))
    w_spec = pl.BlockSpec((None, D, W_IN_CH), lambda i: (layer, 0, jnp.minimum(i, PROJ_WSTEPS - 1)))
    rows = n_tiles * TM

    def out_spec(w):
        return pl.BlockSpec((TM, w), lambda i: (tile_of(i), 0))

    widths = [BW, BW, BW, BW, BW, 2 * LANES, 2 * LANES]
    return pl.pallas_call(
        _proj_kernel,
        grid=(PROJ_WSTEPS + n_tiles,),
        in_specs=[tile_spec, vec_spec,
                  _mod_spec(layer, 3, *tile_info, tile_of=tile_of), _mod_spec(layer, 4, *tile_info, tile_of=tile_of),
                  w_spec, rope_spec, rope_spec],
        out_specs=[out_spec(w) for w in widths],
        out_shape=[jax.ShapeDtypeStruct((rows, w), BF16) for w in widths],
        scratch_shapes=[pltpu.VMEM((PROJ_WSTEPS, D, W_IN_CH), BF16)],
        compiler_params=_cparams(("arbitrary",)),
        name="proj",
    )(h, g.reshape(1, D), mod3, mod3, w_in, cos_t, sin_t)


def _lane_tiles(s):
    return [s[:, j * LANES:(j + 1) * LANES] for j in range(s.shape[1] // LANES)]


def _attend(qs, key_sets, sink_tile=None, n_slabs=1, dead=frozenset()):
    raw = [_dot_nt(qs, k) for k, _, _ in key_sets]
    rows_per_slab = qs.shape[0] // n_slabs
    p_rows = [[] for _ in key_sets]
    sink_terms = []
    for sl in range(n_slabs):
        rows = slice(sl * rows_per_slab, (sl + 1) * rows_per_slab)
        pieces = []
        for ks, (s, (_, _, bias)) in enumerate(zip(raw, key_sets)):
            for t in range(s.shape[1] // LANES):
                if (sl, ks, t) in dead:
                    continue
                cols = slice(t * LANES, (t + 1) * LANES)
                piece = s[rows, cols]
                if bias is not None:
                    piece = piece + bias[rows, cols]
                pieces.append((ks, t, piece))
        tiles = [pc for _, _, pc in pieces]
        if sink_tile is not None:
            tiles.append(sink_tile[rows])
        m = functools.reduce(jnp.maximum, tiles).max(axis=-1, keepdims=True)
        probs = {(ks, t): jnp.exp2(pc - m).astype(BF16) for ks, t, pc in pieces}
        if sink_tile is not None:
            sink_terms.append(jnp.exp2(sink_tile[rows] - m))
        for ks, s in enumerate(raw):
            blocks = [probs.get((ks, t), jnp.zeros((rows_per_slab, LANES), BF16)) for t in range(s.shape[1] // LANES)]
            p_rows[ks].append(jnp.concatenate(blocks, axis=1))
    acc = None
    for ks, (_, v, _) in enumerate(key_sets):
        v_ones = jnp.concatenate([v, jnp.ones_like(v)], axis=1)
        o = _dot(jnp.concatenate(p_rows[ks], axis=0), v_ones)
        acc = o if acc is None else acc + o
    num, den = acc[:, :LANES], acc[:, LANES:]
    if sink_tile is not None:
        den = den + (sink_terms[0] if n_slabs == 1 else jnp.concatenate(sink_terms, axis=0))
    return num / den


def _stack_heads(q, rows):
    lane = lax.broadcasted_iota(jnp.int32, (rows, LANES), 1)
    lo = lane < HD
    zero = jnp.zeros_like(q)
    return jnp.concatenate([jnp.where(lo, q, zero), jnp.where(lo, zero, q)], axis=0), lo


def _na_kernel(q_ref, k_ref, v_ref, kc_ref, vc_ref, bias_ref, o_ref, *, rows, interior_dead):
    j = pl.program_id(1)
    krow = jnp.clip(NA_QR * j - NA_WIN_R // 2, 0, rows - NA_KR)
    kstart = pl.multiple_of(krow * GRID_W, GRID_W)

    def body(n_slabs, dead):
        for p in range(NA_HEADS // 2):
            sl = slice(p * LANES, (p + 1) * LANES)
            kb = k_ref[pl.ds(kstart, NA_K), sl]
            vb = v_ref[pl.ds(kstart, NA_K), sl]
            q = q_ref[:, sl]
            lo = lax.broadcasted_iota(jnp.int32, (NA_Q, LANES), 1) < HD
            outs = []
            for e in range(2):
                qe = jnp.where(lo if e == 0 else jnp.logical_not(lo), q, jnp.zeros_like(q))
                dead_e = frozenset((s - e * n_slabs, ks, t) for s, ks, t in dead if s // n_slabs == e)
                outs.append(_attend(qe, [(kb, vb, bias_ref[p, e * NA_Q:(e + 1) * NA_Q]),
                                         (kc_ref[:, sl], vc_ref[:, sl], None)], n_slabs=n_slabs, dead=dead_e))
            o_ref[:, sl] = jnp.where(lo, outs[0], outs[1]).astype(BF16)

    interior = jnp.logical_and(j > 0, j < rows // NA_QR - 1)

    @pl.when(interior)
    def _():
        body(NA_QR, interior_dead)

    @pl.when(jnp.logical_not(interior))
    def _():
        body(1, frozenset())


def _na(qn, kn, vn, bias_tab, n_batch, seq, ctx_len):
    rows = seq // GRID_W
    n_blk = rows // NA_QR
    n_pair = NA_HEADS // 2
    ctx_blk0 = (n_batch * seq) // ctx_len

    def pat(j):
        return jnp.where(j == 0, 0, jnp.where(j == n_blk - 1, 2, 1))

    interior_dead = frozenset(
        (e * NA_QR + i, 0, t)
        for e in range(2) for i, pairs in enumerate(_na_plan(rows)[1]) for t, pair in enumerate(pairs)
        if pair == (NA_MASKED, NA_MASKED))
    return pl.pallas_call(
        functools.partial(_na_kernel, rows=rows, interior_dead=interior_dead),
        grid=(n_batch, n_blk),
        in_specs=[
            pl.BlockSpec((NA_Q, BW), lambda b, j: (b * n_blk + j, 0)),
            pl.BlockSpec((seq, BW), lambda b, j: (b, 0)),
            pl.BlockSpec((seq, BW), lambda b, j: (b, 0)),
            pl.BlockSpec((ctx_len, BW), lambda b, j: (ctx_blk0 + b, 0)),
            pl.BlockSpec((ctx_len, BW), lambda b, j: (ctx_blk0 + b, 0)),
            pl.BlockSpec((None, n_pair, 2 * NA_Q, NA_K), lambda b, j: (pat(j), 0, 0, 0)),
        ],
        out_specs=pl.BlockSpec((NA_Q, BW), lambda b, j: (b * n_blk + j, 0)),
        out_shape=jax.ShapeDtypeStruct((n_batch * seq, BW), BF16),
        compiler_params=_cparams(("arbitrary", "arbitrary")),
        name="na",
    )(qn, kn, vn, kn, vn, bias_tab)


NA_MASKED = 2 * NA_WIN_R - 1


def _na_plan(rows):
    n_blk = rows // NA_QR
    plan = []
    for blk in (0, 1, n_blk - 1):
        k0 = min(max(NA_QR * blk - NA_WIN_R // 2, 0), rows - NA_KR)
        per_row = []
        for i in range(NA_QR):
            r = NA_QR * blk + i
            rs = min(max(r - NA_WIN_R // 2, 0), rows - NA_WIN_R)
            slabs = [k0 + t - r + NA_WIN_R - 1 if rs <= k0 + t < rs + NA_WIN_R else NA_MASKED for t in range(NA_KR)]
            per_row.append([(slabs[2 * j], slabs[2 * j + 1]) for j in range(NA_KR // 2)])
        plan.append(per_row)
    return plan


def _na_bias_table(bias, rows):
    n_blk = rows // NA_QR
    h = bias.shape[0]
    n_dc = 2 * NA_WIN_C - 1
    qc = np.arange(GRID_W)[:, None]
    kc = np.arange(GRID_W)[None, :]
    ws = np.clip(qc - NA_WIN_C // 2, 0, GRID_W - NA_WIN_C)
    col_ok = (kc >= ws) & (kc < ws + NA_WIN_C)
    dc = np.clip(kc - qc, -(NA_WIN_C - 1), NA_WIN_C - 1) + NA_WIN_C - 1
    onehot = (dc[None] == np.arange(n_dc)[:, None, None]).astype(np.float32)
    toep = jnp.einsum('hrd,dqk->hrqk', bias.astype(F32), jnp.asarray(onehot), precision=lax.Precision.HIGHEST)
    toep = jnp.where(col_ok[None, None], toep * LOG2E, NEG_INF)
    toep = jnp.concatenate([toep, jnp.full((h, 1, GRID_W, GRID_W), NEG_INF, F32)], axis=1)
    toep2 = jnp.concatenate([toep, toep], axis=-1)
    plan = _na_plan(rows)
    return pl.pallas_call(
        functools.partial(_bias_expand_kernel, plan=plan),
        out_shape=jax.ShapeDtypeStruct((len(plan), h // 2, 2 * NA_Q, NA_K), F32),
        compiler_params=pltpu.CompilerParams(vmem_limit_bytes=VMEM_LIMIT),
        name="na_bias_expand",
    )(toep2)


def _bias_expand_kernel(t_ref, o_ref, *, plan):
    lo = lax.broadcasted_iota(jnp.int32, (GRID_W, LANES), 1) < GRID_W
    for pat, per_row in enumerate(plan):
        for hd in range(t_ref.shape[0]):
            for i, pairs in enumerate(per_row):
                r0 = (hd % 2) * NA_Q + i * GRID_W
                for j, (da, db) in enumerate(pairs):
                    o_ref[pat, hd // 2, r0:r0 + GRID_W, j * LANES:(j + 1) * LANES] = (
                        jnp.where(lo, t_ref[hd, da], t_ref[hd, db]))


def _stack_group(q):
    rows = q.shape[0]
    a, lo = _stack_heads(q[:, :LANES], rows)
    b, _ = _stack_heads(q[:, LANES:], rows)
    return jnp.concatenate([a, b], axis=0), lo


def _unstack_group(o, rows, lo):
    oa = jnp.where(lo, o[0:rows], o[rows:2 * rows])
    ob = jnp.where(lo, o[2 * rows:3 * rows], o[3 * rows:4 * rows])
    return jnp.concatenate([oa, ob], axis=1)


def _sink_tile(sink_ref, heads, rows):
    return jnp.concatenate([jnp.full((rows, LANES), sink_ref[h] * LOG2E, F32) for h in heads], axis=0)


def _wa_kernel(sink_ref, q_ref, k_ref, v_ref, kc_ref, vc_ref, o_ref, *, seq):
    gq = WA_HEADS // WA_KV_HEADS
    for t in range(WA_QB):
        n = pl.program_id(1) * WA_QB + t
        rows = slice(t * WA_Q, (t + 1) * WA_Q)
        kstart = pl.multiple_of(jnp.clip(n * WA_Q - WA_Q, 0, seq - WA_K), WA_Q)
        qpos = n * WA_Q + lax.broadcasted_iota(jnp.int32, (WA_Q, WA_K), 0)
        kpos = kstart + lax.broadcasted_iota(jnp.int32, (WA_Q, WA_K), 1)
        band = jnp.where(jnp.abs(kpos - qpos) <= WA_WINDOW, 0.0, NEG_INF).astype(F32)
        band = jnp.concatenate([band] * gq, axis=0)
        for g in range(WA_KV_HEADS):
            sl = slice(g * LANES, (g + 1) * LANES)
            sl2 = slice(g * 2 * LANES, (g + 1) * 2 * LANES)
            kb = k_ref[pl.ds(kstart, WA_K), sl]
            vb = v_ref[pl.ds(kstart, WA_K), sl]
            qs, lo = _stack_group(q_ref[rows, sl2])
            o = _attend(qs, [(kb, vb, band), (kc_ref[:, sl], vc_ref[:, sl], None)],
                        _sink_tile(sink_ref, range(g * gq, (g + 1) * gq), WA_Q))
            o_ref[rows, sl2] = _unstack_group(o, WA_Q, lo).astype(BF16)


def _wa(sink, qw, kw, vw, n_batch, seq, ctx_len):
    n_blk = seq // (WA_Q * WA_QB)
    ctx_blk0 = (n_batch * seq) // ctx_len
    gw = 2 * LANES
    return pl.pallas_call(
        functools.partial(_wa_kernel, seq=seq),
        grid=(n_batch, n_blk),
        in_specs=[
            pl.BlockSpec(memory_space=pltpu.SMEM),
            pl.BlockSpec((WA_Q * WA_QB, BW), lambda b, n: (b * n_blk + n, 0)),
            pl.BlockSpec((seq, gw), lambda b, n: (b, 0)),
            pl.BlockSpec((seq, gw), lambda b, n: (b, 0)),
            pl.BlockSpec((ctx_len, gw), lambda b, n: (ctx_blk0 + b, 0)),
            pl.BlockSpec((ctx_len, gw), lambda b, n: (ctx_blk0 + b, 0)),
        ],
        out_specs=pl.BlockSpec((WA_Q * WA_QB, BW), lambda b, n: (b * n_blk + n, 0)),
        out_shape=jax.ShapeDtypeStruct((n_batch * seq, BW), BF16),
        compiler_params=_cparams(("arbitrary", "arbitrary")),
        name="wa",
    )(sink, qw, kw, vw, kw, vw)


FFT_BLK = 16


def _swap_major(x):
    return jnp.swapaxes(x, 0, 1)


FFT_NBLK = GRID_W // FFT_BLK


def _fft_kernel(f_ref, m_ref, x_ref, zr_ref, zi_ref, ar_s, ai_s, t_scr):
    s = pl.program_id(1)

    @pl.when(s < FFT_NBLK)
    def _():
        xt = _swap_major(x_ref[...].astype(F32)).astype(BF16)
        f = f_ref[...]
        for i in range(FFT_BLK):
            t_scr[i] = _dot(f, xt[i])
        at = _swap_major(t_scr[...])
        ar_s[s] = at[:GRID_W].astype(BF16)
        ai_s[s] = at[GRID_W:].astype(BF16)

    @pl.when(s >= FFT_NBLK)
    def _():
        k0 = (s - FFT_NBLK) * FFT_BLK
        for t in range(FFT_BLK):
            a = jnp.concatenate([ar_s[jb, k0 + t] for jb in range(FFT_NBLK)]
                                + [ai_s[jb, k0 + t] for jb in range(FFT_NBLK)], axis=0)
            t_scr[t] = _dot(m_ref[t], a)
        zt = _swap_major(t_scr[...])
        zr_ref[...] = zt[:GRID_W].astype(BF16)
        zi_ref[...] = zt[GRID_W:].astype(BF16)


def _fft_tables():
    n = GRID_W
    k = np.arange(n)
    ang1 = 2.0 * np.pi * ((k[:, None] * k[None, :]) % n) / n
    f1 = np.concatenate([np.cos(ang1), -np.sin(ang1)], axis=0) / 8.0
    ka = k[:, None, None]
    kb = k[None, :, None]
    n1 = k[None, None, :]
    ang2 = 2.0 * np.pi * ((n1 * (ka + n * kb)) % (n * n)) / (n * n)
    mr, mi = np.cos(ang2), -np.sin(ang2)
    m2 = np.concatenate([np.concatenate([mr, -mi], axis=2), np.concatenate([mi, mr], axis=2)], axis=1) / 8.0
    c = np.arange(FN_GROUP_DIM)
    angc = 2.0 * np.pi * ((c[:, None] * c[None, :]) % FN_GROUP_DIM) / FN_GROUP_DIM
    eye = np.eye(FN_GROUPS)
    cbd = np.kron(eye, np.cos(angc)) / 8.0
    sbd = np.kron(eye, np.sin(angc)) / 8.0
    return f1, m2, cbd, sbd


def _ctx_dft_table(ctx_len):
    k = np.arange(ctx_len)
    ang = 2.0 * np.pi * ((k[:, None] * k[None, :]) % ctx_len) / ctx_len
    return np.concatenate([np.cos(ang), -np.sin(ang)], axis=0) / np.sqrt(ctx_len)


def _fft(fu, f1, m2, n_batch, seq):
    n = GRID_W
    x3 = fu.reshape(-1, n, BW)
    shape3 = jax.ShapeDtypeStruct((n_batch * n, n, BW), BF16)
    z_spec = pl.BlockSpec((n, FFT_BLK, BW), lambda b, s: (b, jnp.maximum(s - FFT_NBLK, 0), 0))
    zr, zi = pl.pallas_call(
        _fft_kernel,
        grid=(n_batch, 2 * FFT_NBLK),
        in_specs=[pl.BlockSpec((2 * n, n), lambda b, s: (0, 0)),
                  pl.BlockSpec((FFT_BLK, 2 * n, 2 * n), lambda b, s: (jnp.maximum(s - FFT_NBLK, 0), 0, 0)),
                  pl.BlockSpec((n, FFT_BLK, BW), lambda b, s: (b, jnp.minimum(s, FFT_NBLK - 1), 0))],
        out_specs=[z_spec, z_spec],
        out_shape=[shape3, shape3],
        scratch_shapes=[pltpu.VMEM((FFT_NBLK, n, FFT_BLK, BW), BF16), pltpu.VMEM((FFT_NBLK, n, FFT_BLK, BW), BF16),
                        pltpu.VMEM((FFT_BLK, 2 * n, BW), F32)],
        compiler_params=_cparams(("arbitrary", "arbitrary")),
        name="fft",
    )(f1, m2, x3)
    return zr.reshape(n_batch * seq, BW), zi.reshape(n_batch * seq, BW)


def _ctx_kernel(sink_ref, qn_ref, kn_ref, vn_ref, fu_ref, qw_ref, kw_ref, vw_ref, dft_ref,
                a_ref, w_ref, zr_ref, zi_ref, *, ctx_len):
    for p in range(NA_HEADS // 2):
        sl = slice(p * LANES, (p + 1) * LANES)
        qs, lo = _stack_heads(qn_ref[:, sl], ctx_len)
        o = _attend(qs, [(kn_ref[:, sl], vn_ref[:, sl], None)])
        a_ref[:, sl] = jnp.where(lo, o[:ctx_len], o[ctx_len:]).astype(BF16)
    for g in range(WA_KV_HEADS):
        sl = slice(g * LANES, (g + 1) * LANES)
        sl2 = slice(g * 2 * LANES, (g + 1) * 2 * LANES)
        qs, lo = _stack_group(qw_ref[:, sl2])
        gq = WA_HEADS // WA_KV_HEADS
        o = _attend(qs, [(kw_ref[:, sl], vw_ref[:, sl], None)],
                    _sink_tile(sink_ref, range(g * gq, (g + 1) * gq), ctx_len))
        w_ref[:, sl2] = _unstack_group(o, ctx_len, lo).astype(BF16)
    z = _dot(dft_ref[...], fu_ref[...])
    zr_ref[...] = z[:ctx_len].astype(BF16)
    zi_ref[...] = z[ctx_len:].astype(BF16)


def _ctx_mix(sink, qn, kn, vn, fu, qw, kw, vw, dft_c, n_batch, seq, ctx_len):
    blk0 = (n_batch * seq) // ctx_len

    def in_spec(w):
        return pl.BlockSpec((ctx_len, w), lambda b: (blk0 + b, 0))

    out_spec = pl.BlockSpec((ctx_len, BW), lambda b: (b, 0))
    return pl.pallas_call(
        functools.partial(_ctx_kernel, ctx_len=ctx_len),
        grid=(n_batch,),
        in_specs=[pl.BlockSpec(memory_space=pltpu.SMEM),
                  in_spec(BW), in_spec(BW), in_spec(BW), in_spec(BW), in_spec(BW),
                  in_spec(2 * LANES), in_spec(2 * LANES),
                  pl.BlockSpec(dft_c.shape, lambda b: (0, 0))],
        out_specs=[out_spec] * 4,
        out_shape=[jax.ShapeDtypeStruct((n_batch * ctx_len, BW), BF16)] * 4,
        compiler_params=_cparams(("arbitrary",)),
        name="ctx_mix",
    )(sink, qn, kn, vn, fu, qw, kw, vw, dft_c)


N_BRANCH = 3
MERGE_WSTEPS = N_BRANCH * D // W_IN_CH
MERGE_BR_CH = N_BRANCH * BW // MERGE_WSTEPS
MERGE_OUT_CH = MERGE_BR_CH
MERGE_OUT_STEPS = D // MERGE_OUT_CH


def _merge_kernel(x_ref, g_ref, sh_ref, sc_ref, gt_ref, a_ref, zr_ref, zi_ref, w_ref, *rest, n_lat_tiles):
    if n_lat_tiles is None:
        wg_ref, wbr_ref, cbd_ref, sbd_ref, wout_ref, o_ref, wg_s, wbr_s, wout_s = rest
    else:
        (ac_ref, zrc_ref, zic_ref, wc_ref, wg_ref, wbr_ref, cbd_ref, sbd_ref, wout_ref, o_ref,
         wg_s, wbr_s, wout_s) = rest
    i = pl.program_id(0)

    @pl.when(i < MERGE_WSTEPS)
    def _():
        wg_s[i] = wg_ref[...].astype(BF16)
        wbr_s[i] = wbr_ref[...].astype(BF16)

    @pl.when(i < MERGE_OUT_STEPS)
    def _():
        wout_s[i] = wout_ref[...].astype(BF16)

    @pl.when(i >= MERGE_WSTEPS)
    def _():
        if n_lat_tiles is None:
            a, zr, zi, w = a_ref[...], zr_ref[...], zi_ref[...], w_ref[...]
        else:
            is_ctx = i - MERGE_WSTEPS >= n_lat_tiles
            a = jnp.where(is_ctx, ac_ref[...], a_ref[...])
            zr = jnp.where(is_ctx, zrc_ref[...], zr_ref[...])
            zi = jnp.where(is_ctx, zic_ref[...], zi_ref[...])
            w = jnp.where(is_ctx, wc_ref[...], w_ref[...])
        x = x_ref[...]
        u = _norm_mod(x, g_ref[...], sh_ref[...], sc_ref[...]).astype(BF16)
        f = (_dot(zr, cbd_ref[...]) + _dot(zi, sbd_ref[...])).astype(BF16)
        w_br = wbr_s[...].reshape(N_BRANCH * BW, D)
        acc = None
        for b, br in enumerate((a, f, w)):
            gate = _sigmoid(_dot(u, _chunk_cols(wg_s, b * D, (b + 1) * D, W_IN_CH)))
            term = gate * _dot(br, w_br[b * BW:(b + 1) * BW])
            acc = term if acc is None else acc + term
        o_ref[...] = x + gt_ref[...] * _dot(acc.astype(BF16), wout_s[...].reshape(D, D))


def _merge(h, n_tiles, g, mod3, layer, branches, ctx_branches, weights, tile_info):
    n_lat_tiles = tile_info[0]

    def tile_of(i):
        return jnp.maximum(i - MERGE_WSTEPS, 0)

    def wstep_of(i):
        return jnp.minimum(i, MERGE_WSTEPS - 1)

    tile_spec = pl.BlockSpec((TM, D), lambda i: (tile_of(i), 0))
    vec_spec = pl.BlockSpec((1, D), lambda i: (0, 0))
    has_ctx = ctx_branches is not None
    lat_spec = pl.BlockSpec((TM, BW), lambda i: (jnp.minimum(tile_of(i), n_lat_tiles - 1), 0))
    ctx_spec = pl.BlockSpec((TM, BW), lambda i: (jnp.maximum(tile_of(i) - n_lat_tiles, 0), 0))
    in_specs = [tile_spec, vec_spec] + [_mod_spec(layer, k, *tile_info, tile_of=tile_of) for k in (3, 4, 5)]
    in_specs += [lat_spec] * 4
    args = [h, g.reshape(1, D), mod3, mod3, mod3, *branches]
    if has_ctx:
        in_specs += [ctx_spec] * 4
        args += list(ctx_branches)
    w_in, w_br, cbd, sbd, w_out = weights
    gate_blk0 = PROJ_W // W_IN_CH
    in_specs += [
        pl.BlockSpec((None, D, W_IN_CH), lambda i: (layer, 0, gate_blk0 + wstep_of(i))),
        pl.BlockSpec((None, MERGE_BR_CH, D), lambda i: (layer, wstep_of(i), 0)),
        _const_spec(cbd.shape), _const_spec(sbd.shape),
        pl.BlockSpec((None, MERGE_OUT_CH, D), lambda i: (layer, jnp.minimum(i, MERGE_OUT_STEPS - 1), 0)),
    ]
    args += list(weights)
    return pl.pallas_call(
        functools.partial(_merge_kernel, n_lat_tiles=n_lat_tiles if has_ctx else None),
        grid=(MERGE_WSTEPS + n_tiles,),
        in_specs=in_specs,
        out_specs=tile_spec,
        out_shape=jax.ShapeDtypeStruct((n_tiles * TM, D), F32),
        scratch_shapes=[pltpu.VMEM((MERGE_WSTEPS, D, W_IN_CH), BF16),
                        pltpu.VMEM((MERGE_WSTEPS, MERGE_BR_CH, D), BF16),
                        pltpu.VMEM((MERGE_OUT_STEPS, MERGE_OUT_CH, D), BF16)],
        compiler_params=_cparams(("arbitrary",)),
        name="merge_ctx" if has_ctx else "merge",
    )(*args)


def _rope_tables(seq):
    t = np.arange(seq)
    row = (t // GRID_W).astype(np.float64)
    col = (t % GRID_W).astype(np.float64)
    n_freq = HD // 4
    inv = ROPE_BASE ** (-np.arange(n_freq, dtype=np.float64) / n_freq)
    ang = np.concatenate([row[:, None] * inv, col[:, None] * inv], axis=-1)
    cos, sin = np.cos(ang), np.sin(ang)
    cos_h = np.concatenate([cos, cos], axis=1)
    sin_h = np.concatenate([-sin, sin], axis=1)
    cos2 = np.concatenate([np.tile(cos_h, (1, 2)), np.ones((TM, LANES), np.float32)], axis=0)
    sin2 = np.concatenate([np.tile(sin_h, (1, 2)), np.zeros((TM, LANES), np.float32)], axis=0)
    return jnp.asarray(cos2, F32), jnp.asarray(sin2, F32)


def kernel(x, c, ctx, c_ctx, w_ada, b_ada, g_ffn1, ffn1_w13, ffn1_w2, g_mix, w_in, na_bias, wa_sink,
           w_br, w_out, g_ffn2, ffn2_w13, ffn2_w2, g_final):
    n_batch, seq, _ = x.shape
    ctx_len = ctx.shape[1]
    depth = w_ada.shape[0]
    rows = seq // GRID_W
    n_lat = n_batch * seq
    n_lat_tiles = n_lat // TM
    n_all_tiles = (n_lat + n_batch * ctx_len) // TM
    mod_rows = 8
    tile_info = (n_lat_tiles, seq // TM, n_batch, mod_rows)

    cc = jnp.concatenate([c, c_ctx[None], jnp.zeros((mod_rows - n_batch - 1, D), F32)], axis=0)
    mod3 = _ada(cc, w_ada, b_ada).reshape(depth * mod_rows, 1, N_MOD * D)

    cos_t, sin_t = _rope_tables(seq)
    f1, m2, cbd, sbd = (jnp.asarray(t, F32).astype(BF16) for t in _fft_tables())
    dft_c = jnp.asarray(_ctx_dft_table(ctx_len), F32).astype(BF16)

    w13a, w2a = ffn1_w13, ffn1_w2
    w13b, w2b = ffn2_w13, ffn2_w2
    w_in_b = w_in
    merge_w = (w_in, w_br.reshape(depth, N_BRANCH * BW, D), cbd, sbd, w_out)

    h = x.reshape(n_lat, D)
    h_ctx = ctx.reshape(n_batch * ctx_len, D)
    for l in range(depth):
        last = l == depth - 1
        bias_tab = _na_bias_table(na_bias[l], rows)

        h = _ffn(h, n_all_tiles, g_ffn1[l], mod3, l, 0, w13a, w2a, tile_info, h_ctx=h_ctx if l == 0 else None)
        qn, kn, vn, fu, qw, kw, vw = _proj(h, n_all_tiles, g_mix[l], mod3, l, w_in_b, cos_t, sin_t, tile_info)
        a = _na(qn, kn, vn, bias_tab, n_batch, seq, ctx_len)
        w = _wa(wa_sink[l], qw, kw, vw, n_batch, seq, ctx_len)
        zr, zi = _fft(fu, f1, m2, n_batch, seq)
        if last:
            h = _merge(h, n_lat_tiles, g_mix[l], mod3, l, (a, zr, zi, w), None, merge_w, tile_info)
            h = _ffn(h, n_lat_tiles, g_ffn2[l], mod3, l, 6, w13b, w2b, tile_info, g_final=g_final)
        else:
            ctx_br = _ctx_mix(wa_sink[l], qn, kn, vn, fu, qw, kw, vw, dft_c, n_batch, seq, ctx_len)
            ac, wc, zrc, zic = ctx_br
            h = _merge(h, n_all_tiles, g_mix[l], mod3, l, (a, zr, zi, w), (ac, zrc, zic, wc), merge_w, tile_info)
            h = _ffn(h, n_all_tiles, g_ffn2[l], mod3, l, 6, w13b, w2b, tile_info)
    return h.reshape(n_batch, seq, D)
```

```python
import functools

import numpy as np
import jax
import jax.numpy as jnp
from jax import lax
from jax.experimental import pallas as pl
from jax.experimental.pallas import tpu as pltpu

D = 1024
GRID_W = 64
HD = 64
NA_HEADS = 8
NA_WIN_R = 8
NA_WIN_C = 16
FN_GROUPS = 8
FN_GROUP_DIM = 64
WA_HEADS = 8
WA_KV_HEADS = 2
WA_WINDOW = 128
D_FF = 2816
N_MOD = 9
ROPE_BASE = 10000.0
EPS = 1e-6
NEG_INF = -1e30
LOG2E = 1.4426950408889634
BW = 512

LANES = 128
MXU_DIM = 256
TM = 512
TF = MXU_DIM
VMEM_LIMIT = 56 * 1024 * 1024

NA_QR = 4
NA_KR = 12
NA_RB = 2
NA_Q = NA_QR * GRID_W
NA_K = NA_KR * GRID_W
WA_Q = 128
WA_K = 3 * WA_Q
WA_QB = 4

F32 = jnp.float32
BF16 = jnp.bfloat16


def _cparams(sem):
    return pltpu.CompilerParams(dimension_semantics=sem, vmem_limit_bytes=VMEM_LIMIT)


def _const_spec(shape):
    nd = len(shape)
    return pl.BlockSpec(shape, lambda *_: (0,) * nd, pipeline_mode=pl.Buffered(1))


def _sigmoid(x):
    return 1.0 / (1.0 + jnp.exp(-x))


def _norm_mod(x, g, shift, scale):
    y = x * lax.rsqrt(jnp.mean(x * x, axis=-1, keepdims=True) + EPS)
    return y * (g * (1.0 + scale)) + shift


def _dot(a, b):
    return jnp.dot(a, b, preferred_element_type=F32)


def _dot_nt(a, b):
    return lax.dot_general(a, b, (((1,), (1,)), ((), ())), preferred_element_type=F32)


def _ada_kernel(c_ref, w_ref, b_ref, o_ref):
    x = c_ref[...]
    sx = (x * _sigmoid(x)).astype(BF16)
    o_ref[...] = _dot(sx, w_ref[...].astype(BF16)) + b_ref[...]


def _ada(cc, w_ada, b_ada):
    depth = w_ada.shape[0]
    n = w_ada.shape[2]
    tn = 1536
    rows = cc.shape[0]
    return pl.pallas_call(
        _ada_kernel,
        grid=(depth, n // tn),
        in_specs=[
            pl.BlockSpec((rows, D), lambda l, j: (0, 0)),
            pl.BlockSpec((None, D, tn), lambda l, j: (l, 0, j)),
            pl.BlockSpec((None, 1, tn), lambda l, j: (l, 0, j)),
        ],
        out_specs=pl.BlockSpec((None, rows, tn), lambda l, j: (l, 0, j)),
        out_shape=jax.ShapeDtypeStruct((depth, rows, n), F32),
        compiler_params=_cparams(("arbitrary", "arbitrary")),
        name="ada",
    )(cc, w_ada, b_ada.reshape(depth, 1, n))


def _mod_spec(layer, k, n_lat_tiles, tiles_per_batch, n_batch, mod_rows, tile_of=lambda i: i):
    def idx(i):
        t = tile_of(i)
        row = jnp.where(t < n_lat_tiles, t // tiles_per_batch, n_batch)
        return (layer * mod_rows + row, 0, k)

    return pl.BlockSpec((None, 1, D), idx)


FFN_W13_CH = 2 * TF
FFN_WSTEPS = 2 * D_FF // FFN_W13_CH
FFN_W2_CH = D_FF // FFN_WSTEPS


def _ffn_kernel(*refs, final, split_at):
    refs = list(refs)
    x_ref = refs.pop(0)
    if split_at is not None:
        xc_ref = refs.pop(0)
    g_ref, sh_ref, sc_ref, gt_ref, w13_ref, w2_ref = refs[:6]
    if final:
        gf_ref, o_ref, w13_s, w2_s, hm_ref = refs[6:]
    else:
        o_ref, w13_s, w2_s, hm_ref = refs[6:]
    i = pl.program_id(0)

    @pl.when(i < FFN_WSTEPS)
    def _():
        w13_s[i] = w13_ref[...].astype(BF16)
        w2_s[i] = w2_ref[...].astype(BF16)

    def w13_cols(lo):
        off = lo % FFN_W13_CH
        return w13_s[lo // FFN_W13_CH][:, off:off + TF]

    @pl.when(i >= FFN_WSTEPS)
    def _():
        x = x_ref[...]
        if split_at is not None:
            x = jnp.where(i - FFN_WSTEPS >= split_at, xc_ref[...], x)
        u = _norm_mod(x, g_ref[...], sh_ref[...], sc_ref[...]).astype(BF16)
        for lo in range(0, D_FF, TF):
            a = _dot(u, w13_cols(lo))
            b = _dot(u, w13_cols(D_FF + lo))
            hm_ref[:, lo:lo + TF] = (a * _sigmoid(a) * b).astype(BF16)
        f = _dot(hm_ref[...], w2_s[...].reshape(D_FF, D))
        out = x + 0.5 * gt_ref[...] * f
        if final:
            out = out * lax.rsqrt(jnp.mean(out * out, axis=-1, keepdims=True) + EPS) * gf_ref[...]
        o_ref[...] = out


def _ffn(h, n_tiles, g, mod3, layer, mod_k, w13, w2, tile_info, g_final=None, h_ctx=None):
    final = g_final is not None
    n_lat_tiles = tile_info[0]

    def tile_of(i):
        return jnp.maximum(i - FFN_WSTEPS, 0)

    def wstep_of(i):
        return jnp.minimum(i, FFN_WSTEPS - 1)

    tile_spec = pl.BlockSpec((TM, D), lambda i: (tile_of(i), 0))
    vec_spec = pl.BlockSpec((1, D), lambda i: (0, 0))
    if h_ctx is None:
        in_specs, args = [tile_spec], [h]
    else:
        in_specs = [pl.BlockSpec((TM, D), lambda i: (jnp.minimum(tile_of(i), n_lat_tiles - 1), 0)),
                    pl.BlockSpec((TM, D), lambda i: (jnp.maximum(tile_of(i) - n_lat_tiles, 0), 0))]
        args = [h, h_ctx]
    in_specs += [
        vec_spec,
        _mod_spec(layer, mod_k, *tile_info, tile_of=tile_of),
        _mod_spec(layer, mod_k + 1, *tile_info, tile_of=tile_of),
        _mod_spec(layer, mod_k + 2, *tile_info, tile_of=tile_of),
        pl.BlockSpec((None, D, FFN_W13_CH), lambda i: (layer, 0, wstep_of(i))),
        pl.BlockSpec((None, FFN_W2_CH, D), lambda i: (layer, wstep_of(i), 0)),
    ]
    args += [g.reshape(1, D), mod3, mod3, mod3, w13, w2]
    if final:
        in_specs.append(vec_spec)
        args.append(g_final.reshape(1, D))
    return pl.pallas_call(
        functools.partial(_ffn_kernel, final=final, split_at=None if h_ctx is None else n_lat_tiles),
        grid=(FFN_WSTEPS + n_tiles,),
        in_specs=in_specs,
        out_specs=tile_spec,
        out_shape=jax.ShapeDtypeStruct((n_tiles * TM, D), F32),
        scratch_shapes=[pltpu.VMEM((FFN_WSTEPS, D, FFN_W13_CH), BF16),
                        pltpu.VMEM((FFN_WSTEPS, FFN_W2_CH, D), BF16),
                        pltpu.VMEM((TM, D_FF), BF16)],
        compiler_params=_cparams(("arbitrary",)),
        name="ffn_final" if final else ("ffn" if h_ctx is None else "ffn_split"),
    )(*args)


PROJ_W = 5 * BW + 2 * LANES


def _rope(t, cos, sin_signed, first_half):
    partner = jnp.where(first_half, pltpu.roll(t, LANES - HD // 2, axis=1), pltpu.roll(t, HD // 2, axis=1))
    return t * cos + partner * sin_signed


def _dup_heads(t, lo):
    sw = pltpu.roll(t, HD, axis=1)
    return jnp.where(lo, t, sw), jnp.where(lo, sw, t)


W_IN_CH = MXU_DIM
PROJ_WSTEPS = PROJ_W // W_IN_CH


def _chunk_cols(ws_ref, lo, hi, ch):
    parts = [ws_ref[c] for c in range(lo // ch, hi // ch)]
    return parts[0] if len(parts) == 1 else jnp.concatenate(parts, axis=1)


def _proj_kernel(x_ref, g_ref, sh_ref, sc_ref, w_ref, cos_ref, sin_ref,
                 qn_ref, kn_ref, vn_ref, fu_ref, qw_ref, kw_ref, vw_ref, w_s):
    i = pl.program_id(0)

    @pl.when(i < PROJ_WSTEPS)
    def _():
        w_s[i] = w_ref[...].astype(BF16)

    def w(lo, hi):
        return _chunk_cols(w_s, lo, hi, W_IN_CH)

    @pl.when(i >= PROJ_WSTEPS)
    def _():
        u = _norm_mod(x_ref[...], g_ref[...], sh_ref[...], sc_ref[...]).astype(BF16)
        scale = HD ** -0.5 * LOG2E
        cos = cos_ref[...]
        sin = sin_ref[...]
        lane = lax.broadcasted_iota(jnp.int32, (TM, LANES), 1)
        first_half = (lane & (HD - 1)) < (HD // 2)
        lo = lane < HD
        kv = _dot(u, w(5 * BW, PROJ_W))
        k0, k1 = _dup_heads(_rope(kv[:, :LANES], cos, sin, first_half), lo)
        kw_ref[:, :LANES] = k0.astype(BF16)
        kw_ref[:, LANES:] = k1.astype(BF16)
        v0, v1 = _dup_heads(kv[:, LANES:], lo)
        vw_ref[:, :LANES] = v0.astype(BF16)
        vw_ref[:, LANES:] = v1.astype(BF16)
        wq = _dot(u, w(4 * BW, 5 * BW))
        for j in range(BW // LANES):
            sl = slice(j * LANES, (j + 1) * LANES)
            qw_ref[:, sl] = (_rope(wq[:, sl], cos, sin, first_half) * scale).astype(BF16)
        qn_ref[...] = (_dot(u, w(0, BW)) * scale).astype(BF16)
        kn_ref[...] = _dot(u, w(BW, 2 * BW)).astype(BF16)
        vn_ref[...] = _dot(u, w(2 * BW, 3 * BW)).astype(BF16)
        fu_ref[...] = _dot(u, w(3 * BW, 4 * BW)).astype(BF16)


def _proj(h, n_tiles, g, mod3, layer, w_in, cos_t, sin_t, tile_info):
    n_lat_tiles, tiles_per_batch, _, _ = tile_info

    def tile_of(i):
        return jnp.maximum(i - PROJ_WSTEPS, 0)

    def rope_blk(i):
        t = tile_of(i)
        return jnp.where(t < n_lat_tiles, t % tiles_per_batch, tiles_per_batch)

    tile_spec = pl.BlockSpec((TM, D), lambda i: (tile_of(i), 0))
    vec_spec = pl.BlockSpec((1, D), lambda i: (0, 0))
    rope_spec = pl.BlockSpec((TM, LANES), lambda i: (rope_blk(i), 0))
    w_spec = pl.BlockSpec((None, D, W_IN_CH), lambda i: (layer, 0, jnp.minimum(i, PROJ_WSTEPS - 1)))
    rows = n_tiles * TM

    def out_spec(w):
        return pl.BlockSpec((TM, w), lambda i: (tile_of(i), 0))

    widths = [BW, BW, BW, BW, BW, 2 * LANES, 2 * LANES]
    return pl.pallas_call(
        _proj_kernel,
        grid=(PROJ_WSTEPS + n_tiles,),
        in_specs=[tile_spec, vec_spec,
                  _mod_spec(layer, 3, *tile_info, tile_of=tile_of), _mod_spec(layer, 4, *tile_info, tile_of=tile_of),
                  w_spec, rope_spec, rope_spec],
        out_specs=[out_spec(w) for w in widths],
        out_shape=[jax.ShapeDtypeStruct((rows, w), BF16) for w in widths],
        scratch_shapes=[pltpu.VMEM((PROJ_WSTEPS, D, W_IN_CH), BF16)],
        compiler_params=_cparams(("arbitrary",)),
        name="proj",
    )(h, g.reshape(1, D), mod3, mod3, w_in, cos_t, sin_t)


def _lane_tiles(s):
    return [s[:, j * LANES:(j + 1) * LANES] for j in range(s.shape[1] // LANES)]


def _attend(qs, key_sets, sink_tile=None, n_slabs=1, dead=frozenset()):
    raw = [_dot_nt(qs, k) for k, _, _ in key_sets]
    rows_per_slab = qs.shape[0] // n_slabs
    p_rows = [[] for _ in key_sets]
    sink_terms = []
    for sl in range(n_slabs):
        rows = slice(sl * rows_per_slab, (sl + 1) * rows_per_slab)
        pieces = []
        for ks, (s, (_, _, bias)) in enumerate(zip(raw, key_sets)):
            for t in range(s.shape[1] // LANES):
                if (sl, ks, t) in dead:
                    continue
                cols = slice(t * LANES, (t + 1) * LANES)
                piece = s[rows, cols]
                if bias is not None:
                    piece = piece + bias[rows, cols]
                pieces.append((ks, t, piece))
        tiles = [pc for _, _, pc in pieces]
        if sink_tile is not None:
            tiles.append(sink_tile[rows])
        m = functools.reduce(jnp.maximum, tiles).max(axis=-1, keepdims=True)
        probs = {(ks, t): jnp.exp2(pc - m).astype(BF16) for ks, t, pc in pieces}
        if sink_tile is not None:
            sink_terms.append(jnp.exp2(sink_tile[rows] - m))
        for ks, s in enumerate(raw):
            blocks = [probs.get((ks, t), jnp.zeros((rows_per_slab, LANES), BF16)) for t in range(s.shape[1] // LANES)]
            p_rows[ks].append(jnp.concatenate(blocks, axis=1))
    acc = None
    for ks, (_, v, _) in enumerate(key_sets):
        v_ones = jnp.concatenate([v, jnp.ones_like(v)], axis=1)
        o = _dot(jnp.concatenate(p_rows[ks], axis=0), v_ones)
        acc = o if acc is None else acc + o
    num, den = acc[:, :LANES], acc[:, LANES:]
    if sink_tile is not None:
        den = den + (sink_terms[0] if n_slabs == 1 else jnp.concatenate(sink_terms, axis=0))
    return num / den


def _stack_heads(q, rows):
    lane = lax.broadcasted_iota(jnp.int32, (rows, LANES), 1)
    lo = lane < HD
    zero = jnp.zeros_like(q)
    return jnp.concatenate([jnp.where(lo, q, zero), jnp.where(lo, zero, q)], axis=0), lo


def _na_kernel(q_ref, k_ref, v_ref, kc_ref, vc_ref, bias_ref, o_ref, *, rows, interior_dead):
    step = pl.program_id(1)
    n_blk = rows // NA_QR

    def body(interior):
        n_slabs, dead = (NA_QR, interior_dead) if interior else (1, frozenset())
        for rb in range(NA_RB):
            j = step * NA_RB + rb
            krow = jnp.clip(NA_QR * j - NA_WIN_R // 2, 0, rows - NA_KR)
            kstart = pl.multiple_of(krow * GRID_W, GRID_W)
            pat = 1 if interior else jnp.where(j == 0, 0, jnp.where(j == n_blk - 1, 2, 1))
            qrows = slice(rb * NA_Q, (rb + 1) * NA_Q)
            for p in range(NA_HEADS // 2):
                sl = slice(p * LANES, (p + 1) * LANES)
                kb = k_ref[pl.ds(kstart, NA_K), sl]
                vb = v_ref[pl.ds(kstart, NA_K), sl]
                q = q_ref[qrows, sl]
                lo = lax.broadcasted_iota(jnp.int32, (NA_Q, LANES), 1) < HD
                outs = []
                for e in range(2):
                    qe = jnp.where(lo if e == 0 else jnp.logical_not(lo), q, jnp.zeros_like(q))
                    dead_e = frozenset((s - e * n_slabs, ks, t) for s, ks, t in dead if s // n_slabs == e)
                    outs.append(_attend(qe, [(kb, vb, bias_ref[pat, p, e * NA_Q:(e + 1) * NA_Q]),
                                             (kc_ref[:, sl], vc_ref[:, sl], None)], n_slabs=n_slabs, dead=dead_e))
                o_ref[qrows, sl] = jnp.where(lo, outs[0], outs[1]).astype(BF16)

    interior = jnp.logical_and(step > 0, step < n_blk // NA_RB - 1)

    @pl.when(interior)
    def _():
        body(True)

    @pl.when(jnp.logical_not(interior))
    def _():
        body(False)


def _na(qn, kn, vn, bias_tab, n_batch, seq, ctx_len):
    rows = seq // GRID_W
    n_steps = rows // (NA_QR * NA_RB)
    ctx_blk0 = (n_batch * seq) // ctx_len
    interior_dead = frozenset(
        (e * NA_QR + i, 0, t)
        for e in range(2) for i, pairs in enumerate(_na_plan(rows)[1]) for t, pair in enumerate(pairs)
        if pair == (NA_MASKED, NA_MASKED))
    return pl.pallas_call(
        functools.partial(_na_kernel, rows=rows, interior_dead=interior_dead),
        grid=(n_batch, n_steps),
        in_specs=[
            pl.BlockSpec((NA_RB * NA_Q, BW), lambda b, j: (b * n_steps + j, 0)),
            pl.BlockSpec((seq, BW), lambda b, j: (b, 0)),
            pl.BlockSpec((seq, BW), lambda b, j: (b, 0)),
            pl.BlockSpec((ctx_len, BW), lambda b, j: (ctx_blk0 + b, 0)),
            pl.BlockSpec((ctx_len, BW), lambda b, j: (ctx_blk0 + b, 0)),
            _const_spec(bias_tab.shape),
        ],
        out_specs=pl.BlockSpec((NA_RB * NA_Q, BW), lambda b, j: (b * n_steps + j, 0)),
        out_shape=jax.ShapeDtypeStruct((n_batch * seq, BW), BF16),
        compiler_params=_cparams(("arbitrary", "arbitrary")),
        name="na",
    )(qn, kn, vn, kn, vn, bias_tab)


NA_MASKED = 2 * NA_WIN_R - 1


def _na_plan(rows):
    n_blk = rows // NA_QR
    plan = []
    for blk in (0, 1, n_blk - 1):
        k0 = min(max(NA_QR * blk - NA_WIN_R // 2, 0), rows - NA_KR)
        per_row = []
        for i in range(NA_QR):
            r = NA_QR * blk + i
            rs = min(max(r - NA_WIN_R // 2, 0), rows - NA_WIN_R)
            slabs = [k0 + t - r + NA_WIN_R - 1 if rs <= k0 + t < rs + NA_WIN_R else NA_MASKED for t in range(NA_KR)]
            per_row.append([(slabs[2 * j], slabs[2 * j + 1]) for j in range(NA_KR // 2)])
        plan.append(per_row)
    return plan


def _na_bias_table(bias, rows):
    n_blk = rows // NA_QR
    h = bias.shape[0]
    n_dc = 2 * NA_WIN_C - 1
    qc = np.arange(GRID_W)[:, None]
    kc = np.arange(GRID_W)[None, :]
    ws = np.clip(qc - NA_WIN_C // 2, 0, GRID_W - NA_WIN_C)
    col_ok = (kc >= ws) & (kc < ws + NA_WIN_C)
    dc = np.clip(kc - qc, -(NA_WIN_C - 1), NA_WIN_C - 1) + NA_WIN_C - 1
    onehot = (dc[None] == np.arange(n_dc)[:, None, None]).astype(np.float32)
    toep = jnp.einsum('hrd,dqk->hrqk', bias.astype(F32), jnp.asarray(onehot), precision=lax.Precision.HIGHEST)
    toep = jnp.where(col_ok[None, None], toep * LOG2E, NEG_INF)
    toep = jnp.concatenate([toep, jnp.full((h, 1, GRID_W, GRID_W), NEG_INF, F32)], axis=1)
    toep2 = jnp.concatenate([toep, toep], axis=-1)
    plan = _na_plan(rows)
    return pl.pallas_call(
        functools.partial(_bias_expand_kernel, plan=plan),
        out_shape=jax.ShapeDtypeStruct((len(plan), h // 2, 2 * NA_Q, NA_K), F32),
        compiler_params=pltpu.CompilerParams(vmem_limit_bytes=VMEM_LIMIT),
        name="na_bias_expand",
    )(toep2)


def _bias_expand_kernel(t_ref, o_ref, *, plan):
    lo = lax.broadcasted_iota(jnp.int32, (GRID_W, LANES), 1) < GRID_W
    for pat, per_row in enumerate(plan):
        for hd in range(t_ref.shape[0]):
            for i, pairs in enumerate(per_row):
                r0 = (hd % 2) * NA_Q + i * GRID_W
                for j, (da, db) in enumerate(pairs):
                    o_ref[pat, hd // 2, r0:r0 + GRID_W, j * LANES:(j + 1) * LANES] = (
                        jnp.where(lo, t_ref[hd, da], t_ref[hd, db]))


def _stack_group(q):
    rows = q.shape[0]
    a, lo = _stack_heads(q[:, :LANES], rows)
    b, _ = _stack_heads(q[:, LANES:], rows)
    return jnp.concatenate([a, b], axis=0), lo


def _unstack_group(o, rows, lo):
    oa = jnp.where(lo, o[0:rows], o[rows:2 * rows])
    ob = jnp.where(lo, o[2 * rows:3 * rows], o[3 * rows:4 * rows])
    return jnp.concatenate([oa, ob], axis=1)


def _sink_tile(sink_ref, heads, rows):
    return jnp.concatenate([jnp.full((rows, LANES), sink_ref[h] * LOG2E, F32) for h in heads], axis=0)


def _wa_kernel(sink_ref, q_ref, k_ref, v_ref, kc_ref, vc_ref, o_ref, *, seq):
    gq = WA_HEADS // WA_KV_HEADS
    for t in range(WA_QB):
        n = pl.program_id(1) * WA_QB + t
        rows = slice(t * WA_Q, (t + 1) * WA_Q)
        kstart = pl.multiple_of(jnp.clip(n * WA_Q - WA_Q, 0, seq - WA_K), WA_Q)
        qpos = n * WA_Q + lax.broadcasted_iota(jnp.int32, (WA_Q, WA_K), 0)
        kpos = kstart + lax.broadcasted_iota(jnp.int32, (WA_Q, WA_K), 1)
        band = jnp.where(jnp.abs(kpos - qpos) <= WA_WINDOW, 0.0, NEG_INF).astype(F32)
        band = jnp.concatenate([band] * gq, axis=0)
        for g in range(WA_KV_HEADS):
            sl = slice(g * LANES, (g + 1) * LANES)
            sl2 = slice(g * 2 * LANES, (g + 1) * 2 * LANES)
            kb = k_ref[pl.ds(kstart, WA_K), sl]
            vb = v_ref[pl.ds(kstart, WA_K), sl]
            qs, lo = _stack_group(q_ref[rows, sl2])
            o = _attend(qs, [(kb, vb, band), (kc_ref[:, sl], vc_ref[:, sl], None)],
                        _sink_tile(sink_ref, range(g * gq, (g + 1) * gq), WA_Q))
            o_ref[rows, sl2] = _unstack_group(o, WA_Q, lo).astype(BF16)


def _wa(sink, qw, kw, vw, n_batch, seq, ctx_len):
    n_blk = seq // (WA_Q * WA_QB)
    ctx_blk0 = (n_batch * seq) // ctx_len
    gw = 2 * LANES
    return pl.pallas_call(
        functools.partial(_wa_kernel, seq=seq),
        grid=(n_batch, n_blk),
        in_specs=[
            pl.BlockSpec(memory_space=pltpu.SMEM),
            pl.BlockSpec((WA_Q * WA_QB, BW), lambda b, n: (b * n_blk + n, 0)),
            pl.BlockSpec((seq, gw), lambda b, n: (b, 0)),
            pl.BlockSpec((seq, gw), lambda b, n: (b, 0)),
            pl.BlockSpec((ctx_len, gw), lambda b, n: (ctx_blk0 + b, 0)),
            pl.BlockSpec((ctx_len, gw), lambda b, n: (ctx_blk0 + b, 0)),
        ],
        out_specs=pl.BlockSpec((WA_Q * WA_QB, BW), lambda b, n: (b * n_blk + n, 0)),
        out_shape=jax.ShapeDtypeStruct((n_batch * seq, BW), BF16),
        compiler_params=_cparams(("arbitrary", "arbitrary")),
        name="wa",
    )(sink, qw, kw, vw, kw, vw)


FFT_BLK = 16


def _swap_major(x):
    return jnp.swapaxes(x, 0, 1)


FFT_NBLK = GRID_W // FFT_BLK


def _fft_kernel(f_ref, m_ref, x_ref, zr_ref, zi_ref, ar_s, ai_s, t_scr):
    s = pl.program_id(1)

    @pl.when(s < FFT_NBLK)
    def _():
        xt = _swap_major(x_ref[...].astype(F32)).astype(BF16)
        f = f_ref[...]
        for i in range(FFT_BLK):
            t_scr[i] = _dot(f, xt[i])
        at = _swap_major(t_scr[...])
        ar_s[s] = at[:GRID_W].astype(BF16)
        ai_s[s] = at[GRID_W:].astype(BF16)

    @pl.when(s >= FFT_NBLK)
    def _():
        k0 = (s - FFT_NBLK) * FFT_BLK
        for t in range(FFT_BLK):
            a = jnp.concatenate([ar_s[jb, k0 + t] for jb in range(FFT_NBLK)]
                                + [ai_s[jb, k0 + t] for jb in range(FFT_NBLK)], axis=0)
            t_scr[t] = _dot(m_ref[t], a)
        zt = _swap_major(t_scr[...])
        zr_ref[...] = zt[:GRID_W].astype(BF16)
        zi_ref[...] = zt[GRID_W:].astype(BF16)


def _fft_tables():
    n = GRID_W
    k = np.arange(n)
    ang1 = 2.0 * np.pi * ((k[:, None] * k[None, :]) % n) / n
    f1 = np.concatenate([np.cos(ang1), -np.sin(ang1)], axis=0) / 8.0
    ka = k[:, None, None]
    kb = k[None, :, None]
    n1 = k[None, None, :]
    ang2 = 2.0 * np.pi * ((n1 * (ka + n * kb)) % (n * n)) / (n * n)
    mr, mi = np.cos(ang2), -np.sin(ang2)
    m2 = np.concatenate([np.concatenate([mr, -mi], axis=2), np.concatenate([mi, mr], axis=2)], axis=1) / 8.0
    c = np.arange(FN_GROUP_DIM)
    angc = 2.0 * np.pi * ((c[:, None] * c[None, :]) % FN_GROUP_DIM) / FN_GROUP_DIM
    eye = np.eye(FN_GROUPS)
    cbd = np.kron(eye, np.cos(angc)) / 8.0
    sbd = np.kron(eye, np.sin(angc)) / 8.0
    return f1, m2, cbd, sbd


def _ctx_dft_table(ctx_len):
    k = np.arange(ctx_len)
    ang = 2.0 * np.pi * ((k[:, None] * k[None, :]) % ctx_len) / ctx_len
    return np.concatenate([np.cos(ang), -np.sin(ang)], axis=0) / np.sqrt(ctx_len)


def _fft(fu, f1, m2, n_batch, seq):
    n = GRID_W
    x3 = fu.reshape(-1, n, BW)
    shape3 = jax.ShapeDtypeStruct((n_batch * n, n, BW), BF16)
    z_spec = pl.BlockSpec((n, FFT_BLK, BW), lambda b, s: (b, jnp.maximum(s - FFT_NBLK, 0), 0))
    zr, zi = pl.pallas_call(
        _fft_kernel,
        grid=(n_batch, 2 * FFT_NBLK),
        in_specs=[pl.BlockSpec((2 * n, n), lambda b, s: (0, 0)),
                  pl.BlockSpec((FFT_BLK, 2 * n, 2 * n), lambda b, s: (jnp.maximum(s - FFT_NBLK, 0), 0, 0)),
                  pl.BlockSpec((n, FFT_BLK, BW), lambda b, s: (b, jnp.minimum(s, FFT_NBLK - 1), 0))],
        out_specs=[z_spec, z_spec],
        out_shape=[shape3, shape3],
        scratch_shapes=[pltpu.VMEM((FFT_NBLK, n, FFT_BLK, BW), BF16), pltpu.VMEM((FFT_NBLK, n, FFT_BLK, BW), BF16),
                        pltpu.VMEM((FFT_BLK, 2 * n, BW), F32)],
        compiler_params=_cparams(("arbitrary", "arbitrary")),
        name="fft",
    )(f1, m2, x3)
    return zr.reshape(n_batch * seq, BW), zi.reshape(n_batch * seq, BW)


def _ctx_kernel(sink_ref, qn_ref, kn_ref, vn_ref, fu_ref, qw_ref, kw_ref, vw_ref, dft_ref,
                a_ref, w_ref, zr_ref, zi_ref, *, ctx_len):
    for p in range(NA_HEADS // 2):
        sl = slice(p * LANES, (p + 1) * LANES)
        qs, lo = _stack_heads(qn_ref[:, sl], ctx_len)
        o = _attend(qs, [(kn_ref[:, sl], vn_ref[:, sl], None)])
        a_ref[:, sl] = jnp.where(lo, o[:ctx_len], o[ctx_len:]).astype(BF16)
    for g in range(WA_KV_HEADS):
        sl = slice(g * LANES, (g + 1) * LANES)
        sl2 = slice(g * 2 * LANES, (g + 1) * 2 * LANES)
        qs, lo = _stack_group(qw_ref[:, sl2])
        gq = WA_HEADS // WA_KV_HEADS
        o = _attend(qs, [(kw_ref[:, sl], vw_ref[:, sl], None)],
                    _sink_tile(sink_ref, range(g * gq, (g + 1) * gq), ctx_len))
        w_ref[:, sl2] = _unstack_group(o, ctx_len, lo).astype(BF16)
    z = _dot(dft_ref[...], fu_ref[...])
    zr_ref[...] = z[:ctx_len].astype(BF16)
    zi_ref[...] = z[ctx_len:].astype(BF16)


def _ctx_mix(sink, qn, kn, vn, fu, qw, kw, vw, dft_c, n_batch, seq, ctx_len):
    blk0 = (n_batch * seq) // ctx_len

    def in_spec(w):
        return pl.BlockSpec((ctx_len, w), lambda b: (blk0 + b, 0))

    out_spec = pl.BlockSpec((ctx_len, BW), lambda b: (b, 0))
    return pl.pallas_call(
        functools.partial(_ctx_kernel, ctx_len=ctx_len),
        grid=(n_batch,),
        in_specs=[pl.BlockSpec(memory_space=pltpu.SMEM),
                  in_spec(BW), in_spec(BW), in_spec(BW), in_spec(BW), in_spec(BW),
                  in_spec(2 * LANES), in_spec(2 * LANES),
                  pl.BlockSpec(dft_c.shape, lambda b: (0, 0))],
        out_specs=[out_spec] * 4,
        out_shape=[jax.ShapeDtypeStruct((n_batch * ctx_len, BW), BF16)] * 4,
        compiler_params=_cparams(("arbitrary",)),
        name="ctx_mix",
    )(sink, qn, kn, vn, fu, qw, kw, vw, dft_c)


N_BRANCH = 3
MERGE_WSTEPS = N_BRANCH * D // W_IN_CH
MERGE_BR_CH = N_BRANCH * BW // MERGE_WSTEPS
MERGE_OUT_CH = MERGE_BR_CH
MERGE_OUT_STEPS = D // MERGE_OUT_CH


def _merge_kernel(x_ref, g_ref, sh_ref, sc_ref, gt_ref, a_ref, zr_ref, zi_ref, w_ref, *rest, n_lat_tiles):
    if n_lat_tiles is None:
        wg_ref, wbr_ref, cbd_ref, sbd_ref, wout_ref, o_ref, wg_s, wbr_s, wout_s = rest
    else:
        (ac_ref, zrc_ref, zic_ref, wc_ref, wg_ref, wbr_ref, cbd_ref, sbd_ref, wout_ref, o_ref,
         wg_s, wbr_s, wout_s) = rest
    i = pl.program_id(0)

    @pl.when(i < MERGE_WSTEPS)
    def _():
        wg_s[i] = wg_ref[...].astype(BF16)
        wbr_s[i] = wbr_ref[...].astype(BF16)

    @pl.when(i < MERGE_OUT_STEPS)
    def _():
        wout_s[i] = wout_ref[...].astype(BF16)

    @pl.when(i >= MERGE_WSTEPS)
    def _():
        if n_lat_tiles is None:
            a, zr, zi, w = a_ref[...], zr_ref[...], zi_ref[...], w_ref[...]
        else:
            is_ctx = i - MERGE_WSTEPS >= n_lat_tiles
            a = jnp.where(is_ctx, ac_ref[...], a_ref[...])
            zr = jnp.where(is_ctx, zrc_ref[...], zr_ref[...])
            zi = jnp.where(is_ctx, zic_ref[...], zi_ref[...])
            w = jnp.where(is_ctx, wc_ref[...], w_ref[...])
        x = x_ref[...]
        u = _norm_mod(x, g_ref[...], sh_ref[...], sc_ref[...]).astype(BF16)
        f = (_dot(zr, cbd_ref[...]) + _dot(zi, sbd_ref[...])).astype(BF16)
        w_br = wbr_s[...].reshape(N_BRANCH * BW, D)
        acc = None
        for b, br in enumerate((a, f, w)):
            gate = _sigmoid(_dot(u, _chunk_cols(wg_s, b * D, (b + 1) * D, W_IN_CH)))
            term = gate * _dot(br, w_br[b * BW:(b + 1) * BW])
            acc = term if acc is None else acc + term
        o_ref[...] = x + gt_ref[...] * _dot(acc.astype(BF16), wout_s[...].reshape(D, D))


def _merge(h, n_tiles, g, mod3, layer, branches, ctx_branches, weights, tile_info):
    n_lat_tiles = tile_info[0]

    def tile_of(i):
        return jnp.maximum(i - MERGE_WSTEPS, 0)

    def wstep_of(i):
        return jnp.minimum(i, MERGE_WSTEPS - 1)

    tile_spec = pl.BlockSpec((TM, D), lambda i: (tile_of(i), 0))
    vec_spec = pl.BlockSpec((1, D), lambda i: (0, 0))
    has_ctx = ctx_branches is not None
    lat_spec = pl.BlockSpec((TM, BW), lambda i: (jnp.minimum(tile_of(i), n_lat_tiles - 1), 0))
    ctx_spec = pl.BlockSpec((TM, BW), lambda i: (jnp.maximum(tile_of(i) - n_lat_tiles, 0), 0))
    in_specs = [tile_spec, vec_spec] + [_mod_spec(layer, k, *tile_info, tile_of=tile_of) for k in (3, 4, 5)]
    in_specs += [lat_spec] * 4
    args = [h, g.reshape(1, D), mod3, mod3, mod3, *branches]
    if has_ctx:
        in_specs += [ctx_spec] * 4
        args += list(ctx_branches)
    w_in, w_br, cbd, sbd, w_out = weights
    gate_blk0 = PROJ_W // W_IN_CH
    in_specs += [
        pl.BlockSpec((None, D, W_IN_CH), lambda i: (layer, 0, gate_blk0 + wstep_of(i))),
        pl.BlockSpec((None, MERGE_BR_CH, D), lambda i: (layer, wstep_of(i), 0)),
        _const_spec(cbd.shape), _const_spec(sbd.shape),
        pl.BlockSpec((None, MERGE_OUT_CH, D), lambda i: (layer, jnp.minimum(i, MERGE_OUT_STEPS - 1), 0)),
    ]
    args += list(weights)
    return pl.pallas_call(
        functools.partial(_merge_kernel, n_lat_tiles=n_lat_tiles if has_ctx else None),
        grid=(MERGE_WSTEPS + n_tiles,),
        in_specs=in_specs,
        out_specs=tile_spec,
        out_shape=jax.ShapeDtypeStruct((n_tiles * TM, D), F32),
        scratch_shapes=[pltpu.VMEM((MERGE_WSTEPS, D, W_IN_CH), BF16),
                        pltpu.VMEM((MERGE_WSTEPS, MERGE_BR_CH, D), BF16),
                        pltpu.VMEM((MERGE_OUT_STEPS, MERGE_OUT_CH, D), BF16)],
        compiler_params=_cparams(("arbitrary",)),
        name="merge_ctx" if has_ctx else "merge",
    )(*args)


def _rope_tables(seq):
    t = np.arange(seq)
    row = (t // GRID_W).astype(np.float64)
    col = (t % GRID_W).astype(np.float64)
    n_freq = HD // 4
    inv = ROPE_BASE ** (-np.arange(n_freq, dtype=np.float64) / n_freq)
    ang = np.concatenate([row[:, None] * inv, col[:, None] * inv], axis=-1)
    cos, sin = np.cos(ang), np.sin(ang)
    cos_h = np.concatenate([cos, cos], axis=1)
    sin_h = np.concatenate([-sin, sin], axis=1)
    cos2 = np.concatenate([np.tile(cos_h, (1, 2)), np.ones((TM, LANES), np.float32)], axis=0)
    sin2 = np.concatenate([np.tile(sin_h, (1, 2)), np.zeros((TM, LANES), np.float32)], axis=0)
    return jnp.asarray(cos2, F32), jnp.asarray(sin2, F32)


def kernel(x, c, ctx, c_ctx, w_ada, b_ada, g_ffn1, ffn1_w13, ffn1_w2, g_mix, w_in, na_bias, wa_sink,
           w_br, w_out, g_ffn2, ffn2_w13, ffn2_w2, g_final):
    n_batch, seq, _ = x.shape
    ctx_len = ctx.shape[1]
    depth = w_ada.shape[0]
    rows = seq // GRID_W
    n_lat = n_batch * seq
    n_lat_tiles = n_lat // TM
    n_all_tiles = (n_lat + n_batch * ctx_len) // TM
    mod_rows = 8
    tile_info = (n_lat_tiles, seq // TM, n_batch, mod_rows)

    cc = jnp.concatenate([c, c_ctx[None], jnp.zeros((mod_rows - n_batch - 1, D), F32)], axis=0)
    mod3 = _ada(cc, w_ada, b_ada).reshape(depth * mod_rows, 1, N_MOD * D)

    cos_t, sin_t = _rope_tables(seq)
    f1, m2, cbd, sbd = (jnp.asarray(t, F32).astype(BF16) for t in _fft_tables())
    dft_c = jnp.asarray(_ctx_dft_table(ctx_len), F32).astype(BF16)

    w13a, w2a = ffn1_w13, ffn1_w2
    w13b, w2b = ffn2_w13, ffn2_w2
    w_in_b = w_in
    merge_w = (w_in, w_br.reshape(depth, N_BRANCH * BW, D), cbd, sbd, w_out)

    h = x.reshape(n_lat, D)
    h_ctx = ctx.reshape(n_batch * ctx_len, D)
    for l in range(depth):
        last = l == depth - 1
        bias_tab = _na_bias_table(na_bias[l], rows)

        h = _ffn(h, n_all_tiles, g_ffn1[l], mod3, l, 0, w13a, w2a, tile_info, h_ctx=h_ctx if l == 0 else None)
        qn, kn, vn, fu, qw, kw, vw = _proj(h, n_all_tiles, g_mix[l], mod3, l, w_in_b, cos_t, sin_t, tile_info)
        a = _na(qn, kn, vn, bias_tab, n_batch, seq, ctx_len)
        w = _wa(wa_sink[l], qw, kw, vw, n_batch, seq, ctx_len)
        zr, zi = _fft(fu, f1, m2, n_batch, seq)
        if last:
            h = _merge(h, n_lat_tiles, g_mix[l], mod3, l, (a, zr, zi, w), None, merge_w, tile_info)
            h = _ffn(h, n_lat_tiles, g_ffn2[l], mod3, l, 6, w13b, w2b, tile_info, g_final=g_final)
        else:
            ctx_br = _ctx_mix(wa_sink[l], qn, kn, vn, fu, qw, kw, vw, dft_c, n_batch, seq, ctx_len)
            ac, wc, zrc, zic = ctx_br
            h = _merge(h, n_all_tiles, g_mix[l], mod3, l, (a, zr, zi, w), (ac, zrc, zic, wc), merge_w, tile_info)
            h = _ffn(h, n_all_tiles, g_ffn2[l], mod3, l, 6, w13b, w2b, tile_info)
    return h.reshape(n_batch, seq, D)
```

```python
import functools

import numpy as np
import jax
import jax.numpy as jnp
from jax import lax
from jax.experimental import pallas as pl
from jax.experimental.pallas import tpu as pltpu

D = 1024
GRID_W = 64
HD = 64
NA_HEADS = 8
NA_WIN_R = 8
NA_WIN_C = 16
FN_GROUPS = 8
FN_GROUP_DIM = 64
WA_HEADS = 8
WA_KV_HEADS = 2
WA_WINDOW = 128
D_FF = 2816
N_MOD = 9
ROPE_BASE = 10000.0
EPS = 1e-6
NEG_INF = -1e30
LOG2E = 1.4426950408889634
BW = 512

LANES = 128
MXU_DIM = 256
TM = 512
TF = MXU_DIM
VMEM_LIMIT = 56 * 1024 * 1024

NA_QR = 4
NA_KR = 12
NA_RB = 2
NA_Q = NA_QR * GRID_W
NA_K = NA_KR * GRID_W
WA_Q = 128
WA_K = 3 * WA_Q
WA_QB = 8

F32 = jnp.float32
BF16 = jnp.bfloat16


def _cparams(sem):
    return pltpu.CompilerParams(dimension_semantics=sem, vmem_limit_bytes=VMEM_LIMIT)


def _const_spec(shape):
    nd = len(shape)
    return pl.BlockSpec(shape, lambda *_: (0,) * nd, pipeline_mode=pl.Buffered(1))


def _sigmoid(x):
    return 1.0 / (1.0 + jnp.exp(-x))


def _norm_mod(x, g, shift, scale):
    y = x * lax.rsqrt(jnp.mean(x * x, axis=-1, keepdims=True) + EPS)
    return y * (g * (1.0 + scale)) + shift


def _dot(a, b):
    return jnp.dot(a, b, preferred_element_type=F32)


def _dot_nt(a, b):
    return lax.dot_general(a, b, (((1,), (1,)), ((), ())), preferred_element_type=F32)


def _ada_kernel(c_ref, w_ref, b_ref, o_ref):
    x = c_ref[...]
    sx = (x * _sigmoid(x)).astype(BF16)
    o_ref[...] = _dot(sx, w_ref[...].astype(BF16)) + b_ref[...]


def _ada(cc, w_ada, b_ada):
    depth = w_ada.shape[0]
    n = w_ada.shape[2]
    tn = 1536
    rows = cc.shape[0]
    return pl.pallas_call(
        _ada_kernel,
        grid=(depth, n // tn),
        in_specs=[
            pl.BlockSpec((rows, D), lambda l, j: (0, 0)),
            pl.BlockSpec((None, D, tn), lambda l, j: (l, 0, j)),
            pl.BlockSpec((None, 1, tn), lambda l, j: (l, 0, j)),
        ],
        out_specs=pl.BlockSpec((None, rows, tn), lambda l, j: (l, 0, j)),
        out_shape=jax.ShapeDtypeStruct((depth, rows, n), F32),
        compiler_params=_cparams(("arbitrary", "arbitrary")),
        name="ada",
    )(cc, w_ada, b_ada.reshape(depth, 1, n))


def _mod_spec(layer, k, n_lat_tiles, tiles_per_batch, n_batch, mod_rows, tile_of=lambda i: i):
    def idx(i):
        t = tile_of(i)
        row = jnp.where(t < n_lat_tiles, t // tiles_per_batch, n_batch)
        return (layer * mod_rows + row, 0, k)

    return pl.BlockSpec((None, 1, D), idx)


FFN_W13_CH = 2 * TF
FFN_WSTEPS = 2 * D_FF // FFN_W13_CH
FFN_W2_CH = D_FF // FFN_WSTEPS


def _ffn_kernel(*refs, final, split_at):
    refs = list(refs)
    x_ref = refs.pop(0)
    if split_at is not None:
        xc_ref = refs.pop(0)
    g_ref, sh_ref, sc_ref, gt_ref, w13_ref, w2_ref = refs[:6]
    if final:
        gf_ref, o_ref, w13_s, w2_s, hm_ref = refs[6:]
    else:
        o_ref, w13_s, w2_s, hm_ref = refs[6:]
    i = pl.program_id(0)

    @pl.when(i < FFN_WSTEPS)
    def _():
        w13_s[i] = w13_ref[...].astype(BF16)
        w2_s[i] = w2_ref[...].astype(BF16)

    def w13_cols(lo):
        off = lo % FFN_W13_CH
        return w13_s[lo // FFN_W13_CH][:, off:off + TF]

    @pl.when(i >= FFN_WSTEPS)
    def _():
        x = x_ref[...]
        if split_at is not None:
            x = jnp.where(i - FFN_WSTEPS >= split_at, xc_ref[...], x)
        u = _norm_mod(x, g_ref[...], sh_ref[...], sc_ref[...]).astype(BF16)
        for lo in range(0, D_FF, TF):
            a = _dot(u, w13_cols(lo))
            b = _dot(u, w13_cols(D_FF + lo))
            hm_ref[:, lo:lo + TF] = (a * _sigmoid(a) * b).astype(BF16)
        f = _dot(hm_ref[...], w2_s[...].reshape(D_FF, D))
        out = x + 0.5 * gt_ref[...] * f
        if final:
            out = out * lax.rsqrt(jnp.mean(out * out, axis=-1, keepdims=True) + EPS) * gf_ref[...]
        o_ref[...] = out


def _ffn(h, n_tiles, g, mod3, layer, mod_k, w13, w2, tile_info, g_final=None, h_ctx=None):
    final = g_final is not None
    n_lat_tiles = tile_info[0]

    def tile_of(i):
        return jnp.maximum(i - FFN_WSTEPS, 0)

    def wstep_of(i):
        return jnp.minimum(i, FFN_WSTEPS - 1)

    tile_spec = pl.BlockSpec((TM, D), lambda i: (tile_of(i), 0))
    vec_spec = pl.BlockSpec((1, D), lambda i: (0, 0))
    if h_ctx is None:
        in_specs, args = [tile_spec], [h]
    else:
        in_specs = [pl.BlockSpec((TM, D), lambda i: (jnp.minimum(tile_of(i), n_lat_tiles - 1), 0)),
                    pl.BlockSpec((TM, D), lambda i: (jnp.maximum(tile_of(i) - n_lat_tiles, 0), 0))]
        args = [h, h_ctx]
    in_specs += [
        vec_spec,
        _mod_spec(layer, mod_k, *tile_info, tile_of=tile_of),
        _mod_spec(layer, mod_k + 1, *tile_info, tile_of=tile_of),
        _mod_spec(layer, mod_k + 2, *tile_info, tile_of=tile_of),
        pl.BlockSpec((None, D, FFN_W13_CH), lambda i: (layer, 0, wstep_of(i))),
        pl.BlockSpec((None, FFN_W2_CH, D), lambda i: (layer, wstep_of(i), 0)),
    ]
    args += [g.reshape(1, D), mod3, mod3, mod3, w13, w2]
    if final:
        in_specs.append(vec_spec)
        args.append(g_final.reshape(1, D))
    return pl.pallas_call(
        functools.partial(_ffn_kernel, final=final, split_at=None if h_ctx is None else n_lat_tiles),
        grid=(FFN_WSTEPS + n_tiles,),
        in_specs=in_specs,
        out_specs=tile_spec,
        out_shape=jax.ShapeDtypeStruct((n_tiles * TM, D), F32),
        scratch_shapes=[pltpu.VMEM((FFN_WSTEPS, D, FFN_W13_CH), BF16),
                        pltpu.VMEM((FFN_WSTEPS, FFN_W2_CH, D), BF16),
                        pltpu.VMEM((TM, D_FF), BF16)],
        compiler_params=_cparams(("arbitrary",)),
        name="ffn_final" if final else ("ffn" if h_ctx is None else "ffn_split"),
    )(*args)


PROJ_W = 5 * BW + 2 * LANES


def _rope(t, cos, sin_signed, first_half):
    partner = jnp.where(first_half, pltpu.roll(t, LANES - HD // 2, axis=1), pltpu.roll(t, HD // 2, axis=1))
    return t * cos + partner * sin_signed


def _dup_heads(t, lo):
    sw = pltpu.roll(t, HD, axis=1)
    return jnp.where(lo, t, sw), jnp.where(lo, sw, t)


W_IN_CH = MXU_DIM
PROJ_WSTEPS = PROJ_W // W_IN_CH


def _chunk_cols(ws_ref, lo, hi, ch):
    parts = [ws_ref[c] for c in range(lo // ch, hi // ch)]
    return parts[0] if len(parts) == 1 else jnp.concatenate(parts, axis=1)


def _proj_kernel(x_ref, g_ref, sh_ref, sc_ref, w_ref, cos_ref, sin_ref,
                 qn_ref, kn_ref, vn_ref, fu_ref, qw_ref, kw_ref, vw_ref, w_s):
    i = pl.program_id(0)

    @pl.when(i < PROJ_WSTEPS)
    def _():
        w_s[i] = w_ref[...].astype(BF16)

    def w(lo, hi):
        return _chunk_cols(w_s, lo, hi, W_IN_CH)

    @pl.when(i >= PROJ_WSTEPS)
    def _():
        u = _norm_mod(x_ref[...], g_ref[...], sh_ref[...], sc_ref[...]).astype(BF16)
        scale = HD ** -0.5 * LOG2E
        cos = cos_ref[...]
        sin = sin_ref[...]
        lane = lax.broadcasted_iota(jnp.int32, (TM, LANES), 1)
        first_half = (lane & (HD - 1)) < (HD // 2)
        lo = lane < HD
        kv = _dot(u, w(5 * BW, PROJ_W))
        k0, k1 = _dup_heads(_rope(kv[:, :LANES], cos, sin, first_half), lo)
        kw_ref[:, :LANES] = k0.astype(BF16)
        kw_ref[:, LANES:] = k1.astype(BF16)
        v0, v1 = _dup_heads(kv[:, LANES:], lo)
        vw_ref[:, :LANES] = v0.astype(BF16)
        vw_ref[:, LANES:] = v1.astype(BF16)
        wq = _dot(u, w(4 * BW, 5 * BW))
        for j in range(BW // LANES):
            sl = slice(j * LANES, (j + 1) * LANES)
            qw_ref[:, sl] = (_rope(wq[:, sl], cos, sin, first_half) * scale).astype(BF16)
        qn_ref[...] = (_dot(u, w(0, BW)) * scale).astype(BF16)
        kn_ref[...] = _dot(u, w(BW, 2 * BW)).astype(BF16)
        vn_ref[...] = _dot(u, w(2 * BW, 3 * BW)).astype(BF16)
        fu_ref[...] = _dot(u, w(3 * BW, 4 * BW)).astype(BF16)


def _proj(h, n_tiles, g, mod3, layer, w_in, cos_t, sin_t, tile_info):
    n_lat_tiles, tiles_per_batch, _, _ = tile_info

    def tile_of(i):
        return jnp.maximum(i - PROJ_WSTEPS, 0)

    def rope_blk(i):
        t = tile_of(i)
        return jnp.where(t < n_lat_tiles, t % tiles_per_batch, tiles_per_batch)

    tile_spec = pl.BlockSpec((TM, D), lambda i: (tile_of(i), 0))
    vec_spec = pl.BlockSpec((1, D), lambda i: (0, 0))
    rope_spec = pl.BlockSpec((TM, LANES), lambda i: (rope_blk(i), 0))
    w_spec = pl.BlockSpec((None, D, W_IN_CH), lambda i: (layer, 0, jnp.minimum(i, PROJ_WSTEPS - 1)))
    rows = n_tiles * TM

    def out_spec(w):
        return pl.BlockSpec((TM, w), lambda i: (tile_of(i), 0))

    widths = [BW, BW, BW, BW, BW, 2 * LANES, 2 * LANES]
    return pl.pallas_call(
        _proj_kernel,
        grid=(PROJ_WSTEPS + n_tiles,),
        in_specs=[tile_spec, vec_spec,
                  _mod_spec(layer, 3, *tile_info, tile_of=tile_of), _mod_spec(layer, 4, *tile_info, tile_of=tile_of),
                  w_spec, rope_spec, rope_spec],
        out_specs=[out_spec(w) for w in widths],
        out_shape=[jax.ShapeDtypeStruct((rows, w), BF16) for w in widths],
        scratch_shapes=[pltpu.VMEM((PROJ_WSTEPS, D, W_IN_CH), BF16)],
        compiler_params=_cparams(("arbitrary",)),
        name="proj",
    )(h, g.reshape(1, D), mod3, mod3, w_in, cos_t, sin_t)


def _lane_tiles(s):
    return [s[:, j * LANES:(j + 1) * LANES] for j in range(s.shape[1] // LANES)]


def _attend(qs, key_sets, sink_tile=None, n_slabs=1, dead=frozenset()):
    raw = [_dot_nt(qs, k) for k, _, _ in key_sets]
    rows_per_slab = qs.shape[0] // n_slabs
    p_rows = [[] for _ in key_sets]
    sink_terms = []
    for sl in range(n_slabs):
        rows = slice(sl * rows_per_slab, (sl + 1) * rows_per_slab)
        pieces = []
        for ks, (s, (_, _, bias)) in enumerate(zip(raw, key_sets)):
            for t in range(s.shape[1] // LANES):
                if (sl, ks, t) in dead:
                    continue
                cols = slice(t * LANES, (t + 1) * LANES)
                piece = s[rows, cols]
                if bias is not None:
                    piece = piece + bias[rows, cols]
                pieces.append((ks, t, piece))
        tiles = [pc for _, _, pc in pieces]
        if sink_tile is not None:
            tiles.append(sink_tile[rows])
        m = functools.reduce(jnp.maximum, tiles).max(axis=-1, keepdims=True)
        probs = {(ks, t): jnp.exp2(pc - m).astype(BF16) for ks, t, pc in pieces}
        if sink_tile is not None:
            sink_terms.append(jnp.exp2(sink_tile[rows] - m))
        for ks, s in enumerate(raw):
            blocks = [probs.get((ks, t), jnp.zeros((rows_per_slab, LANES), BF16)) for t in range(s.shape[1] // LANES)]
            p_rows[ks].append(jnp.concatenate(blocks, axis=1))
    acc = None
    for ks, (_, v, _) in enumerate(key_sets):
        v_ones = jnp.concatenate([v, jnp.ones_like(v)], axis=1)
        o = _dot(jnp.concatenate(p_rows[ks], axis=0), v_ones)
        acc = o if acc is None else acc + o
    num, den = acc[:, :LANES], acc[:, LANES:]
    if sink_tile is not None:
        den = den + (sink_terms[0] if n_slabs == 1 else jnp.concatenate(sink_terms, axis=0))
    return num / den


def _stack_heads(q, rows):
    lane = lax.broadcasted_iota(jnp.int32, (rows, LANES), 1)
    lo = lane < HD
    zero = jnp.zeros_like(q)
    return jnp.concatenate([jnp.where(lo, q, zero), jnp.where(lo, zero, q)], axis=0), lo


def _na_kernel(q_ref, k_ref, v_ref, kc_ref, vc_ref, bias_ref, o_ref, *, rows, interior_dead):
    step = pl.program_id(1)
    n_blk = rows // NA_QR

    def body(interior):
        n_slabs, dead = (NA_QR, interior_dead) if interior else (1, frozenset())
        for rb in range(NA_RB):
            j = step * NA_RB + rb
            krow = jnp.clip(NA_QR * j - NA_WIN_R // 2, 0, rows - NA_KR)
            kstart = pl.multiple_of(krow * GRID_W, GRID_W)
            pat = 1 if interior else jnp.where(j == 0, 0, jnp.where(j == n_blk - 1, 2, 1))
            qrows = slice(rb * NA_Q, (rb + 1) * NA_Q)
            for p in range(NA_HEADS // 2):
                sl = slice(p * LANES, (p + 1) * LANES)
                kb = k_ref[pl.ds(kstart, NA_K), sl]
                vb = v_ref[pl.ds(kstart, NA_K), sl]
                q = q_ref[qrows, sl]
                lo = lax.broadcasted_iota(jnp.int32, (NA_Q, LANES), 1) < HD
                outs = []
                for e in range(2):
                    qe = jnp.where(lo if e == 0 else jnp.logical_not(lo), q, jnp.zeros_like(q))
                    dead_e = frozenset((s - e * n_slabs, ks, t) for s, ks, t in dead if s // n_slabs == e)
                    outs.append(_attend(qe, [(kb, vb, bias_ref[pat, p, e * NA_Q:(e + 1) * NA_Q]),
                                             (kc_ref[:, sl], vc_ref[:, sl], None)], n_slabs=n_slabs, dead=dead_e))
                o_ref[qrows, sl] = jnp.where(lo, outs[0], outs[1]).astype(BF16)

    interior = jnp.logical_and(step > 0, step < n_blk // NA_RB - 1)

    @pl.when(interior)
    def _():
        body(True)

    @pl.when(jnp.logical_not(interior))
    def _():
        body(False)


def _na(qn, kn, vn, bias_tab, n_batch, seq, ctx_len):
    rows = seq // GRID_W
    n_steps = rows // (NA_QR * NA_RB)
    ctx_blk0 = (n_batch * seq) // ctx_len
    interior_dead = frozenset(
        (e * NA_QR + i, 0, t)
        for e in range(2) for i, pairs in enumerate(_na_plan(rows)[1]) for t, pair in enumerate(pairs)
        if pair == (NA_MASKED, NA_MASKED))
    return pl.pallas_call(
        functools.partial(_na_kernel, rows=rows, interior_dead=interior_dead),
        grid=(n_batch, n_steps),
        in_specs=[
            pl.BlockSpec((NA_RB * NA_Q, BW), lambda b, j: (b * n_steps + j, 0)),
            pl.BlockSpec((seq, BW), lambda b, j: (b, 0)),
            pl.BlockSpec((seq, BW), lambda b, j: (b, 0)),
            pl.BlockSpec((ctx_len, BW), lambda b, j: (ctx_blk0 + b, 0)),
            pl.BlockSpec((ctx_len, BW), lambda b, j: (ctx_blk0 + b, 0)),
            _const_spec(bias_tab.shape),
        ],
        out_specs=pl.BlockSpec((NA_RB * NA_Q, BW), lambda b, j: (b * n_steps + j, 0)),
        out_shape=jax.ShapeDtypeStruct((n_batch * seq, BW), BF16),
        compiler_params=_cparams(("arbitrary", "arbitrary")),
        name="na",
    )(qn, kn, vn, kn, vn, bias_tab)


NA_MASKED = 2 * NA_WIN_R - 1


def _na_plan(rows):
    n_blk = rows // NA_QR
    plan = []
    for blk in (0, 1, n_blk - 1):
        k0 = min(max(NA_QR * blk - NA_WIN_R // 2, 0), rows - NA_KR)
        per_row = []
        for i in range(NA_QR):
            r = NA_QR * blk + i
            rs = min(max(r - NA_WIN_R // 2, 0), rows - NA_WIN_R)
            slabs = [k0 + t - r + NA_WIN_R - 1 if rs <= k0 + t < rs + NA_WIN_R else NA_MASKED for t in range(NA_KR)]
            per_row.append([(slabs[2 * j], slabs[2 * j + 1]) for j in range(NA_KR // 2)])
        plan.append(per_row)
    return plan


def _na_bias_table(bias, rows):
    n_blk = rows // NA_QR
    h = bias.shape[0]
    n_dc = 2 * NA_WIN_C - 1
    qc = np.arange(GRID_W)[:, None]
    kc = np.arange(GRID_W)[None, :]
    ws = np.clip(qc - NA_WIN_C // 2, 0, GRID_W - NA_WIN_C)
    col_ok = (kc >= ws) & (kc < ws + NA_WIN_C)
    dc = np.clip(kc - qc, -(NA_WIN_C - 1), NA_WIN_C - 1) + NA_WIN_C - 1
    onehot = (dc[None] == np.arange(n_dc)[:, None, None]).astype(np.float32)
    toep = jnp.einsum('hrd,dqk->hrqk', bias.astype(F32), jnp.asarray(onehot), precision=lax.Precision.HIGHEST)
    toep = jnp.where(col_ok[None, None], toep * LOG2E, NEG_INF)
    toep = jnp.concatenate([toep, jnp.full((h, 1, GRID_W, GRID_W), NEG_INF, F32)], axis=1)
    toep2 = jnp.concatenate([toep, toep], axis=-1)
    plan = _na_plan(rows)
    return pl.pallas_call(
        functools.partial(_bias_expand_kernel, plan=plan),
        out_shape=jax.ShapeDtypeStruct((len(plan), h // 2, 2 * NA_Q, NA_K), F32),
        compiler_params=pltpu.CompilerParams(vmem_limit_bytes=VMEM_LIMIT),
        name="na_bias_expand",
    )(toep2)


def _bias_expand_kernel(t_ref, o_ref, *, plan):
    lo = lax.broadcasted_iota(jnp.int32, (GRID_W, LANES), 1) < GRID_W
    for pat, per_row in enumerate(plan):
        for hd in range(t_ref.shape[0]):
            for i, pairs in enumerate(per_row):
                r0 = (hd % 2) * NA_Q + i * GRID_W
                for j, (da, db) in enumerate(pairs):
                    o_ref[pat, hd // 2, r0:r0 + GRID_W, j * LANES:(j + 1) * LANES] = (
                        jnp.where(lo, t_ref[hd, da], t_ref[hd, db]))


def _stack_group(q):
    rows = q.shape[0]
    a, lo = _stack_heads(q[:, :LANES], rows)
    b, _ = _stack_heads(q[:, LANES:], rows)
    return jnp.concatenate([a, b], axis=0), lo


def _unstack_group(o, rows, lo):
    oa = jnp.where(lo, o[0:rows], o[rows:2 * rows])
    ob = jnp.where(lo, o[2 * rows:3 * rows], o[3 * rows:4 * rows])
    return jnp.concatenate([oa, ob], axis=1)


def _sink_tile(sink_ref, heads, rows):
    return jnp.concatenate([jnp.full((rows, LANES), sink_ref[h] * LOG2E, F32) for h in heads], axis=0)


def _wa_kernel(sink_ref, q_ref, k_ref, v_ref, kc_ref, vc_ref, o_ref, *, seq):
    gq = WA_HEADS // WA_KV_HEADS
    for t in range(WA_QB):
        n = pl.program_id(1) * WA_QB + t
        rows = slice(t * WA_Q, (t + 1) * WA_Q)
        kstart = pl.multiple_of(jnp.clip(n * WA_Q - WA_Q, 0, seq - WA_K), WA_Q)
        qpos = n * WA_Q + lax.broadcasted_iota(jnp.int32, (WA_Q, WA_K), 0)
        kpos = kstart + lax.broadcasted_iota(jnp.int32, (WA_Q, WA_K), 1)
        band = jnp.where(jnp.abs(kpos - qpos) <= WA_WINDOW, 0.0, NEG_INF).astype(F32)
        band = jnp.concatenate([band] * gq, axis=0)
        for g in range(WA_KV_HEADS):
            sl = slice(g * LANES, (g + 1) * LANES)
            sl2 = slice(g * 2 * LANES, (g + 1) * 2 * LANES)
            kb = k_ref[pl.ds(kstart, WA_K), sl]
            vb = v_ref[pl.ds(kstart, WA_K), sl]
            qs, lo = _stack_group(q_ref[rows, sl2])
            o = _attend(qs, [(kb, vb, band), (kc_ref[:, sl], vc_ref[:, sl], None)],
                        _sink_tile(sink_ref, range(g * gq, (g + 1) * gq), WA_Q))
            o_ref[rows, sl2] = _unstack_group(o, WA_Q, lo).astype(BF16)


def _wa(sink, qw, kw, vw, n_batch, seq, ctx_len):
    n_blk = seq // (WA_Q * WA_QB)
    ctx_blk0 = (n_batch * seq) // ctx_len
    gw = 2 * LANES
    return pl.pallas_call(
        functools.partial(_wa_kernel, seq=seq),
        grid=(n_batch, n_blk),
        in_specs=[
            pl.BlockSpec(memory_space=pltpu.SMEM),
            pl.BlockSpec((WA_Q * WA_QB, BW), lambda b, n: (b * n_blk + n, 0)),
            pl.BlockSpec((seq, gw), lambda b, n: (b, 0)),
            pl.BlockSpec((seq, gw), lambda b, n: (b, 0)),
            pl.BlockSpec((ctx_len, gw), lambda b, n: (ctx_blk0 + b, 0)),
            pl.BlockSpec((ctx_len, gw), lambda b, n: (ctx_blk0 + b, 0)),
        ],
        out_specs=pl.BlockSpec((WA_Q * WA_QB, BW), lambda b, n: (b * n_blk + n, 0)),
        out_shape=jax.ShapeDtypeStruct((n_batch * seq, BW), BF16),
        compiler_params=_cparams(("arbitrary", "arbitrary")),
        name="wa",
    )(sink, qw, kw, vw, kw, vw)


FFT_BLK = 16


def _swap_major(x):
    return jnp.swapaxes(x, 0, 1)


FFT_NBLK = GRID_W // FFT_BLK


def _fft_kernel(f_ref, m_ref, x_ref, zr_ref, zi_ref, ar_s, ai_s, t_scr):
    s = pl.program_id(1)

    @pl.when(s < FFT_NBLK)
    def _():
        xt = _swap_major(x_ref[...].astype(F32)).astype(BF16)
        f = f_ref[...]
        for i in range(FFT_BLK):
            t_scr[i] = _dot(f, xt[i])
        at = _swap_major(t_scr[...])
        ar_s[s] = at[:GRID_W].astype(BF16)
        ai_s[s] = at[GRID_W:].astype(BF16)

    @pl.when(s >= FFT_NBLK)
    def _():
        k0 = (s - FFT_NBLK) * FFT_BLK
        for t in range(FFT_BLK):
            a = jnp.concatenate([ar_s[jb, k0 + t] for jb in range(FFT_NBLK)]
                                + [ai_s[jb, k0 + t] for jb in range(FFT_NBLK)], axis=0)
            t_scr[t] = _dot(m_ref[t], a)
        zt = _swap_major(t_scr[...])
        zr_ref[...] = zt[:GRID_W].astype(BF16)
        zi_ref[...] = zt[GRID_W:].astype(BF16)


def _fft_tables():
    n = GRID_W
    k = np.arange(n)
    ang1 = 2.0 * np.pi * ((k[:, None] * k[None, :]) % n) / n
    f1 = np.concatenate([np.cos(ang1), -np.sin(ang1)], axis=0) / 8.0
    ka = k[:, None, None]
    kb = k[None, :, None]
    n1 = k[None, None, :]
    ang2 = 2.0 * np.pi * ((n1 * (ka + n * kb)) % (n * n)) / (n * n)
    mr, mi = np.cos(ang2), -np.sin(ang2)
    m2 = np.concatenate([np.concatenate([mr, -mi], axis=2), np.concatenate([mi, mr], axis=2)], axis=1) / 8.0
    c = np.arange(FN_GROUP_DIM)
    angc = 2.0 * np.pi * ((c[:, None] * c[None, :]) % FN_GROUP_DIM) / FN_GROUP_DIM
    eye = np.eye(FN_GROUPS)
    cbd = np.kron(eye, np.cos(angc)) / 8.0
    sbd = np.kron(eye, np.sin(angc)) / 8.0
    return f1, m2, cbd, sbd


def _ctx_dft_table(ctx_len):
    k = np.arange(ctx_len)
    ang = 2.0 * np.pi * ((k[:, None] * k[None, :]) % ctx_len) / ctx_len
    return np.concatenate([np.cos(ang), -np.sin(ang)], axis=0) / np.sqrt(ctx_len)


def _fft(fu, f1, m2, n_batch, seq):
    n = GRID_W
    x3 = fu.reshape(-1, n, BW)
    shape3 = jax.ShapeDtypeStruct((n_batch * n, n, BW), BF16)
    z_spec = pl.BlockSpec((n, FFT_BLK, BW), lambda b, s: (b, jnp.maximum(s - FFT_NBLK, 0), 0))
    zr, zi = pl.pallas_call(
        _fft_kernel,
        grid=(n_batch, 2 * FFT_NBLK),
        in_specs=[pl.BlockSpec((2 * n, n), lambda b, s: (0, 0)),
                  pl.BlockSpec((FFT_BLK, 2 * n, 2 * n), lambda b, s: (jnp.maximum(s - FFT_NBLK, 0), 0, 0)),
                  pl.BlockSpec((n, FFT_BLK, BW), lambda b, s: (b, jnp.minimum(s, FFT_NBLK - 1), 0))],
        out_specs=[z_spec, z_spec],
        out_shape=[shape3, shape3],
        scratch_shapes=[pltpu.VMEM((FFT_NBLK, n, FFT_BLK, BW), BF16), pltpu.VMEM((FFT_NBLK, n, FFT_BLK, BW), BF16),
                        pltpu.VMEM((FFT_BLK, 2 * n, BW), F32)],
        compiler_params=_cparams(("arbitrary", "arbitrary")),
        name="fft",
    )(f1, m2, x3)
    return zr.reshape(n_batch * seq, BW), zi.reshape(n_batch * seq, BW)


def _ctx_kernel(sink_ref, qn_ref, kn_ref, vn_ref, fu_ref, qw_ref, kw_ref, vw_ref, dft_ref,
                a_ref, w_ref, zr_ref, zi_ref, *, ctx_len):
    for p in range(NA_HEADS // 2):
        sl = slice(p * LANES, (p + 1) * LANES)
        qs, lo = _stack_heads(qn_ref[:, sl], ctx_len)
        o = _attend(qs, [(kn_ref[:, sl], vn_ref[:, sl], None)])
        a_ref[:, sl] = jnp.where(lo, o[:ctx_len], o[ctx_len:]).astype(BF16)
    for g in range(WA_KV_HEADS):
        sl = slice(g * LANES, (g + 1) * LANES)
        sl2 = slice(g * 2 * LANES, (g + 1) * 2 * LANES)
        qs, lo = _stack_group(qw_ref[:, sl2])
        gq = WA_HEADS // WA_KV_HEADS
        o = _attend(qs, [(kw_ref[:, sl], vw_ref[:, sl], None)],
                    _sink_tile(sink_ref, range(g * gq, (g + 1) * gq), ctx_len))
        w_ref[:, sl2] = _unstack_group(o, ctx_len, lo).astype(BF16)
    z = _dot(dft_ref[...], fu_ref[...])
    zr_ref[...] = z[:ctx_len].astype(BF16)
    zi_ref[...] = z[ctx_len:].astype(BF16)


def _ctx_mix(sink, qn, kn, vn, fu, qw, kw, vw, dft_c, n_batch, seq, ctx_len):
    blk0 = (n_batch * seq) // ctx_len

    def in_spec(w):
        return pl.BlockSpec((ctx_len, w), lambda b: (blk0 + b, 0))

    out_spec = pl.BlockSpec((ctx_len, BW), lambda b: (b, 0))
    return pl.pallas_call(
        functools.partial(_ctx_kernel, ctx_len=ctx_len),
        grid=(n_batch,),
        in_specs=[pl.BlockSpec(memory_space=pltpu.SMEM),
                  in_spec(BW), in_spec(BW), in_spec(BW), in_spec(BW), in_spec(BW),
                  in_spec(2 * LANES), in_spec(2 * LANES),
                  pl.BlockSpec(dft_c.shape, lambda b: (0, 0))],
        out_specs=[out_spec] * 4,
        out_shape=[jax.ShapeDtypeStruct((n_batch * ctx_len, BW), BF16)] * 4,
        compiler_params=_cparams(("arbitrary",)),
        name="ctx_mix",
    )(sink, qn, kn, vn, fu, qw, kw, vw, dft_c)


N_BRANCH = 3
MERGE_WSTEPS = N_BRANCH * D // W_IN_CH
MERGE_BR_CH = N_BRANCH * BW // MERGE_WSTEPS
MERGE_OUT_CH = MERGE_BR_CH
MERGE_OUT_STEPS = D // MERGE_OUT_CH


def _merge_kernel(x_ref, g_ref, sh_ref, sc_ref, gt_ref, a_ref, zr_ref, zi_ref, w_ref, *rest, n_lat_tiles):
    if n_lat_tiles is None:
        wg_ref, wbr_ref, cbd_ref, sbd_ref, wout_ref, o_ref, wg_s, wbr_s, wout_s = rest
    else:
        (ac_ref, zrc_ref, zic_ref, wc_ref, wg_ref, wbr_ref, cbd_ref, sbd_ref, wout_ref, o_ref,
         wg_s, wbr_s, wout_s) = rest
    i = pl.program_id(0)

    @pl.when(i < MERGE_WSTEPS)
    def _():
        wg_s[i] = wg_ref[...].astype(BF16)
        wbr_s[i] = wbr_ref[...].astype(BF16)

    @pl.when(i < MERGE_OUT_STEPS)
    def _():
        wout_s[i] = wout_ref[...].astype(BF16)

    @pl.when(i >= MERGE_WSTEPS)
    def _():
        if n_lat_tiles is None:
            a, zr, zi, w = a_ref[...], zr_ref[...], zi_ref[...], w_ref[...]
        else:
            is_ctx = i - MERGE_WSTEPS >= n_lat_tiles
            a = jnp.where(is_ctx, ac_ref[...], a_ref[...])
            zr = jnp.where(is_ctx, zrc_ref[...], zr_ref[...])
            zi = jnp.where(is_ctx, zic_ref[...], zi_ref[...])
            w = jnp.where(is_ctx, wc_ref[...], w_ref[...])
        x = x_ref[...]
        u = _norm_mod(x, g_ref[...], sh_ref[...], sc_ref[...]).astype(BF16)
        f = (_dot(zr, cbd_ref[...]) + _dot(zi, sbd_ref[...])).astype(BF16)
        w_br = wbr_s[...].reshape(N_BRANCH * BW, D)
        acc = None
        for b, br in enumerate((a, f, w)):
            gate = _sigmoid(_dot(u, _chunk_cols(wg_s, b * D, (b + 1) * D, W_IN_CH)))
            term = gate * _dot(br, w_br[b * BW:(b + 1) * BW])
            acc = term if acc is None else acc + term
        o_ref[...] = x + gt_ref[...] * _dot(acc.astype(BF16), wout_s[...].reshape(D, D))


def _merge(h, n_tiles, g, mod3, layer, branches, ctx_branches, weights, tile_info):
    n_lat_tiles = tile_info[0]

    def tile_of(i):
        return jnp.maximum(i - MERGE_WSTEPS, 0)

    def wstep_of(i):
        return jnp.minimum(i, MERGE_WSTEPS - 1)

    tile_spec = pl.BlockSpec((TM, D), lambda i: (tile_of(i), 0))
    vec_spec = pl.BlockSpec((1, D), lambda i: (0, 0))
    has_ctx = ctx_branches is not None
    lat_spec = pl.BlockSpec((TM, BW), lambda i: (jnp.minimum(tile_of(i), n_lat_tiles - 1), 0))
    ctx_spec = pl.BlockSpec((TM, BW), lambda i: (jnp.maximum(tile_of(i) - n_lat_tiles, 0), 0))
    in_specs = [tile_spec, vec_spec] + [_mod_spec(layer, k, *tile_info, tile_of=tile_of) for k in (3, 4, 5)]
    in_specs += [lat_spec] * 4
    args = [h, g.reshape(1, D), mod3, mod3, mod3, *branches]
    if has_ctx:
        in_specs += [ctx_spec] * 4
        args += list(ctx_branches)
    w_in, w_br, cbd, sbd, w_out = weights
    gate_blk0 = PROJ_W // W_IN_CH
    in_specs += [
        pl.BlockSpec((None, D, W_IN_CH), lambda i: (layer, 0, gate_blk0 + wstep_of(i))),
        pl.BlockSpec((None, MERGE_BR_CH, D), lambda i: (layer, wstep_of(i), 0)),
        _const_spec(cbd.shape), _const_spec(sbd.shape),
        pl.BlockSpec((None, MERGE_OUT_CH, D), lambda i: (layer, jnp.minimum(i, MERGE_OUT_STEPS - 1), 0)),
    ]
    args += list(weights)
    return pl.pallas_call(
        functools.partial(_merge_kernel, n_lat_tiles=n_lat_tiles if has_ctx else None),
        grid=(MERGE_WSTEPS + n_tiles,),
        in_specs=in_specs,
        out_specs=tile_spec,
        out_shape=jax.ShapeDtypeStruct((n_tiles * TM, D), F32),
        scratch_shapes=[pltpu.VMEM((MERGE_WSTEPS, D, W_IN_CH), BF16),
                        pltpu.VMEM((MERGE_WSTEPS, MERGE_BR_CH, D), BF16),
                        pltpu.VMEM((MERGE_OUT_STEPS, MERGE_OUT_CH, D), BF16)],
        compiler_params=_cparams(("arbitrary",)),
        name="merge_ctx" if has_ctx else "merge",
    )(*args)


def _rope_tables(seq):
    t = np.arange(seq)
    row = (t // GRID_W).astype(np.float64)
    col = (t % GRID_W).astype(np.float64)
    n_freq = HD // 4
    inv = ROPE_BASE ** (-np.arange(n_freq, dtype=np.float64) / n_freq)
    ang = np.concatenate([row[:, None] * inv, col[:, None] * inv], axis=-1)
    cos, sin = np.cos(ang), np.sin(ang)
    cos_h = np.concatenate([cos, cos], axis=1)
    sin_h = np.concatenate([-sin, sin], axis=1)
    cos2 = np.concatenate([np.tile(cos_h, (1, 2)), np.ones((TM, LANES), np.float32)], axis=0)
    sin2 = np.concatenate([np.tile(sin_h, (1, 2)), np.zeros((TM, LANES), np.float32)], axis=0)
    return jnp.asarray(cos2, F32), jnp.asarray(sin2, F32)


def kernel(x, c, ctx, c_ctx, w_ada, b_ada, g_ffn1, ffn1_w13, ffn1_w2, g_mix, w_in, na_bias, wa_sink,
           w_br, w_out, g_ffn2, ffn2_w13, ffn2_w2, g_final):
    n_batch, seq, _ = x.shape
    ctx_len = ctx.shape[1]
    depth = w_ada.shape[0]
    rows = seq // GRID_W
    n_lat = n_batch * seq
    n_lat_tiles = n_lat // TM
    n_all_tiles = (n_lat + n_batch * ctx_len) // TM
    mod_rows = 8
    tile_info = (n_lat_tiles, seq // TM, n_batch, mod_rows)

    cc = jnp.concatenate([c, c_ctx[None], jnp.zeros((mod_rows - n_batch - 1, D), F32)], axis=0)
    mod3 = _ada(cc, w_ada, b_ada).reshape(depth * mod_rows, 1, N_MOD * D)

    cos_t, sin_t = _rope_tables(seq)
    f1, m2, cbd, sbd = (jnp.asarray(t, F32).astype(BF16) for t in _fft_tables())
    dft_c = jnp.asarray(_ctx_dft_table(ctx_len), F32).astype(BF16)

    w13a, w2a = ffn1_w13, ffn1_w2
    w13b, w2b = ffn2_w13, ffn2_w2
    w_in_b = w_in
    merge_w = (w_in, w_br.reshape(depth, N_BRANCH * BW, D), cbd, sbd, w_out)

    h = x.reshape(n_lat, D)
    h_ctx = ctx.reshape(n_batch * ctx_len, D)
    for l in range(depth):
        last = l == depth - 1
        bias_tab = _na_bias_table(na_bias[l], rows)

        h = _ffn(h, n_all_tiles, g_ffn1[l], mod3, l, 0, w13a, w2a, tile_info, h_ctx=h_ctx if l == 0 else None)
        qn, kn, vn, fu, qw, kw, vw = _proj(h, n_all_tiles, g_mix[l], mod3, l, w_in_b, cos_t, sin_t, tile_info)
        a = _na(qn, kn, vn, bias_tab, n_batch, seq, ctx_len)
        w = _wa(wa_sink[l], qw, kw, vw, n_batch, seq, ctx_len)
        zr, zi = _fft(fu, f1, m2, n_batch, seq)
        if last:
            h = _merge(h, n_lat_tiles, g_mix[l], mod3, l, (a, zr, zi, w), None, merge_w, tile_info)
            h = _ffn(h, n_lat_tiles, g_ffn2[l], mod3, l, 6, w13b, w2b, tile_info, g_final=g_final)
        else:
            ctx_br = _ctx_mix(wa_sink[l], qn, kn, vn, fu, qw, kw, vw, dft_c, n_batch, seq, ctx_len)
            ac, wc, zrc, zic = ctx_br
            h = _merge(h, n_all_tiles, g_mix[l], mod3, l, (a, zr, zi, w), (ac, zrc, zic, wc), merge_w, tile_info)
            h = _ffn(h, n_all_tiles, g_ffn2[l], mod3, l, 6, w13b, w2b, tile_info)
    return h.reshape(n_batch, seq, D)
```

```python
import functools

import numpy as np
import jax
import jax.numpy as jnp
from jax import lax
from jax.experimental import pallas as pl
from jax.experimental.pallas import tpu as pltpu

D = 1024
GRID_W = 64
HD = 64
NA_HEADS = 8
NA_WIN_R = 8
NA_WIN_C = 16
FN_GROUPS = 8
FN_GROUP_DIM = 64
WA_HEADS = 8
WA_KV_HEADS = 2
WA_WINDOW = 128
D_FF = 2816
N_MOD = 9
ROPE_BASE = 10000.0
EPS = 1e-6
NEG_INF = -1e30
LOG2E = 1.4426950408889634
BW = 512

LANES = 128
MXU_DIM = 256
TM = 512
TF = MXU_DIM
VMEM_LIMIT = 56 * 1024 * 1024

NA_QR = 4
NA_KR = 12
NA_RB = 2
NA_Q = NA_QR * GRID_W
NA_K = NA_KR * GRID_W
WA_Q = 128
WA_K = 3 * WA_Q
WA_QB = 8

F32 = jnp.float32
BF16 = jnp.bfloat16


def _cparams(sem):
    return pltpu.CompilerParams(dimension_semantics=sem, vmem_limit_bytes=VMEM_LIMIT)


def _const_spec(shape):
    nd = len(shape)
    return pl.BlockSpec(shape, lambda *_: (0,) * nd, pipeline_mode=pl.Buffered(1))


def _sigmoid(x):
    return 1.0 / (1.0 + jnp.exp(-x))


def _norm_mod(x, g, shift, scale):
    y = x * lax.rsqrt(jnp.mean(x * x, axis=-1, keepdims=True) + EPS)
    return y * (g * (1.0 + scale)) + shift


def _dot(a, b):
    return jnp.dot(a, b, preferred_element_type=F32)


def _dot_nt(a, b):
    return lax.dot_general(a, b, (((1,), (1,)), ((), ())), preferred_element_type=F32)


def _ada_kernel(c_ref, w_ref, b_ref, o_ref):
    x = c_ref[...]
    sx = (x * _sigmoid(x)).astype(BF16)
    o_ref[...] = _dot(sx, w_ref[...].astype(BF16)) + b_ref[...]


def _ada(cc, w_ada, b_ada):
    depth = w_ada.shape[0]
    n = w_ada.shape[2]
    tn = 1536
    rows = cc.shape[0]
    return pl.pallas_call(
        _ada_kernel,
        grid=(depth, n // tn),
        in_specs=[
            pl.BlockSpec((rows, D), lambda l, j: (0, 0)),
            pl.BlockSpec((None, D, tn), lambda l, j: (l, 0, j)),
            pl.BlockSpec((None, 1, tn), lambda l, j: (l, 0, j)),
        ],
        out_specs=pl.BlockSpec((None, rows, tn), lambda l, j: (l, 0, j)),
        out_shape=jax.ShapeDtypeStruct((depth, rows, n), F32),
        compiler_params=_cparams(("arbitrary", "arbitrary")),
        name="ada",
    )(cc, w_ada, b_ada.reshape(depth, 1, n))


def _mod_spec(layer, k, n_lat_tiles, tiles_per_batch, n_batch, mod_rows, tile_of=lambda i: i):
    def idx(i):
        t = tile_of(i)
        row = jnp.where(t < n_lat_tiles, t // tiles_per_batch, n_batch)
        return (layer * mod_rows + row, 0, k)

    return pl.BlockSpec((None, 1, D), idx)


FFN_W13_CH = 2 * TF
FFN_WSTEPS = 2 * D_FF // FFN_W13_CH
FFN_W2_CH = D_FF // FFN_WSTEPS


def _ffn_kernel(*refs, final, split_at):
    refs = list(refs)
    x_ref = refs.pop(0)
    if split_at is not None:
        xc_ref = refs.pop(0)
    g_ref, sh_ref, sc_ref, gt_ref, w13_ref, w2_ref = refs[:6]
    if final:
        gf_ref, o_ref, w13_s, w2_s, hm_ref = refs[6:]
    else:
        o_ref, w13_s, w2_s, hm_ref = refs[6:]
    i = pl.program_id(0)

    @pl.when(i < FFN_WSTEPS)
    def _():
        w13_s[i] = w13_ref[...].astype(BF16)
        w2_s[i] = w2_ref[...].astype(BF16)

    def w13_cols(lo):
        off = lo % FFN_W13_CH
        return w13_s[lo // FFN_W13_CH][:, off:off + TF]

    @pl.when(i >= FFN_WSTEPS)
    def _():
        x = x_ref[...]
        if split_at is not None:
            x = jnp.where(i - FFN_WSTEPS >= split_at, xc_ref[...], x)
        u = _norm_mod(x, g_ref[...], sh_ref[...], sc_ref[...]).astype(BF16)
        for lo in range(0, D_FF, TF):
            a = _dot(u, w13_cols(lo))
            b = _dot(u, w13_cols(D_FF + lo))
            hm_ref[:, lo:lo + TF] = (a * _sigmoid(a) * b).astype(BF16)
        f = _dot(hm_ref[...], w2_s[...].reshape(D_FF, D))
        out = x + 0.5 * gt_ref[...] * f
        if final:
            out = out * lax.rsqrt(jnp.mean(out * out, axis=-1, keepdims=True) + EPS) * gf_ref[...]
        o_ref[...] = out


def _ffn(h, n_tiles, g, mod3, layer, mod_k, w13, w2, tile_info, g_final=None, h_ctx=None):
    final = g_final is not None
    n_lat_tiles = tile_info[0]

    def tile_of(i):
        return jnp.maximum(i - FFN_WSTEPS, 0)

    def wstep_of(i):
        return jnp.minimum(i, FFN_WSTEPS - 1)

    tile_spec = pl.BlockSpec((TM, D), lambda i: (tile_of(i), 0))
    vec_spec = pl.BlockSpec((1, D), lambda i: (0, 0))
    if h_ctx is None:
        in_specs, args = [tile_spec], [h]
    else:
        in_specs = [pl.BlockSpec((TM, D), lambda i: (jnp.minimum(tile_of(i), n_lat_tiles - 1), 0)),
                    pl.BlockSpec((TM, D), lambda i: (jnp.maximum(tile_of(i) - n_lat_tiles, 0), 0))]
        args = [h, h_ctx]
    in_specs += [
        vec_spec,
        _mod_spec(layer, mod_k, *tile_info, tile_of=tile_of),
        _mod_spec(layer, mod_k + 1, *tile_info, tile_of=tile_of),
        _mod_spec(layer, mod_k + 2, *tile_info, tile_of=tile_of),
        pl.BlockSpec((None, D, FFN_W13_CH), lambda i: (layer, 0, wstep_of(i))),
        pl.BlockSpec((None, FFN_W2_CH, D), lambda i: (layer, wstep_of(i), 0)),
    ]
    args += [g.reshape(1, D), mod3, mod3, mod3, w13, w2]
    if final:
        in_specs.append(vec_spec)
        args.append(g_final.reshape(1, D))
    return pl.pallas_call(
        functools.partial(_ffn_kernel, final=final, split_at=None if h_ctx is None else n_lat_tiles),
        grid=(FFN_WSTEPS + n_tiles,),
        in_specs=in_specs,
        out_specs=tile_spec,
        out_shape=jax.ShapeDtypeStruct((n_tiles * TM, D), F32),
        scratch_shapes=[pltpu.VMEM((FFN_WSTEPS, D, FFN_W13_CH), BF16),
                        pltpu.VMEM((FFN_WSTEPS, FFN_W2_CH, D), BF16),
                        pltpu.VMEM((TM, D_FF), BF16)],
        compiler_params=_cparams(("arbitrary",)),
        name="ffn_final" if final else ("ffn" if h_ctx is None else "ffn_split"),
    )(*args)


PROJ_W = 5 * BW + 2 * LANES


def _rope(t, cos, sin_signed, first_half):
    partner = jnp.where(first_half, pltpu.roll(t, LANES - HD // 2, axis=1), pltpu.roll(t, HD // 2, axis=1))
    return t * cos + partner * sin_signed


def _dup_heads(t, lo):
    sw = pltpu.roll(t, HD, axis=1)
    return jnp.where(lo, t, sw), jnp.where(lo, sw, t)


W_IN_CH = MXU_DIM
PROJ_WSTEPS = PROJ_W // W_IN_CH


def _chunk_cols(ws_ref, lo, hi, ch):
    parts = [ws_ref[c] for c in range(lo // ch, hi // ch)]
    return parts[0] if len(parts) == 1 else jnp.concatenate(parts, axis=1)


def _proj_kernel(x_ref, g_ref, sh_ref, sc_ref, w_ref, cos_ref, sin_ref,
                 qn_ref, kn_ref, vn_ref, fu_ref, qw_ref, kw_ref, vw_ref, w_s):
    i = pl.program_id(0)

    @pl.when(i < PROJ_WSTEPS)
    def _():
        w_s[i] = w_ref[...].astype(BF16)

    def w(lo, hi):
        return _chunk_cols(w_s, lo, hi, W_IN_CH)

    @pl.when(i >= PROJ_WSTEPS)
    def _():
        u = _norm_mod(x_ref[...], g_ref[...], sh_ref[...], sc_ref[...]).astype(BF16)
        scale = HD ** -0.5 * LOG2E
        cos = cos_ref[...]
        sin = sin_ref[...]
        lane = lax.broadcasted_iota(jnp.int32, (TM, LANES), 1)
        first_half = (lane & (HD - 1)) < (HD // 2)
        lo = lane < HD
        kv = _dot(u, w(5 * BW, PROJ_W))
        k0, k1 = _dup_heads(_rope(kv[:, :LANES], cos, sin, first_half), lo)
        kw_ref[:, :LANES] = k0.astype(BF16)
        kw_ref[:, LANES:] = k1.astype(BF16)
        v0, v1 = _dup_heads(kv[:, LANES:], lo)
        vw_ref[:, :LANES] = v0.astype(BF16)
        vw_ref[:, LANES:] = v1.astype(BF16)
        wq = _dot(u, w(4 * BW, 5 * BW))
        for j in range(BW // LANES):
            sl = slice(j * LANES, (j + 1) * LANES)
            qw_ref[:, sl] = (_rope(wq[:, sl], cos, sin, first_half) * scale).astype(BF16)
        qn_ref[...] = (_dot(u, w(0, BW)) * scale).astype(BF16)
        kn_ref[...] = _dot(u, w(BW, 2 * BW)).astype(BF16)
        vn_ref[...] = _dot(u, w(2 * BW, 3 * BW)).astype(BF16)
        fu_ref[...] = _dot(u, w(3 * BW, 4 * BW)).astype(BF16)


def _proj(h, n_tiles, g, mod3, layer, w_in, cos_t, sin_t, tile_info):
    n_lat_tiles, tiles_per_batch, _, _ = tile_info

    def tile_of(i):
        return jnp.maximum(i - PROJ_WSTEPS, 0)

    def rope_blk(i):
        t = tile_of(i)
        return jnp.where(t < n_lat_tiles, t % tiles_per_batch, tiles_per_batch)

    tile_spec = pl.BlockSpec((TM, D), lambda i: (tile_of(i), 0))
    vec_spec = pl.BlockSpec((1, D), lambda i: (0, 0))
    rope_spec = pl.BlockSpec((TM, LANES), lambda i: (rope_blk(i), 0))
    w_spec = pl.BlockSpec((None, D, W_IN_CH), lambda i: (layer, 0, jnp.minimum(i, PROJ_WSTEPS - 1)))
    rows = n_tiles * TM

    def out_spec(w):
        return pl.BlockSpec((TM, w), lambda i: (tile_of(i), 0))

    widths = [BW, BW, BW, BW, BW, 2 * LANES, 2 * LANES]
    return pl.pallas_call(
        _proj_kernel,
        grid=(PROJ_WSTEPS + n_tiles,),
        in_specs=[tile_spec, vec_spec,
                  _mod_spec(layer, 3, *tile_info, tile_of=tile_of), _mod_spec(layer, 4, *tile_info, tile_of=tile_of),
                  w_spec, rope_spec, rope_spec],
        out_specs=[out_spec(w) for w in widths],
        out_shape=[jax.ShapeDtypeStruct((rows, w), BF16) for w in widths],
        scratch_shapes=[pltpu.VMEM((PROJ_WSTEPS, D, W_IN_CH), BF16)],
        compiler_params=_cparams(("arbitrary",)),
        name="proj",
    )(h, g.reshape(1, D), mod3, mod3, w_in, cos_t, sin_t)


def _lane_tiles(s):
    return [s[:, j * LANES:(j + 1) * LANES] for j in range(s.shape[1] // LANES)]


def _attend(qs, key_sets, sink_tile=None, n_slabs=1, dead=frozenset()):
    raw = [_dot_nt(qs, k) for k, _, _ in key_sets]
    rows_per_slab = qs.shape[0] // n_slabs
    p_rows = [[] for _ in key_sets]
    sink_terms = []
    for sl in range(n_slabs):
        rows = slice(sl * rows_per_slab, (sl + 1) * rows_per_slab)
        pieces = []
        for ks, (s, (_, _, bias)) in enumerate(zip(raw, key_sets)):
            for t in range(s.shape[1] // LANES):
                if (sl, ks, t) in dead:
                    continue
                cols = slice(t * LANES, (t + 1) * LANES)
                piece = s[rows, cols]
                if bias is not None:
                    piece = piece + bias[rows, cols]
                pieces.append((ks, t, piece))
        tiles = [pc for _, _, pc in pieces]
        if sink_tile is not None:
            tiles.append(sink_tile[rows])
        m = functools.reduce(jnp.maximum, tiles).max(axis=-1, keepdims=True)
        probs = {(ks, t): jnp.exp2(pc - m).astype(BF16) for ks, t, pc in pieces}
        if sink_tile is not None:
            sink_terms.append(jnp.exp2(sink_tile[rows] - m))
        for ks, s in enumerate(raw):
            blocks = [probs.get((ks, t), jnp.zeros((rows_per_slab, LANES), BF16)) for t in range(s.shape[1] // LANES)]
            p_rows[ks].append(jnp.concatenate(blocks, axis=1))
    acc = None
    for ks, (_, v, _) in enumerate(key_sets):
        v_ones = jnp.concatenate([v, jnp.ones_like(v)], axis=1)
        o = _dot(jnp.concatenate(p_rows[ks], axis=0), v_ones)
        acc = o if acc is None else acc + o
    num, den = acc[:, :LANES], acc[:, LANES:]
    if sink_tile is not None:
        den = den + (sink_terms[0] if n_slabs == 1 else jnp.concatenate(sink_terms, axis=0))
    return num / den


def _stack_heads(q, rows):
    lane = lax.broadcasted_iota(jnp.int32, (rows, LANES), 1)
    lo = lane < HD
    zero = jnp.zeros_like(q)
    return jnp.concatenate([jnp.where(lo, q, zero), jnp.where(lo, zero, q)], axis=0), lo


def _na_kernel(q_ref, k_ref, v_ref, kc_ref, vc_ref, bias_ref, o_ref, *, rows, interior_dead):
    step = pl.program_id(1)
    n_blk = rows // NA_QR

    def body(interior):
        n_slabs, dead = (NA_QR, interior_dead) if interior else (1, frozenset())
        for rb in range(NA_RB):
            j = step * NA_RB + rb
            krow = jnp.clip(NA_QR * j - NA_WIN_R // 2, 0, rows - NA_KR)
            kstart = pl.multiple_of(krow * GRID_W, GRID_W)
            pat = 1 if interior else jnp.where(j == 0, 0, jnp.where(j == n_blk - 1, 2, 1))
            qrows = slice(rb * NA_Q, (rb + 1) * NA_Q)
            for p in range(NA_HEADS // 2):
                sl = slice(p * LANES, (p + 1) * LANES)
                kb = k_ref[pl.ds(kstart, NA_K), sl]
                vb = v_ref[pl.ds(kstart, NA_K), sl]
                q = q_ref[qrows, sl]
                lo = lax.broadcasted_iota(jnp.int32, (NA_Q, LANES), 1) < HD
                outs = []
                for e in range(2):
                    qe = jnp.where(lo if e == 0 else jnp.logical_not(lo), q, jnp.zeros_like(q))
                    dead_e = frozenset((s - e * n_slabs, ks, t) for s, ks, t in dead if s // n_slabs == e)
                    outs.append(_attend(qe, [(kb, vb, bias_ref[pat, p, e * NA_Q:(e + 1) * NA_Q]),
                                             (kc_ref[:, sl], vc_ref[:, sl], None)], n_slabs=n_slabs, dead=dead_e))
                o_ref[qrows, sl] = jnp.where(lo, outs[0], outs[1]).astype(BF16)

    interior = jnp.logical_and(step > 0, step < n_blk // NA_RB - 1)

    @pl.when(interior)
    def _():
        body(True)

    @pl.when(jnp.logical_not(interior))
    def _():
        body(False)


def _na(qn, kn, vn, bias_tab, n_batch, seq, ctx_len):
    rows = seq // GRID_W
    n_steps = rows // (NA_QR * NA_RB)
    ctx_blk0 = (n_batch * seq) // ctx_len
    interior_dead = frozenset(
        (e * NA_QR + i, 0, t)
        for e in range(2) for i, pairs in enumerate(_na_plan(rows)[1]) for t, pair in enumerate(pairs)
        if pair == (NA_MASKED, NA_MASKED))
    return pl.pallas_call(
        functools.partial(_na_kernel, rows=rows, interior_dead=interior_dead),
        grid=(n_batch, n_steps),
        in_specs=[
            pl.BlockSpec((NA_RB * NA_Q, BW), lambda b, j: (b * n_steps + j, 0)),
            pl.BlockSpec((seq, BW), lambda b, j: (b, 0)),
            pl.BlockSpec((seq, BW), lambda b, j: (b, 0)),
            pl.BlockSpec((ctx_len, BW), lambda b, j: (ctx_blk0 + b, 0)),
            pl.BlockSpec((ctx_len, BW), lambda b, j: (ctx_blk0 + b, 0)),
            _const_spec(bias_tab.shape),
        ],
        out_specs=pl.BlockSpec((NA_RB * NA_Q, BW), lambda b, j: (b * n_steps + j, 0)),
        out_shape=jax.ShapeDtypeStruct((n_batch * seq, BW), BF16),
        compiler_params=_cparams(("arbitrary", "arbitrary")),
        name="na",
    )(qn, kn, vn, kn, vn, bias_tab)


NA_MASKED = 2 * NA_WIN_R - 1


def _na_plan(rows):
    n_blk = rows // NA_QR
    plan = []
    for blk in (0, 1, n_blk - 1):
        k0 = min(max(NA_QR * blk - NA_WIN_R // 2, 0), rows - NA_KR)
        per_row = []
        for i in range(NA_QR):
            r = NA_QR * blk + i
            rs = min(max(r - NA_WIN_R // 2, 0), rows - NA_WIN_R)
            slabs = [k0 + t - r + NA_WIN_R - 1 if rs <= k0 + t < rs + NA_WIN_R else NA_MASKED for t in range(NA_KR)]
            per_row.append([(slabs[2 * j], slabs[2 * j + 1]) for j in range(NA_KR // 2)])
        plan.append(per_row)
    return plan


def _na_bias_table(bias, rows):
    h = bias.shape[0]
    n_dc = 2 * NA_WIN_C - 1
    qc = np.arange(GRID_W)[:, None]
    kc = np.arange(GRID_W)[None, :]
    ws = np.clip(qc - NA_WIN_C // 2, 0, GRID_W - NA_WIN_C)
    col_ok = (kc >= ws) & (kc < ws + NA_WIN_C)
    dc = np.clip(kc - qc, -(NA_WIN_C - 1), NA_WIN_C - 1) + NA_WIN_C - 1
    onehot = (dc[None] == np.arange(n_dc)[:, None, None]).astype(np.float32)
    toep = jnp.einsum('hrd,dqk->hrqk', bias.astype(F32), jnp.asarray(onehot), precision=lax.Precision.HIGHEST)
    toep = jnp.where(col_ok[None, None], toep * LOG2E, NEG_INF)
    toep = jnp.concatenate([toep, jnp.full((h, 1, GRID_W, GRID_W), NEG_INF, F32)], axis=1)
    toep2 = jnp.concatenate([toep, toep], axis=-1)
    plan = _na_plan(rows)
    return pl.pallas_call(
        functools.partial(_bias_expand_kernel, plan=plan),
        out_shape=jax.ShapeDtypeStruct((len(plan), h // 2, 2 * NA_Q, NA_K), F32),
        compiler_params=pltpu.CompilerParams(vmem_limit_bytes=VMEM_LIMIT),
        name="na_bias_expand",
    )(toep2)


def _bias_expand_kernel(t_ref, o_ref, *, plan):
    lo = lax.broadcasted_iota(jnp.int32, (GRID_W, LANES), 1) < GRID_W
    for pat, per_row in enumerate(plan):
        for hd in range(t_ref.shape[0]):
            for i, pairs in enumerate(per_row):
                r0 = (hd % 2) * NA_Q + i * GRID_W
                for j, (da, db) in enumerate(pairs):
                    o_ref[pat, hd // 2, r0:r0 + GRID_W, j * LANES:(j + 1) * LANES] = (
                        jnp.where(lo, t_ref[hd, da], t_ref[hd, db]))


def _stack_group(q):
    rows = q.shape[0]
    a, lo = _stack_heads(q[:, :LANES], rows)
    b, _ = _stack_heads(q[:, LANES:], rows)
    return jnp.concatenate([a, b], axis=0), lo


def _unstack_group(o, rows, lo):
    oa = jnp.where(lo, o[0:rows], o[rows:2 * rows])
    ob = jnp.where(lo, o[2 * rows:3 * rows], o[3 * rows:4 * rows])
    return jnp.concatenate([oa, ob], axis=1)


def _sink_tile(sink_ref, heads, rows):
    return jnp.concatenate([jnp.full((rows, LANES), sink_ref[h] * LOG2E, F32) for h in heads], axis=0)


def _wa_kernel(sink_ref, q_ref, k_ref, v_ref, kc_ref, vc_ref, o_ref, *, seq):
    gq = WA_HEADS // WA_KV_HEADS
    for t in range(WA_QB):
        n = pl.program_id(1) * WA_QB + t
        rows = slice(t * WA_Q, (t + 1) * WA_Q)
        kstart = pl.multiple_of(jnp.clip(n * WA_Q - WA_Q, 0, seq - WA_K), WA_Q)
        qpos = n * WA_Q + lax.broadcasted_iota(jnp.int32, (WA_Q, WA_K), 0)
        kpos = kstart + lax.broadcasted_iota(jnp.int32, (WA_Q, WA_K), 1)
        band = jnp.where(jnp.abs(kpos - qpos) <= WA_WINDOW, 0.0, NEG_INF).astype(F32)
        band = jnp.concatenate([band] * gq, axis=0)
        for g in range(WA_KV_HEADS):
            sl = slice(g * LANES, (g + 1) * LANES)
            sl2 = slice(g * 2 * LANES, (g + 1) * 2 * LANES)
            kb = k_ref[pl.ds(kstart, WA_K), sl]
            vb = v_ref[pl.ds(kstart, WA_K), sl]
            qs, lo = _stack_group(q_ref[rows, sl2])
            o = _attend(qs, [(kb, vb, band), (kc_ref[:, sl], vc_ref[:, sl], None)],
                        _sink_tile(sink_ref, range(g * gq, (g + 1) * gq), WA_Q))
            o_ref[rows, sl2] = _unstack_group(o, WA_Q, lo).astype(BF16)


def _wa(sink, qw, kw, vw, n_batch, seq, ctx_len):
    n_blk = seq // (WA_Q * WA_QB)
    ctx_blk0 = (n_batch * seq) // ctx_len
    gw = 2 * LANES
    return pl.pallas_call(
        functools.partial(_wa_kernel, seq=seq),
        grid=(n_batch, n_blk),
        in_specs=[
            pl.BlockSpec(memory_space=pltpu.SMEM),
            pl.BlockSpec((WA_Q * WA_QB, BW), lambda b, n: (b * n_blk + n, 0)),
            pl.BlockSpec((seq, gw), lambda b, n: (b, 0)),
            pl.BlockSpec((seq, gw), lambda b, n: (b, 0)),
            pl.BlockSpec((ctx_len, gw), lambda b, n: (ctx_blk0 + b, 0)),
            pl.BlockSpec((ctx_len, gw), lambda b, n: (ctx_blk0 + b, 0)),
        ],
        out_specs=pl.BlockSpec((WA_Q * WA_QB, BW), lambda b, n: (b * n_blk + n, 0)),
        out_shape=jax.ShapeDtypeStruct((n_batch * seq, BW), BF16),
        compiler_params=_cparams(("arbitrary", "arbitrary")),
        name="wa",
    )(sink, qw, kw, vw, kw, vw)


FFT_BLK = 32


def _swap_major(x):
    return jnp.swapaxes(x, 0, 1)


FFT_NBLK = GRID_W // FFT_BLK


def _fft_kernel(f_ref, m_ref, x_ref, zr_ref, zi_ref, ar_s, ai_s, t_scr):
    s = pl.program_id(1)

    @pl.when(s < FFT_NBLK)
    def _():
        xt = _swap_major(x_ref[...].astype(F32)).astype(BF16)
        f = f_ref[...]
        for i in range(FFT_BLK):
            t_scr[i] = _dot(f, xt[i])
        at = _swap_major(t_scr[...])
        ar_s[s] = at[:GRID_W].astype(BF16)
        ai_s[s] = at[GRID_W:].astype(BF16)

    @pl.when(s >= FFT_NBLK)
    def _():
        k0 = (s - FFT_NBLK) * FFT_BLK
        for t in range(FFT_BLK):
            a = jnp.concatenate([ar_s[jb, k0 + t] for jb in range(FFT_NBLK)]
                                + [ai_s[jb, k0 + t] for jb in range(FFT_NBLK)], axis=0)
            t_scr[t] = _dot(m_ref[t], a)
        zt = _swap_major(t_scr[...])
        zr_ref[...] = zt[:GRID_W].astype(BF16)
        zi_ref[...] = zt[GRID_W:].astype(BF16)


def _fft_tables():
    n = GRID_W
    k = np.arange(n)
    ang1 = 2.0 * np.pi * ((k[:, None] * k[None, :]) % n) / n
    f1 = np.concatenate([np.cos(ang1), -np.sin(ang1)], axis=0) / 8.0
    ka = k[:, None, None]
    kb = k[None, :, None]
    n1 = k[None, None, :]
    ang2 = 2.0 * np.pi * ((n1 * (ka + n * kb)) % (n * n)) / (n * n)
    mr, mi = np.cos(ang2), -np.sin(ang2)
    m2 = np.concatenate([np.concatenate([mr, -mi], axis=2), np.concatenate([mi, mr], axis=2)], axis=1) / 8.0
    c = np.arange(FN_GROUP_DIM)
    angc = 2.0 * np.pi * ((c[:, None] * c[None, :]) % FN_GROUP_DIM) / FN_GROUP_DIM
    eye = np.eye(FN_GROUPS)
    cbd = np.kron(eye, np.cos(angc)) / 8.0
    sbd = np.kron(eye, np.sin(angc)) / 8.0
    return f1, m2, cbd, sbd


def _ctx_dft_table(ctx_len):
    k = np.arange(ctx_len)
    ang = 2.0 * np.pi * ((k[:, None] * k[None, :]) % ctx_len) / ctx_len
    return np.concatenate([np.cos(ang), -np.sin(ang)], axis=0) / np.sqrt(ctx_len)


def _fft(fu, f1, m2, n_batch, seq):
    n = GRID_W
    x3 = fu.reshape(-1, n, BW)
    shape3 = jax.ShapeDtypeStruct((n_batch * n, n, BW), BF16)
    z_spec = pl.BlockSpec((n, FFT_BLK, BW), lambda b, s: (b, jnp.maximum(s - FFT_NBLK, 0), 0))
    zr, zi = pl.pallas_call(
        _fft_kernel,
        grid=(n_batch, 2 * FFT_NBLK),
        in_specs=[pl.BlockSpec((2 * n, n), lambda b, s: (0, 0)),
                  pl.BlockSpec((FFT_BLK, 2 * n, 2 * n), lambda b, s: (jnp.maximum(s - FFT_NBLK, 0), 0, 0)),
                  pl.BlockSpec((n, FFT_BLK, BW), lambda b, s: (b, jnp.minimum(s, FFT_NBLK - 1), 0))],
        out_specs=[z_spec, z_spec],
        out_shape=[shape3, shape3],
        scratch_shapes=[pltpu.VMEM((FFT_NBLK, n, FFT_BLK, BW), BF16), pltpu.VMEM((FFT_NBLK, n, FFT_BLK, BW), BF16),
                        pltpu.VMEM((FFT_BLK, 2 * n, BW), F32)],
        compiler_params=_cparams(("arbitrary", "arbitrary")),
        name="fft",
    )(f1, m2, x3)
    return zr.reshape(n_batch * seq, BW), zi.reshape(n_batch * seq, BW)


def _ctx_kernel(sink_ref, qn_ref, kn_ref, vn_ref, fu_ref, qw_ref, kw_ref, vw_ref, dft_ref,
                a_ref, w_ref, zr_ref, zi_ref, *, ctx_len):
    for p in range(NA_HEADS // 2):
        sl = slice(p * LANES, (p + 1) * LANES)
        qs, lo = _stack_heads(qn_ref[:, sl], ctx_len)
        o = _attend(qs, [(kn_ref[:, sl], vn_ref[:, sl], None)])
        a_ref[:, sl] = jnp.where(lo, o[:ctx_len], o[ctx_len:]).astype(BF16)
    for g in range(WA_KV_HEADS):
        sl = slice(g * LANES, (g + 1) * LANES)
        sl2 = slice(g * 2 * LANES, (g + 1) * 2 * LANES)
        qs, lo = _stack_group(qw_ref[:, sl2])
        gq = WA_HEADS // WA_KV_HEADS
        o = _attend(qs, [(kw_ref[:, sl], vw_ref[:, sl], None)],
                    _sink_tile(sink_ref, range(g * gq, (g + 1) * gq), ctx_len))
        w_ref[:, sl2] = _unstack_group(o, ctx_len, lo).astype(BF16)
    z = _dot(dft_ref[...], fu_ref[...])
    zr_ref[...] = z[:ctx_len].astype(BF16)
    zi_ref[...] = z[ctx_len:].astype(BF16)


def _ctx_mix(sink, qn, kn, vn, fu, qw, kw, vw, dft_c, n_batch, seq, ctx_len):
    blk0 = (n_batch * seq) // ctx_len

    def in_spec(w):
        return pl.BlockSpec((ctx_len, w), lambda b: (blk0 + b, 0))

    out_spec = pl.BlockSpec((ctx_len, BW), lambda b: (b, 0))
    return pl.pallas_call(
        functools.partial(_ctx_kernel, ctx_len=ctx_len),
        grid=(n_batch,),
        in_specs=[pl.BlockSpec(memory_space=pltpu.SMEM),
                  in_spec(BW), in_spec(BW), in_spec(BW), in_spec(BW), in_spec(BW),
                  in_spec(2 * LANES), in_spec(2 * LANES),
                  pl.BlockSpec(dft_c.shape, lambda b: (0, 0))],
        out_specs=[out_spec] * 4,
        out_shape=[jax.ShapeDtypeStruct((n_batch * ctx_len, BW), BF16)] * 4,
        compiler_params=_cparams(("arbitrary",)),
        name="ctx_mix",
    )(sink, qn, kn, vn, fu, qw, kw, vw, dft_c)


N_BRANCH = 3
MERGE_WSTEPS = N_BRANCH * D // W_IN_CH
MERGE_BR_CH = N_BRANCH * BW // MERGE_WSTEPS
MERGE_OUT_CH = MERGE_BR_CH
MERGE_OUT_STEPS = D // MERGE_OUT_CH


def _merge_kernel(x_ref, g_ref, sh_ref, sc_ref, gt_ref, a_ref, zr_ref, zi_ref, w_ref, *rest, n_lat_tiles):
    if n_lat_tiles is None:
        wg_ref, wbr_ref, cbd_ref, sbd_ref, wout_ref, o_ref, wg_s, wbr_s, wout_s = rest
    else:
        (ac_ref, zrc_ref, zic_ref, wc_ref, wg_ref, wbr_ref, cbd_ref, sbd_ref, wout_ref, o_ref,
         wg_s, wbr_s, wout_s) = rest
    i = pl.program_id(0)

    @pl.when(i < MERGE_WSTEPS)
    def _():
        wg_s[i] = wg_ref[...].astype(BF16)
        wbr_s[i] = wbr_ref[...].astype(BF16)

    @pl.when(i < MERGE_OUT_STEPS)
    def _():
        wout_s[i] = wout_ref[...].astype(BF16)

    @pl.when(i >= MERGE_WSTEPS)
    def _():
        if n_lat_tiles is None:
            a, zr, zi, w = a_ref[...], zr_ref[...], zi_ref[...], w_ref[...]
        else:
            is_ctx = i - MERGE_WSTEPS >= n_lat_tiles
            a = jnp.where(is_ctx, ac_ref[...], a_ref[...])
            zr = jnp.where(is_ctx, zrc_ref[...], zr_ref[...])
            zi = jnp.where(is_ctx, zic_ref[...], zi_ref[...])
            w = jnp.where(is_ctx, wc_ref[...], w_ref[...])
        x = x_ref[...]
        u = _norm_mod(x, g_ref[...], sh_ref[...], sc_ref[...]).astype(BF16)
        f = (_dot(zr, cbd_ref[...]) + _dot(zi, sbd_ref[...])).astype(BF16)
        w_br = wbr_s[...].reshape(N_BRANCH * BW, D)
        acc = None
        for b, br in enumerate((a, f, w)):
            gate = _sigmoid(_dot(u, _chunk_cols(wg_s, b * D, (b + 1) * D, W_IN_CH)))
            term = gate * _dot(br, w_br[b * BW:(b + 1) * BW])
            acc = term if acc is None else acc + term
        o_ref[...] = x + gt_ref[...] * _dot(acc.astype(BF16), wout_s[...].reshape(D, D))


def _merge(h, n_tiles, g, mod3, layer, branches, ctx_branches, weights, tile_info):
    n_lat_tiles = tile_info[0]

    def tile_of(i):
        return jnp.maximum(i - MERGE_WSTEPS, 0)

    def wstep_of(i):
        return jnp.minimum(i, MERGE_WSTEPS - 1)

    tile_spec = pl.BlockSpec((TM, D), lambda i: (tile_of(i), 0))
    vec_spec = pl.BlockSpec((1, D), lambda i: (0, 0))
    has_ctx = ctx_branches is not None
    lat_spec = pl.BlockSpec((TM, BW), lambda i: (jnp.minimum(tile_of(i), n_lat_tiles - 1), 0))
    ctx_spec = pl.BlockSpec((TM, BW), lambda i: (jnp.maximum(tile_of(i) - n_lat_tiles, 0), 0))
    in_specs = [tile_spec, vec_spec] + [_mod_spec(layer, k, *tile_info, tile_of=tile_of) for k in (3, 4, 5)]
    in_specs += [lat_spec] * 4
    args = [h, g.reshape(1, D), mod3, mod3, mod3, *branches]
    if has_ctx:
        in_specs += [ctx_spec] * 4
        args += list(ctx_branches)
    w_in, w_br, cbd, sbd, w_out = weights
    gate_blk0 = PROJ_W // W_IN_CH
    in_specs += [
        pl.BlockSpec((None, D, W_IN_CH), lambda i: (layer, 0, gate_blk0 + wstep_of(i))),
        pl.BlockSpec((None, MERGE_BR_CH, D), lambda i: (layer, wstep_of(i), 0)),
        _const_spec(cbd.shape), _const_spec(sbd.shape),
        pl.BlockSpec((None, MERGE_OUT_CH, D), lambda i: (layer, jnp.minimum(i, MERGE_OUT_STEPS - 1), 0)),
    ]
    args += list(weights)
    return pl.pallas_call(
        functools.partial(_merge_kernel, n_lat_tiles=n_lat_tiles if has_ctx else None),
        grid=(MERGE_WSTEPS + n_tiles,),
        in_specs=in_specs,
        out_specs=tile_spec,
        out_shape=jax.ShapeDtypeStruct((n_tiles * TM, D), F32),
        scratch_shapes=[pltpu.VMEM((MERGE_WSTEPS, D, W_IN_CH), BF16),
                        pltpu.VMEM((MERGE_WSTEPS, MERGE_BR_CH, D), BF16),
                        pltpu.VMEM((MERGE_OUT_STEPS, MERGE_OUT_CH, D), BF16)],
        compiler_params=_cparams(("arbitrary",)),
        name="merge_ctx" if has_ctx else "merge",
    )(*args)


def _rope_tables(seq):
    t = np.arange(seq)
    row = (t // GRID_W).astype(np.float64)
    col = (t % GRID_W).astype(np.float64)
    n_freq = HD // 4
    inv = ROPE_BASE ** (-np.arange(n_freq, dtype=np.float64) / n_freq)
    ang = np.concatenate([row[:, None] * inv, col[:, None] * inv], axis=-1)
    cos, sin = np.cos(ang), np.sin(ang)
    cos_h = np.concatenate([cos, cos], axis=1)
    sin_h = np.concatenate([-sin, sin], axis=1)
    cos2 = np.concatenate([np.tile(cos_h, (1, 2)), np.ones((TM, LANES), np.float32)], axis=0)
    sin2 = np.concatenate([np.tile(sin_h, (1, 2)), np.zeros((TM, LANES), np.float32)], axis=0)
    return jnp.asarray(cos2, F32), jnp.asarray(sin2, F32)


def kernel(x, c, ctx, c_ctx, w_ada, b_ada, g_ffn1, ffn1_w13, ffn1_w2, g_mix, w_in, na_bias, wa_sink,
           w_br, w_out, g_ffn2, ffn2_w13, ffn2_w2, g_final):
    n_batch, seq, _ = x.shape
    ctx_len = ctx.shape[1]
    depth = w_ada.shape[0]
    rows = seq // GRID_W
    n_lat = n_batch * seq
    n_lat_tiles = n_lat // TM
    n_all_tiles = (n_lat + n_batch * ctx_len) // TM
    mod_rows = 8
    tile_info = (n_lat_tiles, seq // TM, n_batch, mod_rows)

    cc = jnp.concatenate([c, c_ctx[None], jnp.zeros((mod_rows - n_batch - 1, D), F32)], axis=0)
    mod3 = _ada(cc, w_ada, b_ada).reshape(depth * mod_rows, 1, N_MOD * D)

    cos_t, sin_t = _rope_tables(seq)
    f1, m2, cbd, sbd = (jnp.asarray(t, F32).astype(BF16) for t in _fft_tables())
    dft_c = jnp.asarray(_ctx_dft_table(ctx_len), F32).astype(BF16)

    w13a, w2a = ffn1_w13, ffn1_w2
    w13b, w2b = ffn2_w13, ffn2_w2
    w_in_b = w_in
    merge_w = (w_in, w_br.reshape(depth, N_BRANCH * BW, D), cbd, sbd, w_out)

    h = x.reshape(n_lat, D)
    h_ctx = ctx.reshape(n_batch * ctx_len, D)
    for l in range(depth):
        last = l == depth - 1
        bias_tab = _na_bias_table(na_bias[l], rows)

        h = _ffn(h, n_all_tiles, g_ffn1[l], mod3, l, 0, w13a, w2a, tile_info, h_ctx=h_ctx if l == 0 else None)
        qn, kn, vn, fu, qw, kw, vw = _proj(h, n_all_tiles, g_mix[l], mod3, l, w_in_b, cos_t, sin_t, tile_info)
        a = _na(qn, kn, vn, bias_tab, n_batch, seq, ctx_len)
        w = _wa(wa_sink[l], qw, kw, vw, n_batch, seq, ctx_len)
        zr, zi = _fft(fu, f1, m2, n_batch, seq)
        if last:
            h = _merge(h, n_lat_tiles, g_mix[l], mod3, l, (a, zr, zi, w), None, merge_w, tile_info)
            h = _ffn(h, n_lat_tiles, g_ffn2[l], mod3, l, 6, w13b, w2b, tile_info, g_final=g_final)
        else:
            ctx_br = _ctx_mix(wa_sink[l], qn, kn, vn, fu, qw, kw, vw, dft_c, n_batch, seq, ctx_len)
            ac, wc, zrc, zic = ctx_br
            h = _merge(h, n_all_tiles, g_mix[l], mod3, l, (a, zr, zi, w), (ac, zrc, zic, wc), merge_w, tile_info)
            h = _ffn(h, n_all_tiles, g_ffn2[l], mod3, l, 6, w13b, w2b, tile_info)
    return h.reshape(n_batch, seq, D)
```

```python
import functools

import numpy as np
import jax
import jax.numpy as jnp
from jax import lax
from jax.experimental import pallas as pl
from jax.experimental.pallas import tpu as pltpu

D = 1024
GRID_W = 64
HD = 64
NA_HEADS = 8
NA_WIN_R = 8
NA_WIN_C = 16
FN_GROUPS = 8
FN_GROUP_DIM = 64
WA_HEADS = 8
WA_KV_HEADS = 2
WA_WINDOW = 128
D_FF = 2816
N_MOD = 9
ROPE_BASE = 10000.0
EPS = 1e-6
NEG_INF = -1e30
LOG2E = 1.4426950408889634
BW = 512

LANES = 128
MXU_DIM = 256
TM = 512
TF = MXU_DIM
VMEM_LIMIT = 56 * 1024 * 1024

NA_QR = 4
NA_KR = 12
NA_RB = 4
NA_Q = NA_QR * GRID_W
NA_K = NA_KR * GRID_W
WA_Q = 128
WA_K = 3 * WA_Q
WA_QB = 16

F32 = jnp.float32
BF16 = jnp.bfloat16


def _cparams(sem):
    return pltpu.CompilerParams(dimension_semantics=sem, vmem_limit_bytes=VMEM_LIMIT)


def _const_spec(shape):
    nd = len(shape)
    return pl.BlockSpec(shape, lambda *_: (0,) * nd, pipeline_mode=pl.Buffered(1))


def _sigmoid(x):
    return 1.0 / (1.0 + jnp.exp(-x))


def _norm_mod(x, g, shift, scale):
    y = x * lax.rsqrt(jnp.mean(x * x, axis=-1, keepdims=True) + EPS)
    return y * (g * (1.0 + scale)) + shift


def _dot(a, b):
    return jnp.dot(a, b, preferred_element_type=F32)


def _dot_nt(a, b):
    return lax.dot_general(a, b, (((1,), (1,)), ((), ())), preferred_element_type=F32)


def _ada_kernel(c_ref, w_ref, b_ref, o_ref):
    x = c_ref[...]
    sx = (x * _sigmoid(x)).astype(BF16)
    o_ref[...] = _dot(sx, w_ref[...].astype(BF16)) + b_ref[...]


def _ada(cc, w_ada, b_ada):
    depth = w_ada.shape[0]
    n = w_ada.shape[2]
    tn = 1536
    rows = cc.shape[0]
    return pl.pallas_call(
        _ada_kernel,
        grid=(depth, n // tn),
        in_specs=[
            pl.BlockSpec((rows, D), lambda l, j: (0, 0)),
            pl.BlockSpec((None, D, tn), lambda l, j: (l, 0, j)),
            pl.BlockSpec((None, 1, tn), lambda l, j: (l, 0, j)),
        ],
        out_specs=pl.BlockSpec((None, rows, tn), lambda l, j: (l, 0, j)),
        out_shape=jax.ShapeDtypeStruct((depth, rows, n), F32),
        compiler_params=_cparams(("arbitrary", "arbitrary")),
        name="ada",
    )(cc, w_ada, b_ada.reshape(depth, 1, n))


def _mod_spec(layer, k, n_lat_tiles, tiles_per_batch, n_batch, mod_rows, tile_of=lambda i: i):
    def idx(i):
        t = tile_of(i)
        row = jnp.where(t < n_lat_tiles, t // tiles_per_batch, n_batch)
        return (layer * mod_rows + row, 0, k)

    return pl.BlockSpec((None, 1, D), idx)


FFN_W13_CH = 2 * TF
FFN_WSTEPS = 2 * D_FF // FFN_W13_CH
FFN_W2_CH = D_FF // FFN_WSTEPS


def _ffn_kernel(*refs, final, split_at):
    refs = list(refs)
    x_ref = refs.pop(0)
    if split_at is not None:
        xc_ref = refs.pop(0)
    g_ref, sh_ref, sc_ref, gt_ref, w13_ref, w2_ref = refs[:6]
    if final:
        gf_ref, o_ref, w13_s, w2_s, hm_ref = refs[6:]
    else:
        o_ref, w13_s, w2_s, hm_ref = refs[6:]
    i = pl.program_id(0)

    @pl.when(i < FFN_WSTEPS)
    def _():
        w13_s[i] = w13_ref[...].astype(BF16)
        w2_s[i] = w2_ref[...].astype(BF16)

    def w13_cols(lo):
        off = lo % FFN_W13_CH
        return w13_s[lo // FFN_W13_CH][:, off:off + TF]

    @pl.when(i >= FFN_WSTEPS)
    def _():
        x = x_ref[...]
        if split_at is not None:
            x = jnp.where(i - FFN_WSTEPS >= split_at, xc_ref[...], x)
        u = _norm_mod(x, g_ref[...], sh_ref[...], sc_ref[...]).astype(BF16)
        for lo in range(0, D_FF, TF):
            a = _dot(u, w13_cols(lo))
            b = _dot(u, w13_cols(D_FF + lo))
            hm_ref[:, lo:lo + TF] = (a * _sigmoid(a) * b).astype(BF16)
        f = _dot(hm_ref[...], w2_s[...].reshape(D_FF, D))
        out = x + 0.5 * gt_ref[...] * f
        if final:
            out = out * lax.rsqrt(jnp.mean(out * out, axis=-1, keepdims=True) + EPS) * gf_ref[...]
        o_ref[...] = out


def _ffn(h, n_tiles, g, mod3, layer, mod_k, w13, w2, tile_info, g_final=None, h_ctx=None):
    final = g_final is not None
    n_lat_tiles = tile_info[0]

    def tile_of(i):
        return jnp.maximum(i - FFN_WSTEPS, 0)

    def wstep_of(i):
        return jnp.minimum(i, FFN_WSTEPS - 1)

    tile_spec = pl.BlockSpec((TM, D), lambda i: (tile_of(i), 0))
    vec_spec = pl.BlockSpec((1, D), lambda i: (0, 0))
    if h_ctx is None:
        in_specs, args = [tile_spec], [h]
    else:
        in_specs = [pl.BlockSpec((TM, D), lambda i: (jnp.minimum(tile_of(i), n_lat_tiles - 1), 0)),
                    pl.BlockSpec((TM, D), lambda i: (jnp.maximum(tile_of(i) - n_lat_tiles, 0), 0))]
        args = [h, h_ctx]
    in_specs += [
        vec_spec,
        _mod_spec(layer, mod_k, *tile_info, tile_of=tile_of),
        _mod_spec(layer, mod_k + 1, *tile_info, tile_of=tile_of),
        _mod_spec(layer, mod_k + 2, *tile_info, tile_of=tile_of),
        pl.BlockSpec((None, D, FFN_W13_CH), lambda i: (layer, 0, wstep_of(i))),
        pl.BlockSpec((None, FFN_W2_CH, D), lambda i: (layer, wstep_of(i), 0)),
    ]
    args += [g.reshape(1, D), mod3, mod3, mod3, w13, w2]
    if final:
        in_specs.append(vec_spec)
        args.append(g_final.reshape(1, D))
    return pl.pallas_call(
        functools.partial(_ffn_kernel, final=final, split_at=None if h_ctx is None else n_lat_tiles),
        grid=(FFN_WSTEPS + n_tiles,),
        in_specs=in_specs,
        out_specs=tile_spec,
        out_shape=jax.ShapeDtypeStruct((n_tiles * TM, D), F32),
        scratch_shapes=[pltpu.VMEM((FFN_WSTEPS, D, FFN_W13_CH), BF16),
                        pltpu.VMEM((FFN_WSTEPS, FFN_W2_CH, D), BF16),
                        pltpu.VMEM((TM, D_FF), BF16)],
        compiler_params=_cparams(("arbitrary",)),
        name="ffn_final" if final else ("ffn" if h_ctx is None else "ffn_split"),
    )(*args)


PROJ_W = 5 * BW + 2 * LANES


def _rope(t, cos, sin_signed, first_half):
    partner = jnp.where(first_half, pltpu.roll(t, LANES - HD // 2, axis=1), pltpu.roll(t, HD // 2, axis=1))
    return t * cos + partner * sin_signed


def _dup_heads(t, lo):
    sw = pltpu.roll(t, HD, axis=1)
    return jnp.where(lo, t, sw), jnp.where(lo, sw, t)


W_IN_CH = MXU_DIM
PROJ_WSTEPS = PROJ_W // W_IN_CH


def _chunk_cols(ws_ref, lo, hi, ch):
    parts = [ws_ref[c] for c in range(lo // ch, hi // ch)]
    return parts[0] if len(parts) == 1 else jnp.concatenate(parts, axis=1)


def _proj_kernel(x_ref, g_ref, sh_ref, sc_ref, w_ref, cos_ref, sin_ref,
                 qn_ref, kn_ref, vn_ref, fu_ref, qw_ref, kw_ref, vw_ref, w_s):
    i = pl.program_id(0)

    @pl.when(i < PROJ_WSTEPS)
    def _():
        w_s[i] = w_ref[...].astype(BF16)

    def w(lo, hi):
        return _chunk_cols(w_s, lo, hi, W_IN_CH)

    @pl.when(i >= PROJ_WSTEPS)
    def _():
        u = _norm_mod(x_ref[...], g_ref[...], sh_ref[...], sc_ref[...]).astype(BF16)
        scale = HD ** -0.5 * LOG2E
        cos = cos_ref[...]
        sin = sin_ref[...]
        lane = lax.broadcasted_iota(jnp.int32, (TM, LANES), 1)
        first_half = (lane & (HD - 1)) < (HD // 2)
        lo = lane < HD
        kv = _dot(u, w(5 * BW, PROJ_W))
        k0, k1 = _dup_heads(_rope(kv[:, :LANES], cos, sin, first_half), lo)
        kw_ref[:, :LANES] = k0.astype(BF16)
        kw_ref[:, LANES:] = k1.astype(BF16)
        v0, v1 = _dup_heads(kv[:, LANES:], lo)
        vw_ref[:, :LANES] = v0.astype(BF16)
        vw_ref[:, LANES:] = v1.astype(BF16)
        wq = _dot(u, w(4 * BW, 5 * BW))
        for j in range(BW // LANES):
            sl = slice(j * LANES, (j + 1) * LANES)
            qw_ref[:, sl] = (_rope(wq[:, sl], cos, sin, first_half) * scale).astype(BF16)
        qn_ref[...] = (_dot(u, w(0, BW)) * scale).astype(BF16)
        kn_ref[...] = _dot(u, w(BW, 2 * BW)).astype(BF16)
        vn_ref[...] = _dot(u, w(2 * BW, 3 * BW)).astype(BF16)
        fu_ref[...] = _dot(u, w(3 * BW, 4 * BW)).astype(BF16)


def _proj(h, n_tiles, g, mod3, layer, w_in, cos_t, sin_t, tile_info):
    n_lat_tiles, tiles_per_batch, _, _ = tile_info

    def tile_of(i):
        return jnp.maximum(i - PROJ_WSTEPS, 0)

    def rope_blk(i):
        t = tile_of(i)
        return jnp.where(t < n_lat_tiles, t % tiles_per_batch, tiles_per_batch)

    tile_spec = pl.BlockSpec((TM, D), lambda i: (tile_of(i), 0))
    vec_spec = pl.BlockSpec((1, D), lambda i: (0, 0))
    rope_spec = pl.BlockSpec((TM, LANES), lambda i: (rope_blk(i), 0))
    w_spec = pl.BlockSpec((None, D, W_IN_CH), lambda i: (layer, 0, jnp.minimum(i, PROJ_WSTEPS - 1)))
    rows = n_tiles * TM

    def out_spec(w):
        return pl.BlockSpec((TM, w), lambda i: (tile_of(i), 0))

    widths = [BW, BW, BW, BW, BW, 2 * LANES, 2 * LANES]
    return pl.pallas_call(
        _proj_kernel,
        grid=(PROJ_WSTEPS + n_tiles,),
        in_specs=[tile_spec, vec_spec,
                  _mod_spec(layer, 3, *tile_info, tile_of=tile_of), _mod_spec(layer, 4, *tile_info, tile_of=tile_of),
                  w_spec, rope_spec, rope_spec],
        out_specs=[out_spec(w) for w in widths],
        out_shape=[jax.ShapeDtypeStruct((rows, w), BF16) for w in widths],
        scratch_shapes=[pltpu.VMEM((PROJ_WSTEPS, D, W_IN_CH), BF16)],
        compiler_params=_cparams(("arbitrary",)),
        name="proj",
    )(h, g.reshape(1, D), mod3, mod3, w_in, cos_t, sin_t)


def _lane_tiles(s):
    return [s[:, j * LANES:(j + 1) * LANES] for j in range(s.shape[1] // LANES)]


def _attend(qs, key_sets, sink_tile=None, n_slabs=1, dead=frozenset()):
    raw = [_dot_nt(qs, k) for k, _, _ in key_sets]
    rows_per_slab = qs.shape[0] // n_slabs
    p_rows = [[] for _ in key_sets]
    sink_terms = []
    for sl in range(n_slabs):
        rows = slice(sl * rows_per_slab, (sl + 1) * rows_per_slab)
        pieces = []
        for ks, (s, (_, _, bias)) in enumerate(zip(raw, key_sets)):
            for t in range(s.shape[1] // LANES):
                if (sl, ks, t) in dead:
                    continue
                cols = slice(t * LANES, (t + 1) * LANES)
                piece = s[rows, cols]
                if bias is not None:
                    piece = piece + bias[rows, cols]
                pieces.append((ks, t, piece))
        tiles = [pc for _, _, pc in pieces]
        if sink_tile is not None:
            tiles.append(sink_tile[rows])
        m = functools.reduce(jnp.maximum, tiles).max(axis=-1, keepdims=True)
        probs = {(ks, t): jnp.exp2(pc - m).astype(BF16) for ks, t, pc in pieces}
        if sink_tile is not None:
            sink_terms.append(jnp.exp2(sink_tile[rows] - m))
        for ks, s in enumerate(raw):
            blocks = [probs.get((ks, t), jnp.zeros((rows_per_slab, LANES), BF16)) for t in range(s.shape[1] // LANES)]
            p_rows[ks].append(jnp.concatenate(blocks, axis=1))
    acc = None
    for ks, (_, v, _) in enumerate(key_sets):
        v_ones = jnp.concatenate([v, jnp.ones_like(v)], axis=1)
        o = _dot(jnp.concatenate(p_rows[ks], axis=0), v_ones)
        acc = o if acc is None else acc + o
    num, den = acc[:, :LANES], acc[:, LANES:]
    if sink_tile is not None:
        den = den + (sink_terms[0] if n_slabs == 1 else jnp.concatenate(sink_terms, axis=0))
    return num / den


def _stack_heads(q, rows):
    lane = lax.broadcasted_iota(jnp.int32, (rows, LANES), 1)
    lo = lane < HD
    zero = jnp.zeros_like(q)
    return jnp.concatenate([jnp.where(lo, q, zero), jnp.where(lo, zero, q)], axis=0), lo


def _na_kernel(q_ref, k_ref, v_ref, kc_ref, vc_ref, bias_ref, o_ref, *, rows, interior_dead):
    step = pl.program_id(1)
    n_blk = rows // NA_QR

    def body(interior):
        n_slabs, dead = (NA_QR, interior_dead) if interior else (1, frozenset())
        for rb in range(NA_RB):
            j = step * NA_RB + rb
            krow = jnp.clip(NA_QR * j - NA_WIN_R // 2, 0, rows - NA_KR)
            kstart = pl.multiple_of(krow * GRID_W, GRID_W)
            pat = 1 if interior else jnp.where(j == 0, 0, jnp.where(j == n_blk - 1, 2, 1))
            qrows = slice(rb * NA_Q, (rb + 1) * NA_Q)
            for p in range(NA_HEADS // 2):
                sl = slice(p * LANES, (p + 1) * LANES)
                kb = k_ref[pl.ds(kstart, NA_K), sl]
                vb = v_ref[pl.ds(kstart, NA_K), sl]
                q = q_ref[qrows, sl]
                lo = lax.broadcasted_iota(jnp.int32, (NA_Q, LANES), 1) < HD
                outs = []
                for e in range(2):
                    qe = jnp.where(lo if e == 0 else jnp.logical_not(lo), q, jnp.zeros_like(q))
                    dead_e = frozenset((s - e * n_slabs, ks, t) for s, ks, t in dead if s // n_slabs == e)
                    outs.append(_attend(qe, [(kb, vb, bias_ref[pat, p, e * NA_Q:(e + 1) * NA_Q]),
                                             (kc_ref[:, sl], vc_ref[:, sl], None)], n_slabs=n_slabs, dead=dead_e))
                o_ref[qrows, sl] = jnp.where(lo, outs[0], outs[1]).astype(BF16)

    interior = jnp.logical_and(step > 0, step < n_blk // NA_RB - 1)

    @pl.when(interior)
    def _():
        body(True)

    @pl.when(jnp.logical_not(interior))
    def _():
        body(False)


def _na(qn, kn, vn, bias_tab, n_batch, seq, ctx_len):
    rows = seq // GRID_W
    n_steps = rows // (NA_QR * NA_RB)
    ctx_blk0 = (n_batch * seq) // ctx_len
    interior_dead = frozenset(
        (e * NA_QR + i, 0, t)
        for e in range(2) for i, pairs in enumerate(_na_plan(rows)[1]) for t, pair in enumerate(pairs)
        if pair == (NA_MASKED, NA_MASKED))
    return pl.pallas_call(
        functools.partial(_na_kernel, rows=rows, interior_dead=interior_dead),
        grid=(n_batch, n_steps),
        in_specs=[
            pl.BlockSpec((NA_RB * NA_Q, BW), lambda b, j: (b * n_steps + j, 0)),
            pl.BlockSpec((seq, BW), lambda b, j: (b, 0)),
            pl.BlockSpec((seq, BW), lambda b, j: (b, 0)),
            pl.BlockSpec((ctx_len, BW), lambda b, j: (ctx_blk0 + b, 0)),
            pl.BlockSpec((ctx_len, BW), lambda b, j: (ctx_blk0 + b, 0)),
            _const_spec(bias_tab.shape),
        ],
        out_specs=pl.BlockSpec((NA_RB * NA_Q, BW), lambda b, j: (b * n_steps + j, 0)),
        out_shape=jax.ShapeDtypeStruct((n_batch * seq, BW), BF16),
        compiler_params=_cparams(("arbitrary", "arbitrary")),
        name="na",
    )(qn, kn, vn, kn, vn, bias_tab)


NA_MASKED = 2 * NA_WIN_R - 1


def _na_plan(rows):
    n_blk = rows // NA_QR
    plan = []
    for blk in (0, 1, n_blk - 1):
        k0 = min(max(NA_QR * blk - NA_WIN_R // 2, 0), rows - NA_KR)
        per_row = []
        for i in range(NA_QR):
            r = NA_QR * blk + i
            rs = min(max(r - NA_WIN_R // 2, 0), rows - NA_WIN_R)
            slabs = [k0 + t - r + NA_WIN_R - 1 if rs <= k0 + t < rs + NA_WIN_R else NA_MASKED for t in range(NA_KR)]
            per_row.append([(slabs[2 * j], slabs[2 * j + 1]) for j in range(NA_KR // 2)])
        plan.append(per_row)
    return plan


def _na_bias_table(bias, rows):
    h = bias.shape[0]
    n_dc = 2 * NA_WIN_C - 1
    qc = np.arange(GRID_W)[:, None]
    kc = np.arange(GRID_W)[None, :]
    ws = np.clip(qc - NA_WIN_C // 2, 0, GRID_W - NA_WIN_C)
    col_ok = (kc >= ws) & (kc < ws + NA_WIN_C)
    dc = np.clip(kc - qc, -(NA_WIN_C - 1), NA_WIN_C - 1) + NA_WIN_C - 1
    onehot = (dc[None] == np.arange(n_dc)[:, None, None]).astype(np.float32)
    toep = jnp.einsum('hrd,dqk->hrqk', bias.astype(F32), jnp.asarray(onehot), precision=lax.Precision.HIGHEST)
    toep = jnp.where(col_ok[None, None], toep * LOG2E, NEG_INF)
    toep = jnp.concatenate([toep, jnp.full((h, 1, GRID_W, GRID_W), NEG_INF, F32)], axis=1)
    toep2 = jnp.concatenate([toep, toep], axis=-1)
    plan = _na_plan(rows)
    return pl.pallas_call(
        functools.partial(_bias_expand_kernel, plan=plan),
        out_shape=jax.ShapeDtypeStruct((len(plan), h // 2, 2 * NA_Q, NA_K), F32),
        compiler_params=pltpu.CompilerParams(vmem_limit_bytes=VMEM_LIMIT),
        name="na_bias_expand",
    )(toep2)


def _bias_expand_kernel(t_ref, o_ref, *, plan):
    lo = lax.broadcasted_iota(jnp.int32, (GRID_W, LANES), 1) < GRID_W
    for pat, per_row in enumerate(plan):
        for hd in range(t_ref.shape[0]):
            for i, pairs in enumerate(per_row):
                r0 = (hd % 2) * NA_Q + i * GRID_W
                for j, (da, db) in enumerate(pairs):
                    o_ref[pat, hd // 2, r0:r0 + GRID_W, j * LANES:(j + 1) * LANES] = (
                        jnp.where(lo, t_ref[hd, da], t_ref[hd, db]))


def _stack_group(q):
    rows = q.shape[0]
    a, lo = _stack_heads(q[:, :LANES], rows)
    b, _ = _stack_heads(q[:, LANES:], rows)
    return jnp.concatenate([a, b], axis=0), lo


def _unstack_group(o, rows, lo):
    oa = jnp.where(lo, o[0:rows], o[rows:2 * rows])
    ob = jnp.where(lo, o[2 * rows:3 * rows], o[3 * rows:4 * rows])
    return jnp.concatenate([oa, ob], axis=1)


def _sink_tile(sink_ref, heads, rows):
    return jnp.concatenate([jnp.full((rows, LANES), sink_ref[h] * LOG2E, F32) for h in heads], axis=0)


def _wa_kernel(sink_ref, q_ref, k_ref, v_ref, kc_ref, vc_ref, o_ref, *, seq):
    gq = WA_HEADS // WA_KV_HEADS
    for t in range(WA_QB):
        n = pl.program_id(1) * WA_QB + t
        rows = slice(t * WA_Q, (t + 1) * WA_Q)
        kstart = pl.multiple_of(jnp.clip(n * WA_Q - WA_Q, 0, seq - WA_K), WA_Q)
        qpos = n * WA_Q + lax.broadcasted_iota(jnp.int32, (WA_Q, WA_K), 0)
        kpos = kstart + lax.broadcasted_iota(jnp.int32, (WA_Q, WA_K), 1)
        band = jnp.where(jnp.abs(kpos - qpos) <= WA_WINDOW, 0.0, NEG_INF).astype(F32)
        band = jnp.concatenate([band] * gq, axis=0)
        for g in range(WA_KV_HEADS):
            sl = slice(g * LANES, (g + 1) * LANES)
            sl2 = slice(g * 2 * LANES, (g + 1) * 2 * LANES)
            kb = k_ref[pl.ds(kstart, WA_K), sl]
            vb = v_ref[pl.ds(kstart, WA_K), sl]
            qs, lo = _stack_group(q_ref[rows, sl2])
            o = _attend(qs, [(kb, vb, band), (kc_ref[:, sl], vc_ref[:, sl], None)],
                        _sink_tile(sink_ref, range(g * gq, (g + 1) * gq), WA_Q))
            o_ref[rows, sl2] = _unstack_group(o, WA_Q, lo).astype(BF16)


def _wa(sink, qw, kw, vw, n_batch, seq, ctx_len):
    n_blk = seq // (WA_Q * WA_QB)
    ctx_blk0 = (n_batch * seq) // ctx_len
    gw = 2 * LANES
    return pl.pallas_call(
        functools.partial(_wa_kernel, seq=seq),
        grid=(n_batch, n_blk),
        in_specs=[
            pl.BlockSpec(memory_space=pltpu.SMEM),
            pl.BlockSpec((WA_Q * WA_QB, BW), lambda b, n: (b * n_blk + n, 0)),
            pl.BlockSpec((seq, gw), lambda b, n: (b, 0)),
            pl.BlockSpec((seq, gw), lambda b, n: (b, 0)),
            pl.BlockSpec((ctx_len, gw), lambda b, n: (ctx_blk0 + b, 0)),
            pl.BlockSpec((ctx_len, gw), lambda b, n: (ctx_blk0 + b, 0)),
        ],
        out_specs=pl.BlockSpec((WA_Q * WA_QB, BW), lambda b, n: (b * n_blk + n, 0)),
        out_shape=jax.ShapeDtypeStruct((n_batch * seq, BW), BF16),
        compiler_params=_cparams(("arbitrary", "arbitrary")),
        name="wa",
    )(sink, qw, kw, vw, kw, vw)


FFT_BLK = 32


def _swap_major(x):
    return jnp.swapaxes(x, 0, 1)


FFT_NBLK = GRID_W // FFT_BLK


def _fft_kernel(f_ref, m_ref, x_ref, zr_ref, zi_ref, ar_s, ai_s, t_scr):
    s = pl.program_id(1)

    @pl.when(s < FFT_NBLK)
    def _():
        xt = _swap_major(x_ref[...].astype(F32)).astype(BF16)
        f = f_ref[...]
        for i in range(FFT_BLK):
            t_scr[i] = _dot(f, xt[i])
        at = _swap_major(t_scr[...])
        ar_s[s] = at[:GRID_W].astype(BF16)
        ai_s[s] = at[GRID_W:].astype(BF16)

    @pl.when(s >= FFT_NBLK)
    def _():
        k0 = (s - FFT_NBLK) * FFT_BLK
        for t in range(FFT_BLK):
            a = jnp.concatenate([ar_s[jb, k0 + t] for jb in range(FFT_NBLK)]
                                + [ai_s[jb, k0 + t] for jb in range(FFT_NBLK)], axis=0)
            t_scr[t] = _dot(m_ref[t], a)
        zt = _swap_major(t_scr[...])
        zr_ref[...] = zt[:GRID_W].astype(BF16)
        zi_ref[...] = zt[GRID_W:].astype(BF16)


def _fft_tables():
    n = GRID_W
    k = np.arange(n)
    ang1 = 2.0 * np.pi * ((k[:, None] * k[None, :]) % n) / n
    f1 = np.concatenate([np.cos(ang1), -np.sin(ang1)], axis=0) / 8.0
    ka = k[:, None, None]
    kb = k[None, :, None]
    n1 = k[None, None, :]
    ang2 = 2.0 * np.pi * ((n1 * (ka + n * kb)) % (n * n)) / (n * n)
    mr, mi = np.cos(ang2), -np.sin(ang2)
    m2 = np.concatenate([np.concatenate([mr, -mi], axis=2), np.concatenate([mi, mr], axis=2)], axis=1) / 8.0
    c = np.arange(FN_GROUP_DIM)
    angc = 2.0 * np.pi * ((c[:, None] * c[None, :]) % FN_GROUP_DIM) / FN_GROUP_DIM
    eye = np.eye(FN_GROUPS)
    cbd = np.kron(eye, np.cos(angc)) / 8.0
    sbd = np.kron(eye, np.sin(angc)) / 8.0
    return f1, m2, cbd, sbd


def _ctx_dft_table(ctx_len):
    k = np.arange(ctx_len)
    ang = 2.0 * np.pi * ((k[:, None] * k[None, :]) % ctx_len) / ctx_len
    return np.concatenate([np.cos(ang), -np.sin(ang)], axis=0) / np.sqrt(ctx_len)


def _fft(fu, f1, m2, n_batch, seq):
    n = GRID_W
    x3 = fu.reshape(-1, n, BW)
    shape3 = jax.ShapeDtypeStruct((n_batch * n, n, BW), BF16)
    z_spec = pl.BlockSpec((n, FFT_BLK, BW), lambda b, s: (b, jnp.maximum(s - FFT_NBLK, 0), 0))
    zr, zi = pl.pallas_call(
        _fft_kernel,
        grid=(n_batch, 2 * FFT_NBLK),
        in_specs=[pl.BlockSpec((2 * n, n), lambda b, s: (0, 0)),
                  pl.BlockSpec((FFT_BLK, 2 * n, 2 * n), lambda b, s: (jnp.maximum(s - FFT_NBLK, 0), 0, 0)),
                  pl.BlockSpec((n, FFT_BLK, BW), lambda b, s: (b, jnp.minimum(s, FFT_NBLK - 1), 0))],
        out_specs=[z_spec, z_spec],
        out_shape=[shape3, shape3],
        scratch_shapes=[pltpu.VMEM((FFT_NBLK, n, FFT_BLK, BW), BF16), pltpu.VMEM((FFT_NBLK, n, FFT_BLK, BW), BF16),
                        pltpu.VMEM((FFT_BLK, 2 * n, BW), F32)],
        compiler_params=_cparams(("arbitrary", "arbitrary")),
        name="fft",
    )(f1, m2, x3)
    return zr.reshape(n_batch * seq, BW), zi.reshape(n_batch * seq, BW)


def _ctx_kernel(sink_ref, qn_ref, kn_ref, vn_ref, fu_ref, qw_ref, kw_ref, vw_ref, dft_ref,
                a_ref, w_ref, zr_ref, zi_ref, *, ctx_len):
    for p in range(NA_HEADS // 2):
        sl = slice(p * LANES, (p + 1) * LANES)
        qs, lo = _stack_heads(qn_ref[:, sl], ctx_len)
        o = _attend(qs, [(kn_ref[:, sl], vn_ref[:, sl], None)])
        a_ref[:, sl] = jnp.where(lo, o[:ctx_len], o[ctx_len:]).astype(BF16)
    for g in range(WA_KV_HEADS):
        sl = slice(g * LANES, (g + 1) * LANES)
        sl2 = slice(g * 2 * LANES, (g + 1) * 2 * LANES)
        qs, lo = _stack_group(qw_ref[:, sl2])
        gq = WA_HEADS // WA_KV_HEADS
        o = _attend(qs, [(kw_ref[:, sl], vw_ref[:, sl], None)],
                    _sink_tile(sink_ref, range(g * gq, (g + 1) * gq), ctx_len))
        w_ref[:, sl2] = _unstack_group(o, ctx_len, lo).astype(BF16)
    z = _dot(dft_ref[...], fu_ref[...])
    zr_ref[...] = z[:ctx_len].astype(BF16)
    zi_ref[...] = z[ctx_len:].astype(BF16)


def _ctx_mix(sink, qn, kn, vn, fu, qw, kw, vw, dft_c, n_batch, seq, ctx_len):
    blk0 = (n_batch * seq) // ctx_len

    def in_spec(w):
        return pl.BlockSpec((ctx_len, w), lambda b: (blk0 + b, 0))

    out_spec = pl.BlockSpec((ctx_len, BW), lambda b: (b, 0))
    return pl.pallas_call(
        functools.partial(_ctx_kernel, ctx_len=ctx_len),
        grid=(n_batch,),
        in_specs=[pl.BlockSpec(memory_space=pltpu.SMEM),
                  in_spec(BW), in_spec(BW), in_spec(BW), in_spec(BW), in_spec(BW),
                  in_spec(2 * LANES), in_spec(2 * LANES),
                  pl.BlockSpec(dft_c.shape, lambda b: (0, 0))],
        out_specs=[out_spec] * 4,
        out_shape=[jax.ShapeDtypeStruct((n_batch * ctx_len, BW), BF16)] * 4,
        compiler_params=_cparams(("arbitrary",)),
        name="ctx_mix",
    )(sink, qn, kn, vn, fu, qw, kw, vw, dft_c)


N_BRANCH = 3
MERGE_WSTEPS = N_BRANCH * D // W_IN_CH
MERGE_BR_CH = N_BRANCH * BW // MERGE_WSTEPS
MERGE_OUT_CH = MERGE_BR_CH
MERGE_OUT_STEPS = D // MERGE_OUT_CH


def _merge_kernel(x_ref, g_ref, sh_ref, sc_ref, gt_ref, a_ref, zr_ref, zi_ref, w_ref, *rest, n_lat_tiles):
    if n_lat_tiles is None:
        wg_ref, wbr_ref, cbd_ref, sbd_ref, wout_ref, o_ref, wg_s, wbr_s, wout_s = rest
    else:
        (ac_ref, zrc_ref, zic_ref, wc_ref, wg_ref, wbr_ref, cbd_ref, sbd_ref, wout_ref, o_ref,
         wg_s, wbr_s, wout_s) = rest
    i = pl.program_id(0)

    @pl.when(i < MERGE_WSTEPS)
    def _():
        wg_s[i] = wg_ref[...].astype(BF16)
        wbr_s[i] = wbr_ref[...].astype(BF16)

    @pl.when(i < MERGE_OUT_STEPS)
    def _():
        wout_s[i] = wout_ref[...].astype(BF16)

    @pl.when(i >= MERGE_WSTEPS)
    def _():
        if n_lat_tiles is None:
            a, zr, zi, w = a_ref[...], zr_ref[...], zi_ref[...], w_ref[...]
        else:
            is_ctx = i - MERGE_WSTEPS >= n_lat_tiles
            a = jnp.where(is_ctx, ac_ref[...], a_ref[...])
            zr = jnp.where(is_ctx, zrc_ref[...], zr_ref[...])
            zi = jnp.where(is_ctx, zic_ref[...], zi_ref[...])
            w = jnp.where(is_ctx, wc_ref[...], w_ref[...])
        x = x_ref[...]
        u = _norm_mod(x, g_ref[...], sh_ref[...], sc_ref[...]).astype(BF16)
        f = (_dot(zr, cbd_ref[...]) + _dot(zi, sbd_ref[...])).astype(BF16)
        w_br = wbr_s[...].reshape(N_BRANCH * BW, D)
        acc = None
        for b, br in enumerate((a, f, w)):
            gate = _sigmoid(_dot(u, _chunk_cols(wg_s, b * D, (b + 1) * D, W_IN_CH)))
            term = gate * _dot(br, w_br[b * BW:(b + 1) * BW])
            acc = term if acc is None else acc + term
        o_ref[...] = x + gt_ref[...] * _dot(acc.astype(BF16), wout_s[...].reshape(D, D))


def _merge(h, n_tiles, g, mod3, layer, branches, ctx_branches, weights, tile_info):
    n_lat_tiles = tile_info[0]

    def tile_of(i):
        return jnp.maximum(i - MERGE_WSTEPS, 0)

    def wstep_of(i):
        return jnp.minimum(i, MERGE_WSTEPS - 1)

    tile_spec = pl.BlockSpec((TM, D), lambda i: (tile_of(i), 0))
    vec_spec = pl.BlockSpec((1, D), lambda i: (0, 0))
    has_ctx = ctx_branches is not None
    lat_spec = pl.BlockSpec((TM, BW), lambda i: (jnp.minimum(tile_of(i), n_lat_tiles - 1), 0))
    ctx_spec = pl.BlockSpec((TM, BW), lambda i: (jnp.maximum(tile_of(i) - n_lat_tiles, 0), 0))
    in_specs = [tile_spec, vec_spec] + [_mod_spec(layer, k, *tile_info, tile_of=tile_of) for k in (3, 4, 5)]
    in_specs += [lat_spec] * 4
    args = [h, g.reshape(1, D), mod3, mod3, mod3, *branches]
    if has_ctx:
        in_specs += [ctx_spec] * 4
        args += list(ctx_branches)
    w_in, w_br, cbd, sbd, w_out = weights
    gate_blk0 = PROJ_W // W_IN_CH
    in_specs += [
        pl.BlockSpec((None, D, W_IN_CH), lambda i: (layer, 0, gate_blk0 + wstep_of(i))),
        pl.BlockSpec((None, MERGE_BR_CH, D), lambda i: (layer, wstep_of(i), 0)),
        _const_spec(cbd.shape), _const_spec(sbd.shape),
        pl.BlockSpec((None, MERGE_OUT_CH, D), lambda i: (layer, jnp.minimum(i, MERGE_OUT_STEPS - 1), 0)),
    ]
    args += list(weights)
    return pl.pallas_call(
        functools.partial(_merge_kernel, n_lat_tiles=n_lat_tiles if has_ctx else None),
        grid=(MERGE_WSTEPS + n_tiles,),
        in_specs=in_specs,
        out_specs=tile_spec,
        out_shape=jax.ShapeDtypeStruct((n_tiles * TM, D), F32),
        scratch_shapes=[pltpu.VMEM((MERGE_WSTEPS, D, W_IN_CH), BF16),
                        pltpu.VMEM((MERGE_WSTEPS, MERGE_BR_CH, D), BF16),
                        pltpu.VMEM((MERGE_OUT_STEPS, MERGE_OUT_CH, D), BF16)],
        compiler_params=_cparams(("arbitrary",)),
        name="merge_ctx" if has_ctx else "merge",
    )(*args)


def _rope_tables(seq):
    t = np.arange(seq)
    row = (t // GRID_W).astype(np.float64)
    col = (t % GRID_W).astype(np.float64)
    n_freq = HD // 4
    inv = ROPE_BASE ** (-np.arange(n_freq, dtype=np.float64) / n_freq)
    ang = np.concatenate([row[:, None] * inv, col[:, None] * inv], axis=-1)
    cos, sin = np.cos(ang), np.sin(ang)
    cos_h = np.concatenate([cos, cos], axis=1)
    sin_h = np.concatenate([-sin, sin], axis=1)
    cos2 = np.concatenate([np.tile(cos_h, (1, 2)), np.ones((TM, LANES), np.float32)], axis=0)
    sin2 = np.concatenate([np.tile(sin_h, (1, 2)), np.zeros((TM, LANES), np.float32)], axis=0)
    return jnp.asarray(cos2, F32), jnp.asarray(sin2, F32)


def kernel(x, c, ctx, c_ctx, w_ada, b_ada, g_ffn1, ffn1_w13, ffn1_w2, g_mix, w_in, na_bias, wa_sink,
           w_br, w_out, g_ffn2, ffn2_w13, ffn2_w2, g_final):
    n_batch, seq, _ = x.shape
    ctx_len = ctx.shape[1]
    depth = w_ada.shape[0]
    rows = seq // GRID_W
    n_lat = n_batch * seq
    n_lat_tiles = n_lat // TM
    n_all_tiles = (n_lat + n_batch * ctx_len) // TM
    mod_rows = 8
    tile_info = (n_lat_tiles, seq // TM, n_batch, mod_rows)

    cc = jnp.concatenate([c, c_ctx[None], jnp.zeros((mod_rows - n_batch - 1, D), F32)], axis=0)
    mod3 = _ada(cc, w_ada, b_ada).reshape(depth * mod_rows, 1, N_MOD * D)

    cos_t, sin_t = _rope_tables(seq)
    f1, m2, cbd, sbd = (jnp.asarray(t, F32).astype(BF16) for t in _fft_tables())
    dft_c = jnp.asarray(_ctx_dft_table(ctx_len), F32).astype(BF16)

    w13a, w2a = ffn1_w13, ffn1_w2
    w13b, w2b = ffn2_w13, ffn2_w2
    w_in_b = w_in
    merge_w = (w_in, w_br.reshape(depth, N_BRANCH * BW, D), cbd, sbd, w_out)

    h = x.reshape(n_lat, D)
    h_ctx = ctx.reshape(n_batch * ctx_len, D)
    for l in range(depth):
        last = l == depth - 1
        bias_tab = _na_bias_table(na_bias[l], rows)

        h = _ffn(h, n_all_tiles, g_ffn1[l], mod3, l, 0, w13a, w2a, tile_info, h_ctx=h_ctx if l == 0 else None)
        qn, kn, vn, fu, qw, kw, vw = _proj(h, n_all_tiles, g_mix[l], mod3, l, w_in_b, cos_t, sin_t, tile_info)
        a = _na(qn, kn, vn, bias_tab, n_batch, seq, ctx_len)
        w = _wa(wa_sink[l], qw, kw, vw, n_batch, seq, ctx_len)
        zr, zi = _fft(fu, f1, m2, n_batch, seq)
        if last:
            h = _merge(h, n_lat_tiles, g_mix[l], mod3, l, (a, zr, zi, w), None, merge_w, tile_info)
            h = _ffn(h, n_lat_tiles, g_ffn2[l], mod3, l, 6, w13b, w2b, tile_info, g_final=g_final)
        else:
            ctx_br = _ctx_mix(wa_sink[l], qn, kn, vn, fu, qw, kw, vw, dft_c, n_batch, seq, ctx_len)
            ac, wc, zrc, zic = ctx_br
            h = _merge(h, n_all_tiles, g_mix[l], mod3, l, (a, zr, zi, w), (ac, zrc, zic, wc), merge_w, tile_info)
            h = _ffn(h, n_all_tiles, g_ffn2[l], mod3, l, 6, w13b, w2b, tile_info)
    return h.reshape(n_batch, seq, D)
```

```python
import functools

import numpy as np
import jax
import jax.numpy as jnp
from jax import lax
from jax.experimental import pallas as pl
from jax.experimental.pallas import tpu as pltpu

D = 1024
GRID_W = 64
HD = 64
NA_HEADS = 8
NA_WIN_R = 8
NA_WIN_C = 16
FN_GROUPS = 8
FN_GROUP_DIM = 64
WA_HEADS = 8
WA_KV_HEADS = 2
WA_WINDOW = 128
D_FF = 2816
N_MOD = 9
ROPE_BASE = 10000.0
EPS = 1e-6
NEG_INF = -1e30
LOG2E = 1.4426950408889634
BW = 512

LANES = 128
MXU_DIM = 256
TM = 512
TF = MXU_DIM
VMEM_LIMIT = 56 * 1024 * 1024

NA_QR = 4
NA_KR = 12
NA_RB = 2
NA_Q = NA_QR * GRID_W
NA_K = NA_KR * GRID_W
WA_Q = 128
WA_K = 3 * WA_Q
WA_QB = 16

F32 = jnp.float32
BF16 = jnp.bfloat16


def _cparams(sem):
    return pltpu.CompilerParams(dimension_semantics=sem, vmem_limit_bytes=VMEM_LIMIT)


def _const_spec(shape):
    nd = len(shape)
    return pl.BlockSpec(shape, lambda *_: (0,) * nd, pipeline_mode=pl.Buffered(1))


def _sigmoid(x):
    return 1.0 / (1.0 + jnp.exp(-x))


def _norm_mod(x, g, shift, scale):
    y = x * lax.rsqrt(jnp.mean(x * x, axis=-1, keepdims=True) + EPS)
    return y * (g * (1.0 + scale)) + shift


def _dot(a, b):
    return jnp.dot(a, b, preferred_element_type=F32)


def _dot_nt(a, b):
    return lax.dot_general(a, b, (((1,), (1,)), ((), ())), preferred_element_type=F32)


def _ada_kernel(c_ref, w_ref, b_ref, o_ref):
    x = c_ref[...]
    sx = (x * _sigmoid(x)).astype(BF16)
    o_ref[...] = _dot(sx, w_ref[...].astype(BF16)) + b_ref[...]


def _ada(cc, w_ada, b_ada):
    depth = w_ada.shape[0]
    n = w_ada.shape[2]
    tn = 1536
    rows = cc.shape[0]
    return pl.pallas_call(
        _ada_kernel,
        grid=(depth, n // tn),
        in_specs=[
            pl.BlockSpec((rows, D), lambda l, j: (0, 0)),
            pl.BlockSpec((None, D, tn), lambda l, j: (l, 0, j)),
            pl.BlockSpec((None, 1, tn), lambda l, j: (l, 0, j)),
        ],
        out_specs=pl.BlockSpec((None, rows, tn), lambda l, j: (l, 0, j)),
        out_shape=jax.ShapeDtypeStruct((depth, rows, n), F32),
        compiler_params=_cparams(("arbitrary", "arbitrary")),
        name="ada",
    )(cc, w_ada, b_ada.reshape(depth, 1, n))


def _mod_spec(layer, k, n_lat_tiles, tiles_per_batch, n_batch, mod_rows, tile_of=lambda i: i):
    def idx(i):
        t = tile_of(i)
        row = jnp.where(t < n_lat_tiles, t // tiles_per_batch, n_batch)
        return (layer * mod_rows + row, 0, k)

    return pl.BlockSpec((None, 1, D), idx)


FFN_W13_CH = 2 * TF
FFN_WSTEPS = 2 * D_FF // FFN_W13_CH
FFN_W2_CH = D_FF // FFN_WSTEPS


def _ffn_kernel(*refs, final, split_at):
    refs = list(refs)
    x_ref = refs.pop(0)
    if split_at is not None:
        xc_ref = refs.pop(0)
    g_ref, sh_ref, sc_ref, gt_ref, w13_ref, w2_ref = refs[:6]
    if final:
        gf_ref, o_ref, w13_s, w2_s, hm_ref = refs[6:]
    else:
        o_ref, w13_s, w2_s, hm_ref = refs[6:]
    i = pl.program_id(0)

    @pl.when(i < FFN_WSTEPS)
    def _():
        w13_s[i] = w13_ref[...].astype(BF16)
        w2_s[i] = w2_ref[...].astype(BF16)

    def w13_cols(lo):
        off = lo % FFN_W13_CH
        return w13_s[lo // FFN_W13_CH][:, off:off + TF]

    @pl.when(i >= FFN_WSTEPS)
    def _():
        x = x_ref[...]
        if split_at is not None:
            x = jnp.where(i - FFN_WSTEPS >= split_at, xc_ref[...], x)
        u = _norm_mod(x, g_ref[...], sh_ref[...], sc_ref[...]).astype(BF16)
        for lo in range(0, D_FF, TF):
            a = _dot(u, w13_cols(lo))
            b = _dot(u, w13_cols(D_FF + lo))
            hm_ref[:, lo:lo + TF] = (a * _sigmoid(a) * b).astype(BF16)
        f = _dot(hm_ref[...], w2_s[...].reshape(D_FF, D))
        out = x + 0.5 * gt_ref[...] * f
        if final:
            out = out * lax.rsqrt(jnp.mean(out * out, axis=-1, keepdims=True) + EPS) * gf_ref[...]
        o_ref[...] = out


def _ffn(h, n_tiles, g, mod3, layer, mod_k, w13, w2, tile_info, g_final=None, h_ctx=None):
    final = g_final is not None
    n_lat_tiles = tile_info[0]

    def tile_of(i):
        return jnp.maximum(i - FFN_WSTEPS, 0)

    def wstep_of(i):
        return jnp.minimum(i, FFN_WSTEPS - 1)

    tile_spec = pl.BlockSpec((TM, D), lambda i: (tile_of(i), 0))
    vec_spec = pl.BlockSpec((1, D), lambda i: (0, 0))
    if h_ctx is None:
        in_specs, args = [tile_spec], [h]
    else:
        in_specs = [pl.BlockSpec((TM, D), lambda i: (jnp.minimum(tile_of(i), n_lat_tiles - 1), 0)),
                    pl.BlockSpec((TM, D), lambda i: (jnp.maximum(tile_of(i) - n_lat_tiles, 0), 0))]
        args = [h, h_ctx]
    in_specs += [
        vec_spec,
        _mod_spec(layer, mod_k, *tile_info, tile_of=tile_of),
        _mod_spec(layer, mod_k + 1, *tile_info, tile_of=tile_of),
        _mod_spec(layer, mod_k + 2, *tile_info, tile_of=tile_of),
        pl.BlockSpec((None, D, FFN_W13_CH), lambda i: (layer, 0, wstep_of(i))),
        pl.BlockSpec((None, FFN_W2_CH, D), lambda i: (layer, wstep_of(i), 0)),
    ]
    args += [g.reshape(1, D), mod3, mod3, mod3, w13, w2]
    if final:
        in_specs.append(vec_spec)
        args.append(g_final.reshape(1, D))
    return pl.pallas_call(
        functools.partial(_ffn_kernel, final=final, split_at=None if h_ctx is None else n_lat_tiles),
        grid=(FFN_WSTEPS + n_tiles,),
        in_specs=in_specs,
        out_specs=tile_spec,
        out_shape=jax.ShapeDtypeStruct((n_tiles * TM, D), F32),
        scratch_shapes=[pltpu.VMEM((FFN_WSTEPS, D, FFN_W13_CH), BF16),
                        pltpu.VMEM((FFN_WSTEPS, FFN_W2_CH, D), BF16),
                        pltpu.VMEM((TM, D_FF), BF16)],
        compiler_params=_cparams(("arbitrary",)),
        name="ffn_final" if final else ("ffn" if h_ctx is None else "ffn_split"),
    )(*args)


PROJ_W = 5 * BW + 2 * LANES


def _rope(t, cos, sin_signed, first_half):
    partner = jnp.where(first_half, pltpu.roll(t, LANES - HD // 2, axis=1), pltpu.roll(t, HD // 2, axis=1))
    return t * cos + partner * sin_signed


def _dup_heads(t, lo):
    sw = pltpu.roll(t, HD, axis=1)
    return jnp.where(lo, t, sw), jnp.where(lo, sw, t)


W_IN_CH = MXU_DIM
PROJ_WSTEPS = PROJ_W // W_IN_CH


def _chunk_cols(ws_ref, lo, hi, ch):
    parts = [ws_ref[c] for c in range(lo // ch, hi // ch)]
    return parts[0] if len(parts) == 1 else jnp.concatenate(parts, axis=1)


def _proj_kernel(x_ref, g_ref, sh_ref, sc_ref, w_ref, cos_ref, sin_ref,
                 qn_ref, kn_ref, vn_ref, fu_ref, qw_ref, kw_ref, vw_ref, w_s):
    i = pl.program_id(0)

    @pl.when(i < PROJ_WSTEPS)
    def _():
        w_s[i] = w_ref[...].astype(BF16)

    def w(lo, hi):
        return _chunk_cols(w_s, lo, hi, W_IN_CH)

    @pl.when(i >= PROJ_WSTEPS)
    def _():
        u = _norm_mod(x_ref[...], g_ref[...], sh_ref[...], sc_ref[...]).astype(BF16)
        scale = HD ** -0.5 * LOG2E
        cos = cos_ref[...]
        sin = sin_ref[...]
        lane = lax.broadcasted_iota(jnp.int32, (TM, LANES), 1)
        first_half = (lane & (HD - 1)) < (HD // 2)
        lo = lane < HD
        kv = _dot(u, w(5 * BW, PROJ_W))
        k0, k1 = _dup_heads(_rope(kv[:, :LANES], cos, sin, first_half), lo)
        kw_ref[:, :LANES] = k0.astype(BF16)
        kw_ref[:, LANES:] = k1.astype(BF16)
        v0, v1 = _dup_heads(kv[:, LANES:], lo)
        vw_ref[:, :LANES] = v0.astype(BF16)
        vw_ref[:, LANES:] = v1.astype(BF16)
        wq = _dot(u, w(4 * BW, 5 * BW))
        for j in range(BW // LANES):
            sl = slice(j * LANES, (j + 1) * LANES)
            qw_ref[:, sl] = (_rope(wq[:, sl], cos, sin, first_half) * scale).astype(BF16)
        qn_ref[...] = (_dot(u, w(0, BW)) * scale).astype(BF16)
        kn_ref[...] = _dot(u, w(BW, 2 * BW)).astype(BF16)
        vn_ref[...] = _dot(u, w(2 * BW, 3 * BW)).astype(BF16)
        fu_ref[...] = _dot(u, w(3 * BW, 4 * BW)).astype(BF16)


def _proj(h, n_tiles, g, mod3, layer, w_in, cos_t, sin_t, tile_info):
    n_lat_tiles, tiles_per_batch, _, _ = tile_info

    def tile_of(i):
        return jnp.maximum(i - PROJ_WSTEPS, 0)

    def rope_blk(i):
        t = tile_of(i)
        return jnp.where(t < n_lat_tiles, t % tiles_per_batch, tiles_per_batch)

    tile_spec = pl.BlockSpec((TM, D), lambda i: (tile_of(i), 0))
    vec_spec = pl.BlockSpec((1, D), lambda i: (0, 0))
    rope_spec = pl.BlockSpec((TM, LANES), lambda i: (rope_blk(i), 0))
    w_spec = pl.BlockSpec((None, D, W_IN_CH), lambda i: (layer, 0, jnp.minimum(i, PROJ_WSTEPS - 1)))
    rows = n_tiles * TM

    def out_spec(w):
        return pl.BlockSpec((TM, w), lambda i: (tile_of(i), 0))

    widths = [BW, BW, BW, BW, BW, 2 * LANES, 2 * LANES]
    return pl.pallas_call(
        _proj_kernel,
        grid=(PROJ_WSTEPS + n_tiles,),
        in_specs=[tile_spec, vec_spec,
                  _mod_spec(layer, 3, *tile_info, tile_of=tile_of), _mod_spec(layer, 4, *tile_info, tile_of=tile_of),
                  w_spec, rope_spec, rope_spec],
        out_specs=[out_spec(w) for w in widths],
        out_shape=[jax.ShapeDtypeStruct((rows, w), BF16) for w in widths],
        scratch_shapes=[pltpu.VMEM((PROJ_WSTEPS, D, W_IN_CH), BF16)],
        compiler_params=_cparams(("arbitrary",)),
        name="proj",
    )(h, g.reshape(1, D), mod3, mod3, w_in, cos_t, sin_t)


def _lane_tiles(s):
    return [s[:, j * LANES:(j + 1) * LANES] for j in range(s.shape[1] // LANES)]


def _attend(qs, key_sets, sink_tile=None, n_slabs=1, dead=frozenset()):
    raw = [_dot_nt(qs, k) for k, _, _ in key_sets]
    rows_per_slab = qs.shape[0] // n_slabs
    p_rows = [[] for _ in key_sets]
    sink_terms = []
    for sl in range(n_slabs):
        rows = slice(sl * rows_per_slab, (sl + 1) * rows_per_slab)
        pieces = []
        for ks, (s, (_, _, bias)) in enumerate(zip(raw, key_sets)):
            for t in range(s.shape[1] // LANES):
                if (sl, ks, t) in dead:
                    continue
                cols = slice(t * LANES, (t + 1) * LANES)
                piece = s[rows, cols]
                if bias is not None:
                    piece = piece + bias[rows, cols]
                pieces.append((ks, t, piece))
        tiles = [pc for _, _, pc in pieces]
        if sink_tile is not None:
            tiles.append(sink_tile[rows])
        m = functools.reduce(jnp.maximum, tiles).max(axis=-1, keepdims=True)
        probs = {(ks, t): jnp.exp2(pc - m).astype(BF16) for ks, t, pc in pieces}
        if sink_tile is not None:
            sink_terms.append(jnp.exp2(sink_tile[rows] - m))
        for ks, s in enumerate(raw):
            blocks = [probs.get((ks, t), jnp.zeros((rows_per_slab, LANES), BF16)) for t in range(s.shape[1] // LANES)]
            p_rows[ks].append(jnp.concatenate(blocks, axis=1))
    acc = None
    for ks, (_, v, _) in enumerate(key_sets):
        v_ones = jnp.concatenate([v, jnp.ones_like(v)], axis=1)
        o = _dot(jnp.concatenate(p_rows[ks], axis=0), v_ones)
        acc = o if acc is None else acc + o
    num, den = acc[:, :LANES], acc[:, LANES:]
    if sink_tile is not None:
        den = den + (sink_terms[0] if n_slabs == 1 else jnp.concatenate(sink_terms, axis=0))
    return num / den


def _stack_heads(q, rows):
    lane = lax.broadcasted_iota(jnp.int32, (rows, LANES), 1)
    lo = lane < HD
    zero = jnp.zeros_like(q)
    return jnp.concatenate([jnp.where(lo, q, zero), jnp.where(lo, zero, q)], axis=0), lo


def _na_kernel(q_ref, k_ref, v_ref, kc_ref, vc_ref, bias_ref, o_ref, *, rows, interior_dead):
    step = pl.program_id(1)
    n_blk = rows // NA_QR

    def body(interior):
        n_slabs, dead = (NA_QR, interior_dead) if interior else (1, frozenset())
        for rb in range(NA_RB):
            j = step * NA_RB + rb
            krow = jnp.clip(NA_QR * j - NA_WIN_R // 2, 0, rows - NA_KR)
            kstart = pl.multiple_of(krow * GRID_W, GRID_W)
            pat = 1 if interior else jnp.where(j == 0, 0, jnp.where(j == n_blk - 1, 2, 1))
            qrows = slice(rb * NA_Q, (rb + 1) * NA_Q)
            for p in range(NA_HEADS // 2):
                sl = slice(p * LANES, (p + 1) * LANES)
                kb = k_ref[pl.ds(kstart, NA_K), sl]
                vb = v_ref[pl.ds(kstart, NA_K), sl]
                q = q_ref[qrows, sl]
                lo = lax.broadcasted_iota(jnp.int32, (NA_Q, LANES), 1) < HD
                outs = []
                for e in range(2):
                    qe = jnp.where(lo if e == 0 else jnp.logical_not(lo), q, jnp.zeros_like(q))
                    dead_e = frozenset((s - e * n_slabs, ks, t) for s, ks, t in dead if s // n_slabs == e)
                    outs.append(_attend(qe, [(kb, vb, bias_ref[pat, p, e * NA_Q:(e + 1) * NA_Q]),
                                             (kc_ref[:, sl], vc_ref[:, sl], None)], n_slabs=n_slabs, dead=dead_e))
                o_ref[qrows, sl] = jnp.where(lo, outs[0], outs[1]).astype(BF16)

    interior = jnp.logical_and(step > 0, step < n_blk // NA_RB - 1)

    @pl.when(interior)
    def _():
        body(True)

    @pl.when(jnp.logical_not(interior))
    def _():
        body(False)


def _na(qn, kn, vn, bias_tab, n_batch, seq, ctx_len):
    rows = seq // GRID_W
    n_steps = rows // (NA_QR * NA_RB)
    ctx_blk0 = (n_batch * seq) // ctx_len
    interior_dead = frozenset(
        (e * NA_QR + i, 0, t)
        for e in range(2) for i, pairs in enumerate(_na_plan(rows)[1]) for t, pair in enumerate(pairs)
        if pair == (NA_MASKED, NA_MASKED))
    return pl.pallas_call(
        functools.partial(_na_kernel, rows=rows, interior_dead=interior_dead),
        grid=(n_batch, n_steps),
        in_specs=[
            pl.BlockSpec((NA_RB * NA_Q, BW), lambda b, j: (b * n_steps + j, 0)),
            pl.BlockSpec((seq, BW), lambda b, j: (b, 0)),
            pl.BlockSpec((seq, BW), lambda b, j: (b, 0)),
            pl.BlockSpec((ctx_len, BW), lambda b, j: (ctx_blk0 + b, 0)),
            pl.BlockSpec((ctx_len, BW), lambda b, j: (ctx_blk0 + b, 0)),
            _const_spec(bias_tab.shape),
        ],
        out_specs=pl.BlockSpec((NA_RB * NA_Q, BW), lambda b, j: (b * n_steps + j, 0)),
        out_shape=jax.ShapeDtypeStruct((n_batch * seq, BW), BF16),
        compiler_params=_cparams(("arbitrary", "arbitrary")),
        name="na",
    )(qn, kn, vn, kn, vn, bias_tab)


NA_MASKED = 2 * NA_WIN_R - 1


def _na_plan(rows):
    n_blk = rows // NA_QR
    plan = []
    for blk in (0, 1, n_blk - 1):
        k0 = min(max(NA_QR * blk - NA_WIN_R // 2, 0), rows - NA_KR)
        per_row = []
        for i in range(NA_QR):
            r = NA_QR * blk + i
            rs = min(max(r - NA_WIN_R // 2, 0), rows - NA_WIN_R)
            slabs = [k0 + t - r + NA_WIN_R - 1 if rs <= k0 + t < rs + NA_WIN_R else NA_MASKED for t in range(NA_KR)]
            per_row.append([(slabs[2 * j], slabs[2 * j + 1]) for j in range(NA_KR // 2)])
        plan.append(per_row)
    return plan


def _na_bias_table(bias, rows):
    h = bias.shape[0]
    n_dc = 2 * NA_WIN_C - 1
    qc = np.arange(GRID_W)[:, None]
    kc = np.arange(GRID_W)[None, :]
    ws = np.clip(qc - NA_WIN_C // 2, 0, GRID_W - NA_WIN_C)
    col_ok = (kc >= ws) & (kc < ws + NA_WIN_C)
    dc = np.clip(kc - qc, -(NA_WIN_C - 1), NA_WIN_C - 1) + NA_WIN_C - 1
    onehot = (dc[None] == np.arange(n_dc)[:, None, None]).astype(np.float32)
    toep = jnp.einsum('hrd,dqk->hrqk', bias.astype(F32), jnp.asarray(onehot), precision=lax.Precision.HIGHEST)
    toep = jnp.where(col_ok[None, None], toep * LOG2E, NEG_INF)
    toep = jnp.concatenate([toep, jnp.full((h, 1, GRID_W, GRID_W), NEG_INF, F32)], axis=1)
    toep2 = jnp.concatenate([toep, toep], axis=-1)
    plan = _na_plan(rows)
    return pl.pallas_call(
        functools.partial(_bias_expand_kernel, plan=plan),
        out_shape=jax.ShapeDtypeStruct((len(plan), h // 2, 2 * NA_Q, NA_K), F32),
        compiler_params=pltpu.CompilerParams(vmem_limit_bytes=VMEM_LIMIT),
        name="na_bias_expand",
    )(toep2)


def _bias_expand_kernel(t_ref, o_ref, *, plan):
    lo = lax.broadcasted_iota(jnp.int32, (GRID_W, LANES), 1) < GRID_W
    for pat, per_row in enumerate(plan):
        for hd in range(t_ref.shape[0]):
            for i, pairs in enumerate(per_row):
                r0 = (hd % 2) * NA_Q + i * GRID_W
                for j, (da, db) in enumerate(pairs):
                    o_ref[pat, hd // 2, r0:r0 + GRID_W, j * LANES:(j + 1) * LANES] = (
                        jnp.where(lo, t_ref[hd, da], t_ref[hd, db]))


def _stack_group(q):
    rows = q.shape[0]
    a, lo = _stack_heads(q[:, :LANES], rows)
    b, _ = _stack_heads(q[:, LANES:], rows)
    return jnp.concatenate([a, b], axis=0), lo


def _unstack_group(o, rows, lo):
    oa = jnp.where(lo, o[0:rows], o[rows:2 * rows])
    ob = jnp.where(lo, o[2 * rows:3 * rows], o[3 * rows:4 * rows])
    return jnp.concatenate([oa, ob], axis=1)


def _sink_tile(sink_ref, heads, rows):
    return jnp.concatenate([jnp.full((rows, LANES), sink_ref[h] * LOG2E, F32) for h in heads], axis=0)


def _wa_kernel(sink_ref, q_ref, k_ref, v_ref, kc_ref, vc_ref, o_ref, *, seq):
    gq = WA_HEADS // WA_KV_HEADS
    for t in range(WA_QB):
        n = pl.program_id(1) * WA_QB + t
        rows = slice(t * WA_Q, (t + 1) * WA_Q)
        kstart = pl.multiple_of(jnp.clip(n * WA_Q - WA_Q, 0, seq - WA_K), WA_Q)
        qpos = n * WA_Q + lax.broadcasted_iota(jnp.int32, (WA_Q, WA_K), 0)
        kpos = kstart + lax.broadcasted_iota(jnp.int32, (WA_Q, WA_K), 1)
        band = jnp.where(jnp.abs(kpos - qpos) <= WA_WINDOW, 0.0, NEG_INF).astype(F32)
        band = jnp.concatenate([band] * gq, axis=0)
        for g in range(WA_KV_HEADS):
            sl = slice(g * LANES, (g + 1) * LANES)
            sl2 = slice(g * 2 * LANES, (g + 1) * 2 * LANES)
            kb = k_ref[pl.ds(kstart, WA_K), sl]
            vb = v_ref[pl.ds(kstart, WA_K), sl]
            qs, lo = _stack_group(q_ref[rows, sl2])
            o = _attend(qs, [(kb, vb, band), (kc_ref[:, sl], vc_ref[:, sl], None)],
                        _sink_tile(sink_ref, range(g * gq, (g + 1) * gq), WA_Q))
            o_ref[rows, sl2] = _unstack_group(o, WA_Q, lo).astype(BF16)


def _wa(sink, qw, kw, vw, n_batch, seq, ctx_len):
    n_blk = seq // (WA_Q * WA_QB)
    ctx_blk0 = (n_batch * seq) // ctx_len
    gw = 2 * LANES
    return pl.pallas_call(
        functools.partial(_wa_kernel, seq=seq),
        grid=(n_batch, n_blk),
        in_specs=[
            pl.BlockSpec(memory_space=pltpu.SMEM),
            pl.BlockSpec((WA_Q * WA_QB, BW), lambda b, n: (b * n_blk + n, 0)),
            pl.BlockSpec((seq, gw), lambda b, n: (b, 0)),
            pl.BlockSpec((seq, gw), lambda b, n: (b, 0)),
            pl.BlockSpec((ctx_len, gw), lambda b, n: (ctx_blk0 + b, 0)),
            pl.BlockSpec((ctx_len, gw), lambda b, n: (ctx_blk0 + b, 0)),
        ],
        out_specs=pl.BlockSpec((WA_Q * WA_QB, BW), lambda b, n: (b * n_blk + n, 0)),
        out_shape=jax.ShapeDtypeStruct((n_batch * seq, BW), BF16),
        compiler_params=_cparams(("arbitrary", "arbitrary")),
        name="wa",
    )(sink, qw, kw, vw, kw, vw)


FFT_BLK = 32


def _swap_major(x):
    return jnp.swapaxes(x, 0, 1)


FFT_NBLK = GRID_W // FFT_BLK


def _fft_kernel(f_ref, m_ref, x_ref, zr_ref, zi_ref, ar_s, ai_s, t_scr):
    s = pl.program_id(1)

    @pl.when(s < FFT_NBLK)
    def _():
        xt = _swap_major(x_ref[...].astype(F32)).astype(BF16)
        f = f_ref[...]
        for i in range(FFT_BLK):
            t_scr[i] = _dot(f, xt[i])
        at = _swap_major(t_scr[...])
        ar_s[s] = at[:GRID_W].astype(BF16)
        ai_s[s] = at[GRID_W:].astype(BF16)

    @pl.when(s >= FFT_NBLK)
    def _():
        k0 = (s - FFT_NBLK) * FFT_BLK
        for t in range(FFT_BLK):
            a = jnp.concatenate([ar_s[jb, k0 + t] for jb in range(FFT_NBLK)]
                                + [ai_s[jb, k0 + t] for jb in range(FFT_NBLK)], axis=0)
            t_scr[t] = _dot(m_ref[t], a)
        zt = _swap_major(t_scr[...])
        zr_ref[...] = zt[:GRID_W].astype(BF16)
        zi_ref[...] = zt[GRID_W:].astype(BF16)


def _fft_tables():
    n = GRID_W
    k = np.arange(n)
    ang1 = 2.0 * np.pi * ((k[:, None] * k[None, :]) % n) / n
    f1 = np.concatenate([np.cos(ang1), -np.sin(ang1)], axis=0) / 8.0
    ka = k[:, None, None]
    kb = k[None, :, None]
    n1 = k[None, None, :]
    ang2 = 2.0 * np.pi * ((n1 * (ka + n * kb)) % (n * n)) / (n * n)
    mr, mi = np.cos(ang2), -np.sin(ang2)
    m2 = np.concatenate([np.concatenate([mr, -mi], axis=2), np.concatenate([mi, mr], axis=2)], axis=1) / 8.0
    c = np.arange(FN_GROUP_DIM)
    angc = 2.0 * np.pi * ((c[:, None] * c[None, :]) % FN_GROUP_DIM) / FN_GROUP_DIM
    eye = np.eye(FN_GROUPS)
    cbd = np.kron(eye, np.cos(angc)) / 8.0
    sbd = np.kron(eye, np.sin(angc)) / 8.0
    return f1, m2, cbd, sbd


def _ctx_dft_table(ctx_len):
    k = np.arange(ctx_len)
    ang = 2.0 * np.pi * ((k[:, None] * k[None, :]) % ctx_len) / ctx_len
    return np.concatenate([np.cos(ang), -np.sin(ang)], axis=0) / np.sqrt(ctx_len)


def _fft(fu, f1, m2, n_batch, seq):
    n = GRID_W
    x3 = fu.reshape(-1, n, BW)
    shape3 = jax.ShapeDtypeStruct((n_batch * n, n, BW), BF16)
    z_spec = pl.BlockSpec((n, FFT_BLK, BW), lambda b, s: (b, jnp.maximum(s - FFT_NBLK, 0), 0))
    zr, zi = pl.pallas_call(
        _fft_kernel,
        grid=(n_batch, 2 * FFT_NBLK),
        in_specs=[pl.BlockSpec((2 * n, n), lambda b, s: (0, 0)),
                  pl.BlockSpec((FFT_BLK, 2 * n, 2 * n), lambda b, s: (jnp.maximum(s - FFT_NBLK, 0), 0, 0)),
                  pl.BlockSpec((n, FFT_BLK, BW), lambda b, s: (b, jnp.minimum(s, FFT_NBLK - 1), 0))],
        out_specs=[z_spec, z_spec],
        out_shape=[shape3, shape3],
        scratch_shapes=[pltpu.VMEM((FFT_NBLK, n, FFT_BLK, BW), BF16), pltpu.VMEM((FFT_NBLK, n, FFT_BLK, BW), BF16),
                        pltpu.VMEM((FFT_BLK, 2 * n, BW), F32)],
        compiler_params=_cparams(("arbitrary", "arbitrary")),
        name="fft",
    )(f1, m2, x3)
    return zr.reshape(n_batch * seq, BW), zi.reshape(n_batch * seq, BW)


def _ctx_kernel(sink_ref, qn_ref, kn_ref, vn_ref, fu_ref, qw_ref, kw_ref, vw_ref, dft_ref,
                a_ref, w_ref, zr_ref, zi_ref, *, ctx_len):
    for p in range(NA_HEADS // 2):
        sl = slice(p * LANES, (p + 1) * LANES)
        qs, lo = _stack_heads(qn_ref[:, sl], ctx_len)
        o = _attend(qs, [(kn_ref[:, sl], vn_ref[:, sl], None)])
        a_ref[:, sl] = jnp.where(lo, o[:ctx_len], o[ctx_len:]).astype(BF16)
    for g in range(WA_KV_HEADS):
        sl = slice(g * LANES, (g + 1) * LANES)
        sl2 = slice(g * 2 * LANES, (g + 1) * 2 * LANES)
        qs, lo = _stack_group(qw_ref[:, sl2])
        gq = WA_HEADS // WA_KV_HEADS
        o = _attend(qs, [(kw_ref[:, sl], vw_ref[:, sl], None)],
                    _sink_tile(sink_ref, range(g * gq, (g + 1) * gq), ctx_len))
        w_ref[:, sl2] = _unstack_group(o, ctx_len, lo).astype(BF16)
    z = _dot(dft_ref[...], fu_ref[...])
    zr_ref[...] = z[:ctx_len].astype(BF16)
    zi_ref[...] = z[ctx_len:].astype(BF16)


def _ctx_mix(sink, qn, kn, vn, fu, qw, kw, vw, dft_c, n_batch, seq, ctx_len):
    blk0 = (n_batch * seq) // ctx_len

    def in_spec(w):
        return pl.BlockSpec((ctx_len, w), lambda b: (blk0 + b, 0))

    out_spec = pl.BlockSpec((ctx_len, BW), lambda b: (b, 0))
    return pl.pallas_call(
        functools.partial(_ctx_kernel, ctx_len=ctx_len),
        grid=(n_batch,),
        in_specs=[pl.BlockSpec(memory_space=pltpu.SMEM),
                  in_spec(BW), in_spec(BW), in_spec(BW), in_spec(BW), in_spec(BW),
                  in_spec(2 * LANES), in_spec(2 * LANES),
                  pl.BlockSpec(dft_c.shape, lambda b: (0, 0))],
        out_specs=[out_spec] * 4,
        out_shape=[jax.ShapeDtypeStruct((n_batch * ctx_len, BW), BF16)] * 4,
        compiler_params=_cparams(("arbitrary",)),
        name="ctx_mix",
    )(sink, qn, kn, vn, fu, qw, kw, vw, dft_c)


N_BRANCH = 3
MERGE_WSTEPS = N_BRANCH * D // W_IN_CH
MERGE_BR_CH = N_BRANCH * BW // MERGE_WSTEPS
MERGE_OUT_CH = MERGE_BR_CH
MERGE_OUT_STEPS = D // MERGE_OUT_CH


def _merge_kernel(x_ref, g_ref, sh_ref, sc_ref, gt_ref, a_ref, zr_ref, zi_ref, w_ref, *rest, n_lat_tiles):
    if n_lat_tiles is None:
        wg_ref, wbr_ref, cbd_ref, sbd_ref, wout_ref, o_ref, wg_s, wbr_s, wout_s = rest
    else:
        (ac_ref, zrc_ref, zic_ref, wc_ref, wg_ref, wbr_ref, cbd_ref, sbd_ref, wout_ref, o_ref,
         wg_s, wbr_s, wout_s) = rest
    i = pl.program_id(0)

    @pl.when(i < MERGE_WSTEPS)
    def _():
        wg_s[i] = wg_ref[...].astype(BF16)
        wbr_s[i] = wbr_ref[...].astype(BF16)

    @pl.when(i < MERGE_OUT_STEPS)
    def _():
        wout_s[i] = wout_ref[...].astype(BF16)

    @pl.when(i >= MERGE_WSTEPS)
    def _():
        if n_lat_tiles is None:
            a, zr, zi, w = a_ref[...], zr_ref[...], zi_ref[...], w_ref[...]
        else:
            is_ctx = i - MERGE_WSTEPS >= n_lat_tiles
            a = jnp.where(is_ctx, ac_ref[...], a_ref[...])
            zr = jnp.where(is_ctx, zrc_ref[...], zr_ref[...])
            zi = jnp.where(is_ctx, zic_ref[...], zi_ref[...])
            w = jnp.where(is_ctx, wc_ref[...], w_ref[...])
        x = x_ref[...]
        u = _norm_mod(x, g_ref[...], sh_ref[...], sc_ref[...]).astype(BF16)
        f = (_dot(zr, cbd_ref[...]) + _dot(zi, sbd_ref[...])).astype(BF16)
        w_br = wbr_s[...].reshape(N_BRANCH * BW, D)
        acc = None
        for b, br in enumerate((a, f, w)):
            gate = _sigmoid(_dot(u, _chunk_cols(wg_s, b * D, (b + 1) * D, W_IN_CH)))
            term = gate * _dot(br, w_br[b * BW:(b + 1) * BW])
            acc = term if acc is None else acc + term
        o_ref[...] = x + gt_ref[...] * _dot(acc.astype(BF16), wout_s[...].reshape(D, D))


def _merge(h, n_tiles, g, mod3, layer, branches, ctx_branches, weights, tile_info):
    n_lat_tiles = tile_info[0]

    def tile_of(i):
        return jnp.maximum(i - MERGE_WSTEPS, 0)

    def wstep_of(i):
        return jnp.minimum(i, MERGE_WSTEPS - 1)

    tile_spec = pl.BlockSpec((TM, D), lambda i: (tile_of(i), 0))
    vec_spec = pl.BlockSpec((1, D), lambda i: (0, 0))
    has_ctx = ctx_branches is not None
    lat_spec = pl.BlockSpec((TM, BW), lambda i: (jnp.minimum(tile_of(i), n_lat_tiles - 1), 0))
    ctx_spec = pl.BlockSpec((TM, BW), lambda i: (jnp.maximum(tile_of(i) - n_lat_tiles, 0), 0))
    in_specs = [tile_spec, vec_spec] + [_mod_spec(layer, k, *tile_info, tile_of=tile_of) for k in (3, 4, 5)]
    in_specs += [lat_spec] * 4
    args = [h, g.reshape(1, D), mod3, mod3, mod3, *branches]
    if has_ctx:
        in_specs += [ctx_spec] * 4
        args += list(ctx_branches)
    w_in, w_br, cbd, sbd, w_out = weights
    gate_blk0 = PROJ_W // W_IN_CH
    in_specs += [
        pl.BlockSpec((None, D, W_IN_CH), lambda i: (layer, 0, gate_blk0 + wstep_of(i))),
        pl.BlockSpec((None, MERGE_BR_CH, D), lambda i: (layer, wstep_of(i), 0)),
        _const_spec(cbd.shape), _const_spec(sbd.shape),
        pl.BlockSpec((None, MERGE_OUT_CH, D), lambda i: (layer, jnp.minimum(i, MERGE_OUT_STEPS - 1), 0)),
    ]
    args += list(weights)
    return pl.pallas_call(
        functools.partial(_merge_kernel, n_lat_tiles=n_lat_tiles if has_ctx else None),
        grid=(MERGE_WSTEPS + n_tiles,),
        in_specs=in_specs,
        out_specs=tile_spec,
        out_shape=jax.ShapeDtypeStruct((n_tiles * TM, D), F32),
        scratch_shapes=[pltpu.VMEM((MERGE_WSTEPS, D, W_IN_CH), BF16),
                        pltpu.VMEM((MERGE_WSTEPS, MERGE_BR_CH, D), BF16),
                        pltpu.VMEM((MERGE_OUT_STEPS, MERGE_OUT_CH, D), BF16)],
        compiler_params=_cparams(("arbitrary",)),
        name="merge_ctx" if has_ctx else "merge",
    )(*args)


def _rope_tables(seq):
    t = np.arange(seq)
    row = (t // GRID_W).astype(np.float64)
    col = (t % GRID_W).astype(np.float64)
    n_freq = HD // 4
    inv = ROPE_BASE ** (-np.arange(n_freq, dtype=np.float64) / n_freq)
    ang = np.concatenate([row[:, None] * inv, col[:, None] * inv], axis=-1)
    cos, sin = np.cos(ang), np.sin(ang)
    cos_h = np.concatenate([cos, cos], axis=1)
    sin_h = np.concatenate([-sin, sin], axis=1)
    cos2 = np.concatenate([np.tile(cos_h, (1, 2)), np.ones((TM, LANES), np.float32)], axis=0)
    sin2 = np.concatenate([np.tile(sin_h, (1, 2)), np.zeros((TM, LANES), np.float32)], axis=0)
    return jnp.asarray(cos2, F32), jnp.asarray(sin2, F32)


def kernel(x, c, ctx, c_ctx, w_ada, b_ada, g_ffn1, ffn1_w13, ffn1_w2, g_mix, w_in, na_bias, wa_sink,
           w_br, w_out, g_ffn2, ffn2_w13, ffn2_w2, g_final):
    n_batch, seq, _ = x.shape
    ctx_len = ctx.shape[1]
    depth = w_ada.shape[0]
    rows = seq // GRID_W
    n_lat = n_batch * seq
    n_lat_tiles = n_lat // TM
    n_all_tiles = (n_lat + n_batch * ctx_len) // TM
    mod_rows = 8
    tile_info = (n_lat_tiles, seq // TM, n_batch, mod_rows)

    cc = jnp.concatenate([c, c_ctx[None], jnp.zeros((mod_rows - n_batch - 1, D), F32)], axis=0)
    mod3 = _ada(cc, w_ada, b_ada).reshape(depth * mod_rows, 1, N_MOD * D)

    cos_t, sin_t = _rope_tables(seq)
    f1, m2, cbd, sbd = (jnp.asarray(t, F32).astype(BF16) for t in _fft_tables())
    dft_c = jnp.asarray(_ctx_dft_table(ctx_len), F32).astype(BF16)

    w13a, w2a = ffn1_w13, ffn1_w2
    w13b, w2b = ffn2_w13, ffn2_w2
    w_in_b = w_in
    merge_w = (w_in, w_br.reshape(depth, N_BRANCH * BW, D), cbd, sbd, w_out)

    h = x.reshape(n_lat, D)
    h_ctx = ctx.reshape(n_batch * ctx_len, D)
    for l in range(depth):
        last = l == depth - 1
        bias_tab = _na_bias_table(na_bias[l], rows)

        h = _ffn(h, n_all_tiles, g_ffn1[l], mod3, l, 0, w13a, w2a, tile_info, h_ctx=h_ctx if l == 0 else None)
        qn, kn, vn, fu, qw, kw, vw = _proj(h, n_all_tiles, g_mix[l], mod3, l, w_in_b, cos_t, sin_t, tile_info)
        a = _na(qn, kn, vn, bias_tab, n_batch, seq, ctx_len)
        w = _wa(wa_sink[l], qw, kw, vw, n_batch, seq, ctx_len)
        zr, zi = _fft(fu, f1, m2, n_batch, seq)
        if last:
            h = _merge(h, n_lat_tiles, g_mix[l], mod3, l, (a, zr, zi, w), None, merge_w, tile_info)
            h = _ffn(h, n_lat_tiles, g_ffn2[l], mod3, l, 6, w13b, w2b, tile_info, g_final=g_final)
        else:
            ctx_br = _ctx_mix(wa_sink[l], qn, kn, vn, fu, qw, kw, vw, dft_c, n_batch, seq, ctx_len)
            ac, wc, zrc, zic = ctx_br
            h = _merge(h, n_all_tiles, g_mix[l], mod3, l, (a, zr, zi, w), (ac, zrc, zic, wc), merge_w, tile_info)
            h = _ffn(h, n_all_tiles, g_ffn2[l], mod3, l, 6, w13b, w2b, tile_info)
    return h.reshape(n_batch, seq, D)
```
